```python
import jax, jax.numpy as jnp
from jax import lax
import numpy as np

D_MODEL = 1024
BATCH = 16
SEQ = 2048
DEPTH = 2

GRID_W = 64
CTX_LEN = 256
HEAD_DIM = 64
NA_HEADS = 8
NA_WIDTH = NA_HEADS * HEAD_DIM
KH_MAX = 8
KW = 16
FN_GROUPS = 8
FN_GROUP_DIM = 64
FN_WIDTH = FN_GROUPS * FN_GROUP_DIM
AB_IN = 3 * NA_WIDTH + FN_WIDTH
AB_OUT = NA_WIDTH + FN_WIDTH
CONV_K = 3
PEER_HEADS = 8
PEER_NKEYS = 128
PEER_EXPERTS = PEER_NKEYS * PEER_NKEYS
PEER_DK = 256
PEER_DK_HALF = PEER_DK // 2
PEER_TOPK = 16
PEER_BLOCK = 128
EPS = 1e-6

kernel_name = "hybrid_natten_fnet_shortconv_peer_dit"


def rmsnorm(x, g):
    x32 = x.astype(jnp.float32)
    y = x32 * lax.rsqrt(jnp.mean(x32 * x32, axis=-1, keepdims=True) + EPS)
    return y.astype(x.dtype) * g


def adaln_chunks(cvec, w, b):
    m = jax.nn.silu(cvec) @ w + b
    return jnp.split(m.reshape(-1, 1, m.shape[-1]), 6, axis=-1)


def modulate(x, shift, scale):
    return x * (1 + scale) + shift


def split_heads(t):
    b, n, _ = t.shape
    return t.reshape(b, n, NA_HEADS, HEAD_DIM)


def neighbourhood_attention(q, k, v, k_ctx, v_ctx, rpb):
    b, s, h, dh = q.shape
    rows = s // GRID_W
    kh = min(KH_MAX, rows)
    qg = (q * dh ** -0.5).reshape(b, rows, GRID_W, h, dh)
    kg = k.reshape(b, rows, GRID_W, h, dh)
    vg = v.reshape(b, rows, GRID_W, h, dh)
    cols = np.arange(GRID_W)
    col_start = np.clip(cols - KW // 2, 0, GRID_W - KW)
    in_win = (cols[None, :] >= col_start[:, None]) & (cols[None, :] < col_start[:, None] + KW)
    col_off = np.clip(cols[None, :] - cols[:, None] + KW - 1, 0, 2 * KW - 2)
    rpb32 = rpb.astype(jnp.float32)
    n_win = kh * GRID_W

    def row_step(r):
        start = jnp.clip(r - kh // 2, 0, rows - kh)
        q_r = lax.dynamic_index_in_dim(qg, r, axis=1, keepdims=False)
        k_b = lax.dynamic_slice_in_dim(kg, start, kh, axis=1)
        v_b = lax.dynamic_slice_in_dim(vg, start, kh, axis=1)
        row_off = start + jnp.arange(kh) - r + KH_MAX - 1
        bias = rpb32[:, row_off[:, None, None], col_off[None]]
        bias = jnp.where(in_win[None, None], bias, -jnp.inf).transpose(0, 2, 1, 3)
        s_win = jnp.einsum('bqhd,brkhd->bhqrk', q_r, k_b).astype(jnp.float32) + bias
        s_ctx = jnp.einsum('bqhd,blhd->bhql', q_r, k_ctx).astype(jnp.float32)
        logits = jnp.concatenate([s_win.reshape(b, h, GRID_W, n_win), s_ctx], axis=-1)
        p = jax.nn.softmax(logits, axis=-1).astype(v.dtype)
        p_win = p[..., :n_win].reshape(b, h, GRID_W, kh, GRID_W)
        p_ctx = p[..., n_win:]
        return (jnp.einsum('bhqrk,brkhd->bqhd', p_win, v_b)
                + jnp.einsum('bhql,blhd->bqhd', p_ctx, v_ctx))

    out = lax.map(row_step, jnp.arange(rows))
    return out.transpose(1, 0, 2, 3, 4).reshape(b, s, h * dh)


def context_attention(q, k, v):
    b, n, h, dh = q.shape
    s = jnp.einsum('bqhd,bkhd->bhqk', q * dh ** -0.5, k).astype(jnp.float32)
    p = jax.nn.softmax(s, axis=-1).astype(v.dtype)
    return jnp.einsum('bhqk,bkhd->bqhd', p, v).reshape(b, n, h * dh)


def fourier_mix(f, w):
    b, n, _ = f.shape
    fg = f.reshape(b, n, FN_GROUPS, FN_GROUP_DIM).astype(jnp.float32)
    spec = jnp.fft.fft2(fg, axes=(1, 3), norm="ortho").real.astype(f.dtype)
    return jnp.einsum('bngc,gce->bnge', spec, w).reshape(b, n, FN_WIDTH)


def ab_mixer(h, h_ctx, w_in, w_out, rpb, fn_w, ctx_out):
    q, k, v, f = jnp.split(h @ w_in, [NA_WIDTH, 2 * NA_WIDTH, 3 * NA_WIDTH], axis=-1)
    kc, vc = jnp.split(h_ctx @ w_in[:, NA_WIDTH:3 * NA_WIDTH], 2, axis=-1)
    kc, vc = split_heads(kc), split_heads(vc)
    a = neighbourhood_attention(split_heads(q), split_heads(k), split_heads(v), kc, vc, rpb)
    out = jnp.concatenate([a, fourier_mix(f, fn_w)], axis=-1) @ w_out
    if not ctx_out:
        return out, None
    qc = split_heads(h_ctx @ w_in[:, :NA_WIDTH])
    fc = h_ctx @ w_in[:, 3 * NA_WIDTH:]
    out_c = jnp.concatenate([context_attention(qc, kc, vc), fourier_mix(fc, fn_w)], axis=-1) @ w_out
    return out, out_c


def conv_mixer(h, w_in, conv_w, w_out):
    bg, cg, v = jnp.split(h @ w_in, 3, axis=-1)
    u = jnp.pad(cg * v, ((0, 0), (1, 1), (0, 0)))
    y = conv_w[0] * u[:, :-2] + conv_w[1] * u[:, 1:-1] + conv_w[2] * u[:, 2:]
    return (bg * y) @ w_out


def peer(h, w_q, keys, down, up):
    shape = h.shape
    blocks = h.reshape(-1, PEER_BLOCK, shape[-1])

    def block_fn(hb):
        t = hb.shape[0]
        q = (hb @ w_q).reshape(t, PEER_HEADS, 2, PEER_DK_HALF)
        s = jnp.einsum('thpk,hpnk->thpn', q, keys).astype(jnp.float32)
        vals, idx = lax.top_k(s, PEER_TOPK)
        cand = (vals[:, :, 0, :, None] + vals[:, :, 1, None, :]).reshape(t, PEER_HEADS, PEER_TOPK * PEER_TOPK)
        cand_idx = (idx[:, :, 0, :, None] * PEER_NKEYS + idx[:, :, 1, None, :]).reshape(t, PEER_HEADS, PEER_TOPK * PEER_TOPK)
        top_s, pos = lax.top_k(cand, PEER_TOPK)
        eidx = jnp.take_along_axis(cand_idx, pos, axis=-1)
        g = jax.nn.softmax(top_s, axis=-1)
        act = jax.nn.gelu(jnp.einsum('td,thkd->thk', hb, down[eidx]).astype(jnp.float32), approximate=False)
        w_e = (g * act).astype(hb.dtype)
        return jnp.einsum('thk,thkd->td', w_e, up[eidx])

    return lax.map(block_fn, blocks).reshape(shape)


def setup_inputs(seed: int = 0) -> dict:
    key = jax.random.key(seed)
    ks = jax.random.split(key, 20)
    n_even = (DEPTH + 1) // 2
    n_odd = DEPTH // 2
    D = D_MODEL

    def nrm(k, shape, scale):
        return jax.random.normal(k, shape, jnp.float32) * scale

    return {
        "x": nrm(ks[0], (BATCH, SEQ, D), 1.0),
        "c": nrm(ks[1], (BATCH, D), 1.0),
        "ctx": nrm(ks[2], (BATCH, CTX_LEN, D), 1.0),
        "c_ctx": nrm(ks[3], (D,), 1.0),
        "ada_w": nrm(ks[4], (DEPTH, D, 6 * D), 0.5 * D ** -0.5),
        "ada_b": nrm(ks[5], (DEPTH, 6 * D), 0.02),
        "norm1_g": 1.0 + nrm(ks[6], (DEPTH, D), 0.02),
        "norm2_g": 1.0 + nrm(ks[7], (DEPTH, D), 0.02),
        "final_g": 1.0 + nrm(ks[8], (D,), 0.02),
        "ab_w_in": nrm(ks[9], (n_even, D, AB_IN), D ** -0.5),
        "ab_w_out": nrm(ks[10], (n_even, AB_OUT, D), AB_OUT ** -0.5),
        "na_rpb": nrm(ks[11], (n_even, NA_HEADS, 2 * KH_MAX - 1, 2 * KW - 1), 0.2),
        "fn_w": nrm(ks[12], (n_even, FN_GROUPS, FN_GROUP_DIM, FN_GROUP_DIM), FN_GROUP_DIM ** -0.5),
        "cv_w_in": nrm(ks[13], (n_odd, D, 3 * D), D ** -0.5),
        "cv_w": nrm(ks[14], (n_odd, CONV_K, D), CONV_K ** -0.5),
        "cv_w_out": nrm(ks[15], (n_odd, D, D), D ** -0.5),
        "peer_w_q": nrm(ks[16], (DEPTH, D, PEER_HEADS * PEER_DK), D ** -0.5),
        "peer_keys": nrm(ks[17], (DEPTH, PEER_HEADS, 2, PEER_NKEYS, PEER_DK_HALF), PEER_DK_HALF ** -0.5),
        "peer_down": nrm(ks[18], (DEPTH, PEER_EXPERTS, D), D ** -0.5),
        "peer_up": nrm(ks[19], (DEPTH, PEER_EXPERTS, D), PEER_HEADS ** -0.5),
    }


def reference(x, c, ctx, c_ctx, ada_w, ada_b, norm1_g, norm2_g, final_g,
              ab_w_in, ab_w_out, na_rpb, fn_w, cv_w_in, cv_w, cv_w_out,
              peer_w_q, peer_keys, peer_down, peer_up):
    for i in range(DEPTH):
        even = i % 2 == 0
        upd_ctx = any(j % 2 == 0 for j in range(i + 1, DEPTH))
        sh1, sc1, g1, sh2, sc2, g2 = adaln_chunks(c, ada_w[i], ada_b[i])
        h = modulate(rmsnorm(x, norm1_g[i]), sh1, sc1)
        if even or upd_ctx:
            csh1, csc1, cg1, csh2, csc2, cg2 = adaln_chunks(c_ctx, ada_w[i], ada_b[i])
            hc = modulate(rmsnorm(ctx, norm1_g[i]), csh1, csc1)
        if even:
            e = i // 2
            out, out_c = ab_mixer(h, hc, ab_w_in[e], ab_w_out[e], na_rpb[e], fn_w[e], upd_ctx)
        else:
            o = i // 2
            out = conv_mixer(h, cv_w_in[o], cv_w[o], cv_w_out[o])
            out_c = conv_mixer(hc, cv_w_in[o], cv_w[o], cv_w_out[o]) if upd_ctx else None
        x = x + g1 * out
        x = x + g2 * peer(modulate(rmsnorm(x, norm2_g[i]), sh2, sc2),
                          peer_w_q[i], peer_keys[i], peer_down[i], peer_up[i])
        if upd_ctx:
            ctx = ctx + cg1 * out_c
            ctx = ctx + cg2 * peer(modulate(rmsnorm(ctx, norm2_g[i]), csh2, csc2),
                                   peer_w_q[i], peer_keys[i], peer_down[i], peer_up[i])
    return rmsnorm(x, final_g)
```

```python
import functools

import jax
import jax.numpy as jnp
import numpy as np
from jax import lax
from jax.experimental import pallas as pl
from jax.experimental.pallas import tpu as pltpu

F32 = jnp.float32
BF16 = jnp.bfloat16
EPS = 1e-6

GRID_W = 64
HEAD_DIM = 64
NA_HEADS = 8
NA_WIDTH = NA_HEADS * HEAD_DIM
KH = 8
KH_MAX = 8
KW = 16
FN_GROUPS = 8
FN_GROUP_DIM = 64
FN_WIDTH = FN_GROUPS * FN_GROUP_DIM
PEER_HEADS = 8
PEER_NKEYS = 128
PEER_DK_HALF = 128
PEER_TOPK = 16
PEER_SEL = PEER_HEADS * PEER_TOPK
LANES = 128
SUBLANES = 8
VMEM_LIMIT = 56 * 1024 * 1024

NEG_INF = float("-inf")


def _cparams(n_axes):
    return pltpu.CompilerParams(dimension_semantics=("arbitrary",) * n_axes, vmem_limit_bytes=VMEM_LIMIT)


def _norm_mod(x, g, sh, sc):
    y = x * lax.rsqrt(jnp.mean(x * x, axis=-1, keepdims=True) + EPS)
    return (y * g) * (1.0 + sc) + sh


def _ada_kernel(c_ref, w_ref, b_ref, o_ref):
    c = c_ref[...]
    s = c / (1.0 + jnp.exp(-c))
    o_ref[0] = jnp.dot(s, w_ref[0], preferred_element_type=F32, precision=lax.Precision.HIGHEST) + b_ref[0]


def _ada_vectors(cc, ada_w, ada_b):
    depth, d, n = ada_w.shape
    rows = cc.shape[0]
    tn = 1536
    return pl.pallas_call(
        _ada_kernel,
        grid=(depth, n // tn),
        in_specs=[pl.BlockSpec((rows, d), lambda i, j: (0, 0)),
                  pl.BlockSpec((1, d, tn), lambda i, j: (i, 0, j)),
                  pl.BlockSpec((1, 1, tn), lambda i, j: (i, 0, j))],
        out_specs=pl.BlockSpec((1, rows, tn), lambda i, j: (i, 0, j)),
        out_shape=jax.ShapeDtypeStruct((depth, rows, n), F32),
        compiler_params=_cparams(2),
        name="ada_vectors",
    )(cc, ada_w, ada_b.reshape(depth, 1, n))


def _nmm_kernel(x_ref, g_ref, sh_ref, sc_ref, w_ref, *o_refs):
    h = _norm_mod(x_ref[0], g_ref[...], sh_ref[0], sc_ref[0])
    acc = jnp.dot(h.astype(BF16), w_ref[...], preferred_element_type=F32)
    off = 0
    for o_ref in o_refs:
        n = o_ref.shape[-1]
        o_ref[0] = acc[:, off:off + n].astype(o_ref.dtype)
        off += n


def _mod_spec(arr):
    d = arr.shape[-1]
    if arr.shape[0] == 1:
        return pl.BlockSpec((1, 1, d), lambda b, i: (0, 0, 0))
    return pl.BlockSpec((1, 1, d), lambda b, i: (b, 0, 0))


def _norm_mod_matmul(x, g, sh, sc, w, splits, dtypes, tm, name):
    b, s, d = x.shape
    n = w.shape[1]
    assert sum(splits) == n
    return pl.pallas_call(
        _nmm_kernel,
        grid=(b, s // tm),
        in_specs=[pl.BlockSpec((1, tm, d), lambda bi, i: (bi, i, 0)),
                  pl.BlockSpec((1, d), lambda bi, i: (0, 0)),
                  _mod_spec(sh), _mod_spec(sc),
                  pl.BlockSpec((d, n), lambda bi, i: (0, 0))],
        out_specs=[pl.BlockSpec((1, tm, k), lambda bi, i: (bi, i, 0)) for k in splits],
        out_shape=[jax.ShapeDtypeStruct((b, s, k), dt) for k, dt in zip(splits, dtypes)],
        compiler_params=_cparams(2),
        name=name,
    )(x, g.reshape(1, d), sh, sc, w)


def _attn_kernel(qkv_ref, ctx_ref, bias_ref, o_ref):
    rows = qkv_ref.shape[1] // GRID_W
    nt = (((1,), (1,)), ((), ()))
    scale = HEAD_DIM ** -0.5

    def row_step(r, carry):
        start = jnp.clip(r - KH // 2, 0, rows - KH)
        off = start - r + (KH_MAX - 1)
        q0 = pl.multiple_of(r * GRID_W, GRID_W)
        k0 = pl.multiple_of(start * GRID_W, GRID_W)
        outs = []
        for h in range(NA_HEADS):
            lo = h * HEAD_DIM
            q = qkv_ref[0, pl.ds(q0, GRID_W), lo:lo + HEAD_DIM]
            kw = qkv_ref[0, pl.ds(k0, KH * GRID_W), NA_WIDTH + lo:NA_WIDTH + lo + HEAD_DIM]
            vw = qkv_ref[0, pl.ds(k0, KH * GRID_W), 2 * NA_WIDTH + lo:2 * NA_WIDTH + lo + HEAD_DIM]
            kc = ctx_ref[0, :, lo:lo + HEAD_DIM]
            vc = ctx_ref[0, :, NA_WIDTH + lo:NA_WIDTH + lo + HEAD_DIM]
            s_win = lax.dot_general(q, kw, nt, preferred_element_type=F32) * scale + bias_ref[h, off]
            s_ctx = lax.dot_general(q, kc, nt, preferred_element_type=F32) * scale
            m = jnp.maximum(jnp.max(s_win, axis=-1, keepdims=True), jnp.max(s_ctx, axis=-1, keepdims=True))
            p_win = jnp.exp(s_win - m)
            p_ctx = jnp.exp(s_ctx - m)
            l = jnp.sum(p_win, axis=-1, keepdims=True) + jnp.sum(p_ctx, axis=-1, keepdims=True)
            o = (jnp.dot(p_win.astype(BF16), vw, preferred_element_type=F32)
                 + jnp.dot(p_ctx.astype(BF16), vc, preferred_element_type=F32))
            outs.append(o / l)
        o_ref[0, pl.ds(q0, GRID_W), :] = jnp.concatenate(outs, axis=-1).astype(o_ref.dtype)
        return carry

    lax.fori_loop(0, rows, row_step, 0)


def _attention_bias(rpb):
    cols = np.arange(GRID_W)
    col_start = np.clip(cols - KW // 2, 0, GRID_W - KW)
    in_win = (cols[None, :] >= col_start[:, None]) & (cols[None, :] < col_start[:, None] + KW)
    col_off = np.clip(cols[None, :] - cols[:, None] + KW - 1, 0, 2 * KW - 2)
    row_off = np.arange(KH)[:, None] + np.arange(KH)[None, :]
    b = rpb.astype(F32)[:, row_off[:, :, None, None], col_off[None, None]]
    b = jnp.where(in_win[None, None, None], b, NEG_INF)
    return b.transpose(0, 1, 3, 2, 4).reshape(NA_HEADS, KH, GRID_W, KH * GRID_W)


def _attention(qkv, ctx_kv, bias):
    b, s, _ = qkv.shape
    l = ctx_kv.shape[1]
    return pl.pallas_call(
        _attn_kernel,
        grid=(b,),
        in_specs=[pl.BlockSpec((1, s, 3 * NA_WIDTH), lambda i: (i, 0, 0)),
                  pl.BlockSpec((1, l, 2 * NA_WIDTH), lambda i: (i, 0, 0)),
                  pl.BlockSpec(bias.shape, lambda i: (0, 0, 0, 0), pipeline_mode=pl.Buffered(1))],
        out_specs=pl.BlockSpec((1, s, NA_WIDTH), lambda i: (i, 0, 0)),
        out_shape=jax.ShapeDtypeStruct((b, s, NA_WIDTH), BF16),
        compiler_params=_cparams(1),
        name="nbr_attention",
    )(qkv, ctx_kv, bias)


FN_ROW_CHUNK = 256


def _fnet_kernel(f_ref, cn_ref, sn_ref, c64_ref, s64_ref, wbd_ref, o_ref, a_scr, b_scr, xa_scr, xb_scr):
    n = f_ref.shape[1]
    scale = (n * FN_GROUP_DIM) ** -0.5

    @pl.when(pl.program_id(0) == 0)
    def _():
        hi = lax.Precision.HIGHEST
        a_scr[...] = (jnp.dot(c64_ref[...], wbd_ref[...], preferred_element_type=F32, precision=hi) * scale).astype(BF16)
        b_scr[...] = (jnp.dot(s64_ref[...], wbd_ref[...], preferred_element_type=F32, precision=hi) * -scale).astype(BF16)

    x = f_ref[0]
    xa_scr[...] = jnp.dot(x, a_scr[...], preferred_element_type=F32).astype(BF16)
    xb_scr[...] = jnp.dot(x, b_scr[...], preferred_element_type=F32).astype(BF16)

    def chunk(i, carry):
        r0 = pl.multiple_of(i * FN_ROW_CHUNK, FN_ROW_CHUNK)
        y = (jnp.dot(cn_ref[pl.ds(r0, FN_ROW_CHUNK), :], xa_scr[...], preferred_element_type=F32)
             + jnp.dot(sn_ref[pl.ds(r0, FN_ROW_CHUNK), :], xb_scr[...], preferred_element_type=F32))
        o_ref[0, pl.ds(r0, FN_ROW_CHUNK), :] = y.astype(o_ref.dtype)
        return carry

    lax.fori_loop(0, n // FN_ROW_CHUNK, chunk, 0)


def _dft_tables(n):
    k = (np.arange(n)[:, None] * np.arange(n)[None, :]) % n
    ang = 2.0 * np.pi * k.astype(np.float64) / n
    return np.cos(ang), np.sin(ang)


def _fnet(f, fn_w):
    b, s, _ = f.shape
    cn, sn = _dft_tables(s)
    c64, s64 = _dft_tables(FN_GROUP_DIM)
    eye = np.eye(FN_GROUPS)
    c64bd = jnp.asarray(np.kron(eye, c64), F32)
    s64bd = jnp.asarray(np.kron(eye, s64), F32)
    wbd = (jnp.asarray(eye, F32)[:, None, :, None] * fn_w[:, :, None, :]).reshape(FN_WIDTH, FN_WIDTH)
    const = lambda shape: pl.BlockSpec(shape, lambda i: (0, 0), pipeline_mode=pl.Buffered(1))
    return pl.pallas_call(
        _fnet_kernel,
        grid=(b,),
        in_specs=[pl.BlockSpec((1, s, FN_WIDTH), lambda i: (i, 0, 0)),
                  const((s, s)), const((s, s)),
                  const((FN_WIDTH, FN_WIDTH)), const((FN_WIDTH, FN_WIDTH)), const((FN_WIDTH, FN_WIDTH))],
        out_specs=pl.BlockSpec((1, s, FN_WIDTH), lambda i: (i, 0, 0)),
        out_shape=jax.ShapeDtypeStruct((b, s, FN_WIDTH), BF16),
        scratch_shapes=[pltpu.VMEM((FN_WIDTH, FN_WIDTH), BF16), pltpu.VMEM((FN_WIDTH, FN_WIDTH), BF16),
                        pltpu.VMEM((s, FN_WIDTH), BF16), pltpu.VMEM((s, FN_WIDTH), BF16)],
        compiler_params=_cparams(1),
        name="fnet_mix",
    )(f, jnp.asarray(cn, BF16), jnp.asarray(sn, BF16), c64bd, s64bd, wbd)


def _proj_res_kernel(a_ref, f_ref, x_ref, g_ref, w_ref, o_ref):
    ka = a_ref.shape[-1]
    out = (jnp.dot(a_ref[0], w_ref[:ka, :], preferred_element_type=F32)
           + jnp.dot(f_ref[0], w_ref[ka:, :], preferred_element_type=F32))
    o_ref[0] = x_ref[0] + g_ref[0] * out


def _proj_residual(a, f, x, gate, w, tm):
    b, s, d = x.shape
    ka, kf = a.shape[-1], f.shape[-1]
    return pl.pallas_call(
        _proj_res_kernel,
        grid=(b, s // tm),
        in_specs=[pl.BlockSpec((1, tm, ka), lambda bi, i: (bi, i, 0)),
                  pl.BlockSpec((1, tm, kf), lambda bi, i: (bi, i, 0)),
                  pl.BlockSpec((1, tm, d), lambda bi, i: (bi, i, 0)),
                  _mod_spec(gate),
                  pl.BlockSpec((ka + kf, d), lambda bi, i: (0, 0))],
        out_specs=pl.BlockSpec((1, tm, d), lambda bi, i: (bi, i, 0)),
        out_shape=jax.ShapeDtypeStruct((b, s, d), F32),
        compiler_params=_cparams(2),
        name="proj_residual",
    )(a, f, x, gate, w)


def _conv_in_kernel(x_ref, g_ref, sh_ref, sc_ref, w_ref, bg_ref, u_ref):
    d = bg_ref.shape[-1]
    h = _norm_mod(x_ref[0], g_ref[...], sh_ref[0], sc_ref[0])
    acc = jnp.dot(h.astype(BF16), w_ref[...], preferred_element_type=F32)
    bg_ref[0] = acc[:, :d]
    u_ref[0] = acc[:, d:2 * d] * acc[:, 2 * d:]


def _conv_in(x, g, sh, sc, w, tm):
    b, s, d = x.shape
    n = w.shape[1]
    return pl.pallas_call(
        _conv_in_kernel,
        grid=(b, s // tm),
        in_specs=[pl.BlockSpec((1, tm, d), lambda bi, i: (bi, i, 0)),
                  pl.BlockSpec((1, d), lambda bi, i: (0, 0)),
                  _mod_spec(sh), _mod_spec(sc),
                  pl.BlockSpec((d, n), lambda bi, i: (0, 0))],
        out_specs=[pl.BlockSpec((1, tm, d), lambda bi, i: (bi, i, 0))] * 2,
        out_shape=[jax.ShapeDtypeStruct((b, s, d), F32)] * 2,
        compiler_params=_cparams(2),
        name="conv_in",
    )(x, g.reshape(1, d), sh, sc, w)


def _conv_out_kernel(bg_ref, u_ref, up_ref, un_ref, cw_ref, x_ref, g_ref, w_ref, o_ref):
    i = pl.program_id(1)
    last = pl.num_programs(1) - 1
    u = u_ref[0]
    tm = u.shape[0]
    row = lax.broadcasted_iota(jnp.int32, u.shape, 0)
    prev_row = jnp.where(i == 0, 0.0, up_ref[0, SUBLANES - 1:SUBLANES, :])
    next_row = jnp.where(i == last, 0.0, un_ref[0, 0:1, :])
    u_prev = jnp.where(row == 0, prev_row, pltpu.roll(u, 1, axis=0))
    u_next = jnp.where(row == tm - 1, next_row, pltpu.roll(u, tm - 1, axis=0))
    y = cw_ref[0:1, :] * u_prev + cw_ref[1:2, :] * u + cw_ref[2:3, :] * u_next
    z = (bg_ref[0] * y).astype(BF16)
    o_ref[0] = x_ref[0] + g_ref[0] * jnp.dot(z, w_ref[...], preferred_element_type=F32)


def _conv_out(bg, u, cw, x, gate, w, tm):
    b, s, d = x.shape
    hb = tm // SUBLANES
    nhb = s // SUBLANES
    tile = pl.BlockSpec((1, tm, d), lambda bi, i: (bi, i, 0))
    return pl.pallas_call(
        _conv_out_kernel,
        grid=(b, s // tm),
        in_specs=[tile, tile,
                  pl.BlockSpec((1, SUBLANES, d), lambda bi, i: (bi, jnp.maximum(i * hb - 1, 0), 0)),
                  pl.BlockSpec((1, SUBLANES, d), lambda bi, i: (bi, jnp.minimum((i + 1) * hb, nhb - 1), 0)),
                  pl.BlockSpec(cw.shape, lambda bi, i: (0, 0)),
                  tile, _mod_spec(gate),
                  pl.BlockSpec((d, d), lambda bi, i: (0, 0))],
        out_specs=tile,
        out_shape=jax.ShapeDtypeStruct((b, s, d), F32),
        compiler_params=_cparams(2),
        name="conv_out",
    )(bg, u, u, u, cw, x, gate, w)


PEER_TM = 256
PEER_RC = 32


def _peer_score_kernel(x_ref, g_ref, sh_ref, sc_ref, wq_ref, keys_ref, h_ref, eidx_ref, gate_ref,
                       q_scr, s_scr):
    tm = x_ref.shape[1]
    nt = (((1,), (1,)), ((), ()))
    h = _norm_mod(x_ref[0], g_ref[...], sh_ref[0], sc_ref[0])
    h_ref[0] = h
    q = jnp.dot(h.astype(BF16), wq_ref[...], preferred_element_type=F32)
    for c in range(2 * PEER_HEADS):
        q_scr[c] = q[:, c * PEER_DK_HALF:(c + 1) * PEER_DK_HALF].astype(BF16)
    eidx_ref[0] = jnp.zeros((tm, PEER_SEL), jnp.int32)
    gate_ref[0] = jnp.zeros((tm, PEER_SEL), F32)

    lane = lax.broadcasted_iota(jnp.int32, (PEER_RC, LANES), 1)
    lane2 = lax.broadcasted_iota(jnp.int32, (PEER_RC, PEER_TOPK * PEER_TOPK), 1)
    sel_of_half = (lane2 // PEER_TOPK, lane2 % PEER_TOPK)

    def head_step(hd, carry):
        for p in range(2):
            s_scr[p] = lax.dot_general(q_scr[2 * hd + p], keys_ref[hd, p], nt, preferred_element_type=F32)

        def chunk(rc, carry2):
            r0 = pl.multiple_of(rc * PEER_RC, PEER_RC)
            cvals, cidxs = [], []
            for p in range(2):
                s = s_scr[p, pl.ds(r0, PEER_RC), :]
                cv = jnp.zeros(lane2.shape, F32)
                ci = jnp.zeros(lane2.shape, jnp.int32)
                for it in range(PEER_TOPK):
                    m = jnp.max(s, axis=-1, keepdims=True)
                    idx = jnp.min(jnp.where(s == m, lane, PEER_NKEYS), axis=-1, keepdims=True)
                    hit = sel_of_half[p] == it
                    cv = jnp.where(hit, m, cv)
                    ci = jnp.where(hit, idx, ci)
                    s = jnp.where(lane == idx, NEG_INF, s)
                cvals.append(cv)
                cidxs.append(ci)
            cand = cvals[0] + cvals[1]
            cidx = cidxs[0] * PEER_NKEYS + cidxs[1]
            m0 = jnp.max(cand, axis=-1, keepdims=True)
            e_acc = eidx_ref[0, pl.ds(r0, PEER_RC), :]
            g_acc = gate_ref[0, pl.ds(r0, PEER_RC), :]
            denom = jnp.zeros((PEER_RC, 1), F32)
            for it in range(PEER_TOPK):
                m = jnp.max(cand, axis=-1, keepdims=True)
                pos = jnp.min(jnp.where(cand == m, lane2, PEER_TOPK * PEER_TOPK), axis=-1, keepdims=True)
                hit = lane2 == pos
                e = jnp.sum(jnp.where(hit, cidx, 0), axis=-1, keepdims=True)
                ex = jnp.exp(m - m0)
                denom = denom + ex
                slot = lane == hd * PEER_TOPK + it
                e_acc = jnp.where(slot, e, e_acc)
                g_acc = jnp.where(slot, ex, g_acc)
                cand = jnp.where(hit, NEG_INF, cand)
            in_head = (lane // PEER_TOPK) == hd
            g_acc = jnp.where(in_head, g_acc / denom, g_acc)
            eidx_ref[0, pl.ds(r0, PEER_RC), :] = e_acc
            gate_ref[0, pl.ds(r0, PEER_RC), :] = g_acc
            return carry2

        lax.fori_loop(0, tm // PEER_RC, chunk, 0)
        return carry

    lax.fori_loop(0, PEER_HEADS, head_step, 0)


def _peer_score(x, g, sh, sc, wq, keys):
    b, s, d = x.shape
    tm = PEER_TM
    n = wq.shape[1]
    tile = lambda k: pl.BlockSpec((1, tm, k), lambda bi, i: (bi, i, 0))
    return pl.pallas_call(
        _peer_score_kernel,
        grid=(b, s // tm),
        in_specs=[tile(d),
                  pl.BlockSpec((1, d), lambda bi, i: (0, 0)),
                  _mod_spec(sh), _mod_spec(sc),
                  pl.BlockSpec((d, n), lambda bi, i: (0, 0)),
                  pl.BlockSpec(keys.shape, lambda bi, i: (0, 0, 0, 0))],
        out_specs=[tile(d), tile(PEER_SEL), tile(PEER_SEL)],
        out_shape=[jax.ShapeDtypeStruct((b, s, d), F32),
                   jax.ShapeDtypeStruct((b, s, PEER_SEL), jnp.int32),
                   jax.ShapeDtypeStruct((b, s, PEER_SEL), F32)],
        scratch_shapes=[pltpu.VMEM((2 * PEER_HEADS, tm, PEER_DK_HALF), BF16),
                        pltpu.VMEM((2, tm, PEER_NKEYS), F32)],
        compiler_params=_cparams(2),
        name="peer_score",
    )(x, g.reshape(1, d), sh, sc, wq, keys)


def _pack_table(t):
    e, d = t.shape
    bits = lax.bitcast_convert_type(t.astype(BF16), jnp.uint16).astype(jnp.uint32).reshape(e // 2, 2, d)
    return ((bits[:, 0] << 16) | bits[:, 1]).reshape(e // 2, d // LANES, LANES)


def _expert_row(tbl_ref, e):
    w = tbl_ref[e >> 1]
    sh = ((e & 1) * 16).astype(jnp.uint32)
    return pltpu.bitcast((w << sh) & jnp.uint32(0xFFFF0000), F32)


_FOLD_POS = (6, 2, 4, 0, 7, 3, 5, 1)


def _fold_sublanes(p):
    sub = lax.broadcasted_iota(jnp.int32, (SUBLANES, LANES), 0)
    m = [jnp.where(sub < 4, p[2 * k] + pltpu.roll(p[2 * k], 4, axis=0),
                   p[2 * k + 1] + pltpu.roll(p[2 * k + 1], 4, axis=0)) for k in range(4)]
    n = [jnp.where((sub & 2) != 0, m[2 * k] + pltpu.roll(m[2 * k], 2, axis=0),
                   m[2 * k + 1] + pltpu.roll(m[2 * k + 1], 6, axis=0)) for k in range(2)]
    return jnp.where((sub & 1) != 0, n[0] + pltpu.roll(n[0], 1, axis=0), n[1] + pltpu.roll(n[1], 7, axis=0))


DOWN_TM = 256


def _peer_down_kernel(eidx_ref, h_ref, gate_ref, tbl_ref, o_ref):
    tm = h_ref.shape[0]
    lane = lax.broadcasted_iota(jnp.int32, (SUBLANES, LANES), 1)

    def group(gi, carry):
        t0 = pl.multiple_of(gi * SUBLANES, SUBLANES)
        hs = [h_ref[t0 + s] for s in range(SUBLANES)]

        def pair(j, acc):
            prods = [None] * SUBLANES
            for s in range(SUBLANES):
                prods[_FOLD_POS[s]] = _expert_row(tbl_ref, eidx_ref[t0 + s, j]) * hs[s]
            tot = jnp.sum(_fold_sublanes(prods), axis=-1, keepdims=True)
            return jnp.where(lane == j, tot, acc)

        act = lax.fori_loop(0, PEER_SEL, pair, jnp.zeros((SUBLANES, LANES), F32))
        gelu = act * (lax.erf(act / np.sqrt(2).astype(np.float32)) + 1.0) / 2.0
        o_ref[pl.ds(t0, SUBLANES), :] = gate_ref[pl.ds(t0, SUBLANES), :] * gelu
        return carry

    lax.fori_loop(0, tm // SUBLANES, group, 0)


def _peer_down(eidx, h3, gate, tbl):
    t = eidx.shape[0]
    tm = DOWN_TM
    return pl.pallas_call(
        _peer_down_kernel,
        grid=(t // tm,),
        in_specs=[pl.BlockSpec((tm, PEER_SEL), lambda i: (i, 0), memory_space=pltpu.SMEM),
                  pl.BlockSpec((tm,) + h3.shape[1:], lambda i: (i, 0, 0)),
                  pl.BlockSpec((tm, PEER_SEL), lambda i: (i, 0)),
                  pl.BlockSpec(tbl.shape, lambda i: (0, 0, 0), pipeline_mode=pl.Buffered(1))],
        out_specs=pl.BlockSpec((tm, PEER_SEL), lambda i: (i, 0)),
        out_shape=jax.ShapeDtypeStruct((t, PEER_SEL), F32),
        compiler_params=_cparams(1),
        name="peer_down",
    )(eidx, h3, gate, tbl)


UP_TM = 128
UP_ACCS = 4


def _peer_up_kernel(eidx_ref, w_ref, tbl_ref, o_ref):
    tm = o_ref.shape[0]

    def token(t, carry):
        def step(jj, accs):
            j0 = jj * UP_ACCS
            return tuple(a + w_ref[t, j0 + u] * _expert_row(tbl_ref, eidx_ref[t, j0 + u])
                         for u, a in enumerate(accs))

        accs = lax.fori_loop(0, PEER_SEL // UP_ACCS, step,
                             tuple(jnp.zeros(o_ref.shape[1:], F32) for _ in range(UP_ACCS)))
        o_ref[t] = (accs[0] + accs[1]) + (accs[2] + accs[3])
        return carry

    lax.fori_loop(0, tm, token, 0)


def _peer_up(eidx, w, tbl):
    t = eidx.shape[0]
    tm = UP_TM
    smem = pl.BlockSpec((tm, PEER_SEL), lambda i: (i, 0), memory_space=pltpu.SMEM)
    return pl.pallas_call(
        _peer_up_kernel,
        grid=(t // tm,),
        in_specs=[smem, smem,
                  pl.BlockSpec(tbl.shape, lambda i: (0, 0, 0), pipeline_mode=pl.Buffered(1))],
        out_specs=pl.BlockSpec((tm,) + tbl.shape[1:], lambda i: (i, 0, 0)),
        out_shape=jax.ShapeDtypeStruct((t,) + tbl.shape[1:], F32),
        compiler_params=_cparams(1),
        name="peer_up",
    )(eidx, w, tbl)


def _residual_kernel(x_ref, p_ref, g_ref, o_ref):
    o_ref[0] = x_ref[0] + g_ref[0] * p_ref[0]


def _residual_norm_kernel(x_ref, p_ref, g_ref, fg_ref, o_ref):
    x = x_ref[0] + g_ref[0] * p_ref[0]
    o_ref[0] = (x * lax.rsqrt(jnp.mean(x * x, axis=-1, keepdims=True) + EPS)) * fg_ref[...]


def _residual(x, p, gate, final_g, tm):
    b, s, d = x.shape
    tile = pl.BlockSpec((1, tm, d), lambda bi, i: (bi, i, 0))
    in_specs = [tile, tile, _mod_spec(gate)]
    args = [x, p, gate]
    body = _residual_kernel
    if final_g is not None:
        in_specs.append(pl.BlockSpec((1, d), lambda bi, i: (0, 0)))
        args.append(final_g.reshape(1, d))
        body = _residual_norm_kernel
    return pl.pallas_call(
        body,
        grid=(b, s // tm),
        in_specs=in_specs,
        out_specs=tile,
        out_shape=jax.ShapeDtypeStruct((b, s, d), F32),
        compiler_params=_cparams(2),
        name="residual_norm" if final_g is not None else "residual",
    )(*args)


def _peer_block(x, g, sh, sc, gate2, wq, keys, down, up, final_g):
    b, s, d = x.shape
    t = b * s
    h, eidx, gsm = _peer_score(x, g, sh, sc, wq.astype(BF16), keys.astype(BF16))
    eidx = eidx.reshape(t, PEER_SEL)
    h3 = h.reshape(t, d // LANES, LANES)
    w = _peer_down(eidx, h3, gsm.reshape(t, PEER_SEL), _pack_table(down))
    po = _peer_up(eidx, w, _pack_table(up)).reshape(b, s, d)
    return _residual(x, po, gate2, final_g, 512)


def kernel(x, c, ctx, c_ctx, ada_w, ada_b, norm1_g, norm2_g, final_g, ab_w_in, ab_w_out, na_rpb, fn_w,
           cv_w_in, cv_w, cv_w_out, peer_w_q, peer_keys, peer_down, peer_up):
    b, s, d = x.shape
    depth = ada_w.shape[0]
    rows = -(-(b + 1) // SUBLANES) * SUBLANES
    cc = jnp.concatenate([c, c_ctx[None], jnp.zeros((rows - b - 1, d), F32)], axis=0)
    mod = _ada_vectors(cc, ada_w, ada_b)

    def chunks(i, lo, hi):
        m = mod[i, lo:hi].reshape(hi - lo, 1, 6, d)
        return [m[:, :, k] for k in range(6)]

    for i in range(depth):
        sh1, sc1, g1, sh2, sc2, g2 = chunks(i, 0, b)
        if i % 2 == 0:
            e = i // 2
            csh1, csc1 = chunks(i, b, b + 1)[:2]
            w_in = ab_w_in[e].astype(BF16)
            qkv, f = _norm_mod_matmul(x, norm1_g[i], sh1, sc1, w_in, (3 * NA_WIDTH, FN_WIDTH), (BF16, BF16),
                                      512, "ab_in")
            (ctx_kv,) = _norm_mod_matmul(ctx, norm1_g[i], csh1, csc1, w_in[:, NA_WIDTH:3 * NA_WIDTH],
                                         (2 * NA_WIDTH,), (BF16,), ctx.shape[1], "ab_in_ctx")
            a = _attention(qkv, ctx_kv, _attention_bias(na_rpb[e]))
            fm = _fnet(f, fn_w[e])
            x = _proj_residual(a, fm, x, g1, ab_w_out[e].astype(BF16), 512)
        else:
            o = i // 2
            bg, u = _conv_in(x, norm1_g[i], sh1, sc1, cv_w_in[o].astype(BF16), 256)
            x = _conv_out(bg, u, cv_w[o], x, g1, cv_w_out[o].astype(BF16), 512)
        x = _peer_block(x, norm2_g[i], sh2, sc2, g2, peer_w_q[i], peer_keys[i], peer_down[i], peer_up[i],
                        final_g if i == depth - 1 else None)
    return x
```

```python
import functools

import jax
import jax.numpy as jnp
import numpy as np
from jax import lax
from jax.experimental import pallas as pl
from jax.experimental.pallas import tpu as pltpu

F32 = jnp.float32
BF16 = jnp.bfloat16
EPS = 1e-6

GRID_W = 64
HEAD_DIM = 64
NA_HEADS = 8
NA_WIDTH = NA_HEADS * HEAD_DIM
KH = 8
KH_MAX = 8
KW = 16
FN_GROUPS = 8
FN_GROUP_DIM = 64
FN_WIDTH = FN_GROUPS * FN_GROUP_DIM
PEER_HEADS = 8
PEER_NKEYS = 128
PEER_DK_HALF = 128
PEER_TOPK = 16
PEER_SEL = PEER_HEADS * PEER_TOPK
LANES = 128
SUBLANES = 8
VMEM_LIMIT = 56 * 1024 * 1024

NEG_INF = float("-inf")


def _cparams(n_axes):
    return pltpu.CompilerParams(dimension_semantics=("arbitrary",) * n_axes, vmem_limit_bytes=VMEM_LIMIT)


def _norm_mod(x, g, sh, sc):
    y = x * lax.rsqrt(jnp.mean(x * x, axis=-1, keepdims=True) + EPS)
    return (y * g) * (1.0 + sc) + sh


def _ada_kernel(c_ref, w_ref, b_ref, o_ref):
    c = c_ref[...]
    s = c / (1.0 + jnp.exp(-c))
    o_ref[0] = jnp.dot(s, w_ref[0], preferred_element_type=F32, precision=lax.Precision.HIGHEST) + b_ref[0]


def _ada_vectors(cc, ada_w, ada_b):
    depth, d, n = ada_w.shape
    rows = cc.shape[0]
    tn = 1536
    return pl.pallas_call(
        _ada_kernel,
        grid=(depth, n // tn),
        in_specs=[pl.BlockSpec((rows, d), lambda i, j: (0, 0)),
                  pl.BlockSpec((1, d, tn), lambda i, j: (i, 0, j)),
                  pl.BlockSpec((1, 1, tn), lambda i, j: (i, 0, j))],
        out_specs=pl.BlockSpec((1, rows, tn), lambda i, j: (i, 0, j)),
        out_shape=jax.ShapeDtypeStruct((depth, rows, n), F32),
        compiler_params=_cparams(2),
        name="ada_vectors",
    )(cc, ada_w, ada_b.reshape(depth, 1, n))


def _nmm_kernel(x_ref, g_ref, sh_ref, sc_ref, w_ref, *o_refs):
    h = _norm_mod(x_ref[0], g_ref[...], sh_ref[0], sc_ref[0])
    acc = jnp.dot(h.astype(BF16), w_ref[...], preferred_element_type=F32)
    off = 0
    for o_ref in o_refs:
        n = o_ref.shape[-1]
        o_ref[0] = acc[:, off:off + n].astype(o_ref.dtype)
        off += n


def _mod_spec(arr):
    d = arr.shape[-1]
    if arr.shape[0] == 1:
        return pl.BlockSpec((1, 1, d), lambda b, i: (0, 0, 0))
    return pl.BlockSpec((1, 1, d), lambda b, i: (b, 0, 0))


def _norm_mod_matmul(x, g, sh, sc, w, splits, dtypes, tm, name):
    b, s, d = x.shape
    n = w.shape[1]
    assert sum(splits) == n
    return pl.pallas_call(
        _nmm_kernel,
        grid=(b, s // tm),
        in_specs=[pl.BlockSpec((1, tm, d), lambda bi, i: (bi, i, 0)),
                  pl.BlockSpec((1, d), lambda bi, i: (0, 0)),
                  _mod_spec(sh), _mod_spec(sc),
                  pl.BlockSpec((d, n), lambda bi, i: (0, 0))],
        out_specs=[pl.BlockSpec((1, tm, k), lambda bi, i: (bi, i, 0)) for k in splits],
        out_shape=[jax.ShapeDtypeStruct((b, s, k), dt) for k, dt in zip(splits, dtypes)],
        compiler_params=_cparams(2),
        name=name,
    )(x, g.reshape(1, d), sh, sc, w)


def _attn_kernel(qkv_ref, ctx_ref, bias_ref, o_ref):
    rows = qkv_ref.shape[1] // GRID_W
    nt = (((1,), (1,)), ((), ()))
    scale = HEAD_DIM ** -0.5

    def row_step(r, carry):
        start = jnp.clip(r - KH // 2, 0, rows - KH)
        off = start - r + (KH_MAX - 1)
        q0 = pl.multiple_of(r * GRID_W, GRID_W)
        k0 = pl.multiple_of(start * GRID_W, GRID_W)
        outs = []
        for h in range(NA_HEADS):
            lo = h * HEAD_DIM
            q = qkv_ref[0, pl.ds(q0, GRID_W), lo:lo + HEAD_DIM]
            kw = qkv_ref[0, pl.ds(k0, KH * GRID_W), NA_WIDTH + lo:NA_WIDTH + lo + HEAD_DIM]
            vw = qkv_ref[0, pl.ds(k0, KH * GRID_W), 2 * NA_WIDTH + lo:2 * NA_WIDTH + lo + HEAD_DIM]
            kc = ctx_ref[0, :, lo:lo + HEAD_DIM]
            vc = ctx_ref[0, :, NA_WIDTH + lo:NA_WIDTH + lo + HEAD_DIM]
            s_win = lax.dot_general(q, kw, nt, preferred_element_type=F32) * scale + bias_ref[h, off]
            s_ctx = lax.dot_general(q, kc, nt, preferred_element_type=F32) * scale
            m = jnp.maximum(jnp.max(s_win, axis=-1, keepdims=True), jnp.max(s_ctx, axis=-1, keepdims=True))
            p_win = jnp.exp(s_win - m)
            p_ctx = jnp.exp(s_ctx - m)
            l = jnp.sum(p_win, axis=-1, keepdims=True) + jnp.sum(p_ctx, axis=-1, keepdims=True)
            o = (jnp.dot(p_win.astype(BF16), vw, preferred_element_type=F32)
                 + jnp.dot(p_ctx.astype(BF16), vc, preferred_element_type=F32))
            outs.append(o / l)
        o_ref[0, pl.ds(q0, GRID_W), :] = jnp.concatenate(outs, axis=-1).astype(o_ref.dtype)
        return carry

    lax.fori_loop(0, rows, row_step, 0)


def _attention_bias(rpb):
    cols = np.arange(GRID_W)
    col_start = np.clip(cols - KW // 2, 0, GRID_W - KW)
    in_win = (cols[None, :] >= col_start[:, None]) & (cols[None, :] < col_start[:, None] + KW)
    col_off = np.clip(cols[None, :] - cols[:, None] + KW - 1, 0, 2 * KW - 2)
    onehot = jnp.asarray(col_off[None] == np.arange(2 * KW - 1)[:, None, None], F32)
    by_row = jnp.sum(rpb.astype(F32)[:, :, :, None, None] * onehot[None, None], axis=2)
    by_row = jnp.where(in_win[None, None], by_row, NEG_INF)
    b = jnp.stack([by_row[:, o:o + KH] for o in range(KH)], axis=1)
    return b.transpose(0, 1, 3, 2, 4).reshape(NA_HEADS, KH, GRID_W, KH * GRID_W)


def _attention(qkv, ctx_kv, bias):
    b, s, _ = qkv.shape
    l = ctx_kv.shape[1]
    return pl.pallas_call(
        _attn_kernel,
        grid=(b,),
        in_specs=[pl.BlockSpec((1, s, 3 * NA_WIDTH), lambda i: (i, 0, 0)),
                  pl.BlockSpec((1, l, 2 * NA_WIDTH), lambda i: (i, 0, 0)),
                  pl.BlockSpec(bias.shape, lambda i: (0, 0, 0, 0), pipeline_mode=pl.Buffered(1))],
        out_specs=pl.BlockSpec((1, s, NA_WIDTH), lambda i: (i, 0, 0)),
        out_shape=jax.ShapeDtypeStruct((b, s, NA_WIDTH), BF16),
        compiler_params=_cparams(1),
        name="nbr_attention",
    )(qkv, ctx_kv, bias)


FN_ROW_CHUNK = 256


def _fnet_kernel(f_ref, cn_ref, sn_ref, c64_ref, s64_ref, wbd_ref, o_ref, a_scr, b_scr, xa_scr, xb_scr):
    n = f_ref.shape[1]
    scale = (n * FN_GROUP_DIM) ** -0.5

    @pl.when(pl.program_id(0) == 0)
    def _():
        hi = lax.Precision.HIGHEST
        a_scr[...] = (jnp.dot(c64_ref[...], wbd_ref[...], preferred_element_type=F32, precision=hi) * scale).astype(BF16)
        b_scr[...] = (jnp.dot(s64_ref[...], wbd_ref[...], preferred_element_type=F32, precision=hi) * -scale).astype(BF16)

    x = f_ref[0]
    xa_scr[...] = jnp.dot(x, a_scr[...], preferred_element_type=F32).astype(BF16)
    xb_scr[...] = jnp.dot(x, b_scr[...], preferred_element_type=F32).astype(BF16)

    def chunk(i, carry):
        r0 = pl.multiple_of(i * FN_ROW_CHUNK, FN_ROW_CHUNK)
        y = (jnp.dot(cn_ref[pl.ds(r0, FN_ROW_CHUNK), :], xa_scr[...], preferred_element_type=F32)
             + jnp.dot(sn_ref[pl.ds(r0, FN_ROW_CHUNK), :], xb_scr[...], preferred_element_type=F32))
        o_ref[0, pl.ds(r0, FN_ROW_CHUNK), :] = y.astype(o_ref.dtype)
        return carry

    lax.fori_loop(0, n // FN_ROW_CHUNK, chunk, 0)


def _dft_tables(n):
    k = (np.arange(n)[:, None] * np.arange(n)[None, :]) % n
    ang = 2.0 * np.pi * k.astype(np.float64) / n
    return np.cos(ang), np.sin(ang)


def _fnet(f, fn_w):
    b, s, _ = f.shape
    cn, sn = _dft_tables(s)
    c64, s64 = _dft_tables(FN_GROUP_DIM)
    eye = np.eye(FN_GROUPS)
    c64bd = jnp.asarray(np.kron(eye, c64), F32)
    s64bd = jnp.asarray(np.kron(eye, s64), F32)
    wbd = (jnp.asarray(eye, F32)[:, None, :, None] * fn_w[:, :, None, :]).reshape(FN_WIDTH, FN_WIDTH)
    const = lambda shape: pl.BlockSpec(shape, lambda i: (0, 0), pipeline_mode=pl.Buffered(1))
    return pl.pallas_call(
        _fnet_kernel,
        grid=(b,),
        in_specs=[pl.BlockSpec((1, s, FN_WIDTH), lambda i: (i, 0, 0)),
                  const((s, s)), const((s, s)),
                  const((FN_WIDTH, FN_WIDTH)), const((FN_WIDTH, FN_WIDTH)), const((FN_WIDTH, FN_WIDTH))],
        out_specs=pl.BlockSpec((1, s, FN_WIDTH), lambda i: (i, 0, 0)),
        out_shape=jax.ShapeDtypeStruct((b, s, FN_WIDTH), BF16),
        scratch_shapes=[pltpu.VMEM((FN_WIDTH, FN_WIDTH), BF16), pltpu.VMEM((FN_WIDTH, FN_WIDTH), BF16),
                        pltpu.VMEM((s, FN_WIDTH), BF16), pltpu.VMEM((s, FN_WIDTH), BF16)],
        compiler_params=_cparams(1),
        name="fnet_mix",
    )(f, jnp.asarray(cn, BF16), jnp.asarray(sn, BF16), c64bd, s64bd, wbd)


def _proj_res_kernel(a_ref, f_ref, x_ref, g_ref, w_ref, o_ref):
    ka = a_ref.shape[-1]
    out = (jnp.dot(a_ref[0], w_ref[:ka, :], preferred_element_type=F32)
           + jnp.dot(f_ref[0], w_ref[ka:, :], preferred_element_type=F32))
    o_ref[0] = x_ref[0] + g_ref[0] * out


def _proj_residual(a, f, x, gate, w, tm):
    b, s, d = x.shape
    ka, kf = a.shape[-1], f.shape[-1]
    return pl.pallas_call(
        _proj_res_kernel,
        grid=(b, s // tm),
        in_specs=[pl.BlockSpec((1, tm, ka), lambda bi, i: (bi, i, 0)),
                  pl.BlockSpec((1, tm, kf), lambda bi, i: (bi, i, 0)),
                  pl.BlockSpec((1, tm, d), lambda bi, i: (bi, i, 0)),
                  _mod_spec(gate),
                  pl.BlockSpec((ka + kf, d), lambda bi, i: (0, 0))],
        out_specs=pl.BlockSpec((1, tm, d), lambda bi, i: (bi, i, 0)),
        out_shape=jax.ShapeDtypeStruct((b, s, d), F32),
        compiler_params=_cparams(2),
        name="proj_residual",
    )(a, f, x, gate, w)


def _conv_in_kernel(x_ref, g_ref, sh_ref, sc_ref, w_ref, bg_ref, u_ref):
    d = bg_ref.shape[-1]
    h = _norm_mod(x_ref[0], g_ref[...], sh_ref[0], sc_ref[0])
    acc = jnp.dot(h.astype(BF16), w_ref[...], preferred_element_type=F32)
    bg_ref[0] = acc[:, :d]
    u_ref[0] = acc[:, d:2 * d] * acc[:, 2 * d:]


def _conv_in(x, g, sh, sc, w, tm):
    b, s, d = x.shape
    n = w.shape[1]
    return pl.pallas_call(
        _conv_in_kernel,
        grid=(b, s // tm),
        in_specs=[pl.BlockSpec((1, tm, d), lambda bi, i: (bi, i, 0)),
                  pl.BlockSpec((1, d), lambda bi, i: (0, 0)),
                  _mod_spec(sh), _mod_spec(sc),
                  pl.BlockSpec((d, n), lambda bi, i: (0, 0))],
        out_specs=[pl.BlockSpec((1, tm, d), lambda bi, i: (bi, i, 0))] * 2,
        out_shape=[jax.ShapeDtypeStruct((b, s, d), F32)] * 2,
        compiler_params=_cparams(2),
        name="conv_in",
    )(x, g.reshape(1, d), sh, sc, w)


def _conv_out_kernel(bg_ref, u_ref, up_ref, un_ref, cw_ref, x_ref, g_ref, w_ref, o_ref):
    i = pl.program_id(1)
    last = pl.num_programs(1) - 1
    u = u_ref[0]
    tm = u.shape[0]
    row = lax.broadcasted_iota(jnp.int32, u.shape, 0)
    prev_row = jnp.where(i == 0, 0.0, up_ref[0, SUBLANES - 1:SUBLANES, :])
    next_row = jnp.where(i == last, 0.0, un_ref[0, 0:1, :])
    u_prev = jnp.where(row == 0, prev_row, pltpu.roll(u, 1, axis=0))
    u_next = jnp.where(row == tm - 1, next_row, pltpu.roll(u, tm - 1, axis=0))
    y = cw_ref[0:1, :] * u_prev + cw_ref[1:2, :] * u + cw_ref[2:3, :] * u_next
    z = (bg_ref[0] * y).astype(BF16)
    o_ref[0] = x_ref[0] + g_ref[0] * jnp.dot(z, w_ref[...], preferred_element_type=F32)


def _conv_out(bg, u, cw, x, gate, w, tm):
    b, s, d = x.shape
    hb = tm // SUBLANES
    nhb = s // SUBLANES
    tile = pl.BlockSpec((1, tm, d), lambda bi, i: (bi, i, 0))
    return pl.pallas_call(
        _conv_out_kernel,
        grid=(b, s // tm),
        in_specs=[tile, tile,
                  pl.BlockSpec((1, SUBLANES, d), lambda bi, i: (bi, jnp.maximum(i * hb - 1, 0), 0)),
                  pl.BlockSpec((1, SUBLANES, d), lambda bi, i: (bi, jnp.minimum((i + 1) * hb, nhb - 1), 0)),
                  pl.BlockSpec(cw.shape, lambda bi, i: (0, 0)),
                  tile, _mod_spec(gate),
                  pl.BlockSpec((d, d), lambda bi, i: (0, 0))],
        out_specs=tile,
        out_shape=jax.ShapeDtypeStruct((b, s, d), F32),
        compiler_params=_cparams(2),
        name="conv_out",
    )(bg, u, u, u, cw, x, gate, w)


PEER_TM = 256
KEY_TILES = PEER_NKEYS // SUBLANES
assert PEER_TOPK == 2 * SUBLANES


def _tree(op, xs):
    xs = list(xs)
    while len(xs) > 1:
        xs = [op(xs[i], xs[i + 1]) for i in range(0, len(xs) - 1, 2)] + ([xs[-1]] if len(xs) % 2 else [])
    return xs[0]


def _all_sublanes(op, x):
    for shift in (4, 2, 1):
        x = op(x, pltpu.roll(x, shift, axis=0))
    return x


def _top_keys(tiles, sub):
    key_id = [sub + SUBLANES * v for v in range(KEY_TILES)]
    out = []
    for _ in range(PEER_TOPK):
        m = _all_sublanes(jnp.maximum, _tree(jnp.maximum, tiles))
        idx = _all_sublanes(jnp.minimum, _tree(jnp.minimum, [jnp.where(t == m, k, PEER_NKEYS)
                                                             for t, k in zip(tiles, key_id)]))
        tiles = [jnp.where(k == idx, NEG_INF, t) for t, k in zip(tiles, key_id)]
        out.append((m, idx))
    return out


def _rows_of(ranked, sub, which, pick):
    t = ranked[pick(0)][which]
    for r in range(1, SUBLANES):
        t = jnp.where(sub == r, ranked[pick(r)][which], t)
    return t


def _product_key_topk(first, second, sub):
    def tiles(which):
        lo = _rows_of(second, sub, which, lambda r: r)
        hi = _rows_of(second, sub, which, lambda r: SUBLANES + r)
        quad = _rows_of(second, sub, which, lambda r: r % 4)
        a45 = _rows_of(first, sub, which, lambda r: 4 + r // 4)
        a67 = _rows_of(first, sub, which, lambda r: 6 + r // 4)
        ahi = _rows_of(first, sub, which, lambda r: SUBLANES + r)
        a = [first[k][which] for k in range(4)]
        return (a[0], lo), (a[0], hi), (a[1], lo), (a[2], lo), (a[3], lo), (a45, quad), (a67, quad), (ahi, second[0][which])

    cand = [x + y for x, y in tiles(0)]
    cidx = [x * PEER_NKEYS + y for x, y in tiles(1)]
    quad_pos = jnp.where(sub < 4, sub, sub + (PEER_TOPK - 4))
    pos = [sub, sub + 8, sub + 16, sub + 32, sub + 48, quad_pos + 64, quad_pos + 96, (sub + 8) * PEER_TOPK]
    out = []
    for _ in range(PEER_TOPK):
        m = _all_sublanes(jnp.maximum, _tree(jnp.maximum, cand))
        psel = _all_sublanes(jnp.minimum, _tree(jnp.minimum, [jnp.where(c == m, p, PEER_TOPK * PEER_TOPK)
                                                              for c, p in zip(cand, pos)]))
        hits = [p == psel for p in pos]
        e = _all_sublanes(jnp.maximum, _tree(jnp.maximum, [jnp.where(hh, x, -1) for hh, x in zip(hits, cidx)]))
        cand = [jnp.where(hh, NEG_INF, c) for hh, c in zip(hits, cand)]
        out.append((m, e))
    return out


def _peer_score_kernel(x_ref, g_ref, sh_ref, sc_ref, wq_ref, keys_ref, h_ref, eidx_ref, gate_ref,
                       q_scr, s_scr, e_scr, p_scr):
    tm = x_ref.shape[1]
    nt = (((1,), (1,)), ((), ()))
    h = _norm_mod(x_ref[0], g_ref[...], sh_ref[0], sc_ref[0])
    h_ref[0] = h
    q = jnp.dot(h.astype(BF16), wq_ref[...], preferred_element_type=F32)
    for c in range(2 * PEER_HEADS):
        q_scr[c] = q[:, c * PEER_DK_HALF:(c + 1) * PEER_DK_HALF].astype(BF16)
    sub = lax.broadcasted_iota(jnp.int32, (SUBLANES, LANES), 0)

    def head_step(hd, carry):
        for p in range(2):
            s_scr[p] = lax.dot_general(keys_ref[hd, p], q_scr[2 * hd + p], nt, preferred_element_type=F32)
        row0 = pl.multiple_of(hd * PEER_TOPK, PEER_TOPK)
        for lg in range(tm // LANES):
            cols = slice(lg * LANES, (lg + 1) * LANES)
            ranked = [_top_keys([s_scr[p, v * SUBLANES:(v + 1) * SUBLANES, cols] for v in range(KEY_TILES)], sub)
                      for p in range(2)]
            top = _product_key_topk(ranked[0], ranked[1], sub)
            ex = [(jnp.exp(m - top[0][0]), e) for m, e in top]
            denom = _tree(jnp.add, [v for v, _ in ex])
            for half in range(2):
                pick = lambda r, half=half: half * SUBLANES + r
                rows = pl.ds(row0 + half * SUBLANES, SUBLANES)
                e_scr[rows, cols] = _rows_of(ex, sub, 1, pick)
                p_scr[rows, cols] = _rows_of(ex, sub, 0, pick) / denom
        return carry

    lax.fori_loop(0, PEER_HEADS, head_step, 0)
    eidx_ref[0] = e_scr[...].T
    gate_ref[0] = p_scr[...].T


def _peer_score(x, g, sh, sc, wq, keys):
    b, s, d = x.shape
    tm = PEER_TM
    n = wq.shape[1]
    tile = lambda k: pl.BlockSpec((1, tm, k), lambda bi, i: (bi, i, 0))
    return pl.pallas_call(
        _peer_score_kernel,
        grid=(b, s // tm),
        in_specs=[tile(d),
                  pl.BlockSpec((1, d), lambda bi, i: (0, 0)),
                  _mod_spec(sh), _mod_spec(sc),
                  pl.BlockSpec((d, n), lambda bi, i: (0, 0)),
                  pl.BlockSpec(keys.shape, lambda bi, i: (0, 0, 0, 0))],
        out_specs=[tile(d), tile(PEER_SEL), tile(PEER_SEL)],
        out_shape=[jax.ShapeDtypeStruct((b, s, d), F32),
                   jax.ShapeDtypeStruct((b, s, PEER_SEL), jnp.int32),
                   jax.ShapeDtypeStruct((b, s, PEER_SEL), F32)],
        scratch_shapes=[pltpu.VMEM((2 * PEER_HEADS, tm, PEER_DK_HALF), BF16),
                        pltpu.VMEM((2, PEER_NKEYS, tm), F32),
                        pltpu.VMEM((PEER_SEL, tm), jnp.int32),
                        pltpu.VMEM((PEER_SEL, tm), F32)],
        compiler_params=_cparams(2),
        name="peer_score",
    )(x, g.reshape(1, d), sh, sc, wq, keys)


HALF_ROWS = SUBLANES // 2


def _pack_table(t):
    e, d = t.shape
    half = d // 2
    assert half == HALF_ROWS * LANES
    bits = lax.bitcast_convert_type(t.astype(BF16), jnp.uint16).astype(jnp.uint32)
    words = ((bits[:, :half] << 16) | bits[:, half:]).reshape(e * HALF_ROWS, LANES)
    pad = jnp.zeros((HALF_ROWS, LANES), jnp.uint32)
    return jnp.concatenate([pad, words, pad], axis=0)


def _load_rows(eidx, on_low_sublanes):
    return HALF_ROWS * eidx + jnp.where(on_low_sublanes, HALF_ROWS, 0)


def _unpack(words):
    return (pltpu.bitcast(words & jnp.uint32(0xFFFF0000), F32), pltpu.bitcast(words << 16, F32))


_FOLD_POS = (6, 2, 4, 0, 7, 3, 5, 1)


def _fold_halves(p, sub):
    m = [jnp.where(sub < HALF_ROWS, p[2 * k], p[2 * k + 1]) for k in range(4)]
    n = [jnp.where((sub & 2) != 0, m[2 * k] + pltpu.roll(m[2 * k], 2, axis=0),
                   m[2 * k + 1] + pltpu.roll(m[2 * k + 1], 6, axis=0)) for k in range(2)]
    return jnp.where((sub & 1) != 0, n[0] + pltpu.roll(n[0], 1, axis=0), n[1] + pltpu.roll(n[1], 7, axis=0))


DOWN_TM = 256
DOWN_UNROLL = 8


def _token_tiles(h_ref, t0, sub):
    chunks = [h_ref[pl.ds(t0, SUBLANES), r * LANES:(r + 1) * LANES] for r in range(SUBLANES)]
    tiles = []
    for s in range(SUBLANES):
        tile = None
        for r in range(SUBLANES):
            piece = jnp.broadcast_to(chunks[r][s:s + 1, :], (SUBLANES, LANES))
            tile = piece if tile is None else jnp.where(sub == r, piece, tile)
        tiles.append(tile)
    return tiles


def _peer_down_kernel(row_ref, h_ref, gate_ref, tbl_ref, o_ref):
    tm = h_ref.shape[0]
    lane = lax.broadcasted_iota(jnp.int32, (SUBLANES, LANES), 1)
    sub = lax.broadcasted_iota(jnp.int32, (SUBLANES, LANES), 0)

    def group(gi, carry):
        t0 = pl.multiple_of(gi * SUBLANES, SUBLANES)
        hs = []
        for s, tile in enumerate(_token_tiles(h_ref, t0, sub)):
            swapped = pltpu.roll(tile, HALF_ROWS, axis=0)
            hs.append((tile, swapped) if _FOLD_POS[s] % 2 == 0 else (swapped, tile))
        gbase = gi * (PEER_SEL * SUBLANES)

        def pair(j, carry2):
            acc, prev = carry2
            acc = jnp.where(lane == j - 1, jnp.sum(prev, axis=-1, keepdims=True), acc)
            base = gbase + j * SUBLANES
            prods = [None] * SUBLANES
            for s in range(SUBLANES):
                hi, lo = _unpack(tbl_ref[pl.ds(row_ref[base + s], SUBLANES), :])
                prods[_FOLD_POS[s]] = hi * hs[s][0] + lo * hs[s][1]
            return acc, _fold_halves(prods, sub)

        zero = jnp.zeros((SUBLANES, LANES), F32)
        acc, last = lax.fori_loop(0, PEER_SEL, pair, (zero, zero), unroll=DOWN_UNROLL)
        act = jnp.where(lane == PEER_SEL - 1, jnp.sum(last, axis=-1, keepdims=True), acc)
        gelu = act * (lax.erf(act / np.sqrt(2).astype(np.float32)) + 1.0) / 2.0
        o_ref[pl.ds(t0, SUBLANES), :] = gate_ref[pl.ds(t0, SUBLANES), :] * gelu
        return carry

    lax.fori_loop(0, tm // SUBLANES, group, 0)


def _peer_down(eidx, h, gate, tbl):
    t, d = h.shape
    tm = DOWN_TM
    low = (jnp.asarray(_FOLD_POS, jnp.int32) % 2 == 0)[None, :, None]
    rows = _load_rows(eidx.reshape(t // SUBLANES, SUBLANES, PEER_SEL), low)
    rows = rows.transpose(0, 2, 1).reshape(-1)
    return pl.pallas_call(
        _peer_down_kernel,
        grid=(t // tm,),
        in_specs=[pl.BlockSpec((tm * PEER_SEL,), lambda i: (i,), memory_space=pltpu.SMEM),
                  pl.BlockSpec((tm, d), lambda i: (i, 0)),
                  pl.BlockSpec((tm, PEER_SEL), lambda i: (i, 0)),
                  pl.BlockSpec(tbl.shape, lambda i: (0, 0), pipeline_mode=pl.Buffered(1))],
        out_specs=pl.BlockSpec((tm, PEER_SEL), lambda i: (i, 0)),
        out_shape=jax.ShapeDtypeStruct((t, PEER_SEL), F32),
        compiler_params=_cparams(1),
        name="peer_down",
    )(rows, h, gate, tbl)


UP_TM = 128
UP_PAIRS = 2
UP_UNROLL = 4


def _peer_up_kernel(row_ref, w_ref, tbl_ref, o_ref, stage_scr):
    tm = o_ref.shape[0]
    sub = lax.broadcasted_iota(jnp.int32, (SUBLANES, LANES), 0)
    low = sub < HALF_ROWS

    def token(t, carry):
        tbase = t * PEER_SEL
        t_in_group = t % SUBLANES

        def step(jj, accs):
            j0 = tbase + jj * (2 * UP_PAIRS)
            new = []
            for u in range(UP_PAIRS):
                ja = j0 + 2 * u
                words = jnp.where(low, tbl_ref[pl.ds(row_ref[ja], SUBLANES), :],
                                  tbl_ref[pl.ds(row_ref[ja + 1], SUBLANES), :])
                wv = jnp.where(low, w_ref[ja], w_ref[ja + 1])
                hi, lo = _unpack(words)
                new += [accs[2 * u] + wv * hi, accs[2 * u + 1] + wv * lo]
            return tuple(new)

        zero = jnp.zeros((SUBLANES, LANES), F32)
        accs = lax.fori_loop(0, PEER_SEL // (2 * UP_PAIRS), step, (zero,) * (2 * UP_PAIRS), unroll=UP_UNROLL)
        hi = _tree(jnp.add, accs[0::2])
        lo = _tree(jnp.add, accs[1::2])
        out = jnp.where(low, hi + pltpu.roll(hi, HALF_ROWS, axis=0), lo + pltpu.roll(lo, HALF_ROWS, axis=0))
        stage_scr[pl.ds(pl.multiple_of(t_in_group * SUBLANES, SUBLANES), SUBLANES), :] = out

        @pl.when(t_in_group == SUBLANES - 1)
        def _():
            t0 = pl.multiple_of(t - (SUBLANES - 1), SUBLANES)
            for r in range(SUBLANES):
                o_ref[pl.ds(t0, SUBLANES), r * LANES:(r + 1) * LANES] = stage_scr[pl.ds(r, SUBLANES, stride=SUBLANES), :]

        return carry

    lax.fori_loop(0, tm, token, 0)


def _peer_up(eidx, w, tbl, d):
    t = w.shape[0]
    tm = UP_TM
    rows = _load_rows(eidx, (jnp.arange(PEER_SEL) % 2 == 0)[None, :])
    smem = pl.BlockSpec((tm * PEER_SEL,), lambda i: (i,), memory_space=pltpu.SMEM)
    return pl.pallas_call(
        _peer_up_kernel,
        grid=(t // tm,),
        in_specs=[smem, smem,
                  pl.BlockSpec(tbl.shape, lambda i: (0, 0), pipeline_mode=pl.Buffered(1))],
        out_specs=pl.BlockSpec((tm, d), lambda i: (i, 0)),
        out_shape=jax.ShapeDtypeStruct((t, d), F32),
        scratch_shapes=[pltpu.VMEM((SUBLANES * SUBLANES, LANES), F32)],
        compiler_params=_cparams(1),
        name="peer_up",
    )(rows.reshape(-1), w.reshape(-1), tbl)


def _residual_kernel(x_ref, p_ref, g_ref, o_ref):
    o_ref[0] = x_ref[0] + g_ref[0] * p_ref[0]


def _residual_norm_kernel(x_ref, p_ref, g_ref, fg_ref, o_ref):
    x = x_ref[0] + g_ref[0] * p_ref[0]
    o_ref[0] = (x * lax.rsqrt(jnp.mean(x * x, axis=-1, keepdims=True) + EPS)) * fg_ref[...]


def _residual(x, p, gate, final_g, tm):
    b, s, d = x.shape
    tile = pl.BlockSpec((1, tm, d), lambda bi, i: (bi, i, 0))
    in_specs = [tile, tile, _mod_spec(gate)]
    args = [x, p, gate]
    body = _residual_kernel
    if final_g is not None:
        in_specs.append(pl.BlockSpec((1, d), lambda bi, i: (0, 0)))
        args.append(final_g.reshape(1, d))
        body = _residual_norm_kernel
    return pl.pallas_call(
        body,
        grid=(b, s // tm),
        in_specs=in_specs,
        out_specs=tile,
        out_shape=jax.ShapeDtypeStruct((b, s, d), F32),
        compiler_params=_cparams(2),
        name="residual_norm" if final_g is not None else "residual",
    )(*args)


def _peer_block(x, g, sh, sc, gate2, wq, keys, down, up, final_g):
    b, s, d = x.shape
    t = b * s
    h, eidx, gsm = _peer_score(x, g, sh, sc, wq.astype(BF16), keys.astype(BF16))
    eidx = eidx.reshape(t, PEER_SEL)
    w = _peer_down(eidx, h.reshape(t, d), gsm.reshape(t, PEER_SEL), _pack_table(down))
    po = _peer_up(eidx, w, _pack_table(up), d).reshape(b, s, d)
    return _residual(x, po, gate2, final_g, 512)


def kernel(x, c, ctx, c_ctx, ada_w, ada_b, norm1_g, norm2_g, final_g, ab_w_in, ab_w_out, na_rpb, fn_w,
           cv_w_in, cv_w, cv_w_out, peer_w_q, peer_keys, peer_down, peer_up):
    b, s, d = x.shape
    depth = ada_w.shape[0]
    rows = -(-(b + 1) // SUBLANES) * SUBLANES
    cc = jnp.concatenate([c, c_ctx[None], jnp.zeros((rows - b - 1, d), F32)], axis=0)
    mod = _ada_vectors(cc, ada_w, ada_b)

    def chunks(i, lo, hi):
        m = mod[i, lo:hi].reshape(hi - lo, 1, 6, d)
        return [m[:, :, k] for k in range(6)]

    for i in range(depth):
        sh1, sc1, g1, sh2, sc2, g2 = chunks(i, 0, b)
        if i % 2 == 0:
            e = i // 2
            csh1, csc1 = chunks(i, b, b + 1)[:2]
            w_in = ab_w_in[e].astype(BF16)
            qkv, f = _norm_mod_matmul(x, norm1_g[i], sh1, sc1, w_in, (3 * NA_WIDTH, FN_WIDTH), (BF16, BF16),
                                      512, "ab_in")
            (ctx_kv,) = _norm_mod_matmul(ctx, norm1_g[i], csh1, csc1, w_in[:, NA_WIDTH:3 * NA_WIDTH],
                                         (2 * NA_WIDTH,), (BF16,), ctx.shape[1], "ab_in_ctx")
            a = _attention(qkv, ctx_kv, _attention_bias(na_rpb[e]))
            fm = _fnet(f, fn_w[e])
            x = _proj_residual(a, fm, x, g1, ab_w_out[e].astype(BF16), 512)
        else:
            o = i // 2
            bg, u = _conv_in(x, norm1_g[i], sh1, sc1, cv_w_in[o].astype(BF16), 256)
            x = _conv_out(bg, u, cv_w[o], x, g1, cv_w_out[o].astype(BF16), 512)
        x = _peer_block(x, norm2_g[i], sh2, sc2, g2, peer_w_q[i], peer_keys[i], peer_down[i], peer_up[i],
                        final_g if i == depth - 1 else None)
    return x
```

```python
import functools

import jax
import jax.numpy as jnp
import numpy as np
from jax import lax
from jax.experimental import pallas as pl
from jax.experimental.pallas import tpu as pltpu

F32 = jnp.float32
BF16 = jnp.bfloat16
EPS = 1e-6

GRID_W = 64
HEAD_DIM = 64
NA_HEADS = 8
NA_WIDTH = NA_HEADS * HEAD_DIM
KH = 8
KH_MAX = 8
KW = 16
FN_GROUPS = 8
FN_GROUP_DIM = 64
FN_WIDTH = FN_GROUPS * FN_GROUP_DIM
PEER_HEADS = 8
PEER_NKEYS = 128
PEER_DK_HALF = 128
PEER_TOPK = 16
PEER_SEL = PEER_HEADS * PEER_TOPK
LANES = 128
SUBLANES = 8
VMEM_LIMIT = 56 * 1024 * 1024

NEG_INF = float("-inf")


def _cparams(n_axes):
    return pltpu.CompilerParams(dimension_semantics=("arbitrary",) * n_axes, vmem_limit_bytes=VMEM_LIMIT)


def _norm_mod(x, g, sh, sc):
    y = x * lax.rsqrt(jnp.mean(x * x, axis=-1, keepdims=True) + EPS)
    return (y * g) * (1.0 + sc) + sh


def _ada_kernel(c_ref, w_ref, b_ref, o_ref):
    c = c_ref[...]
    s = c / (1.0 + jnp.exp(-c))
    o_ref[0] = jnp.dot(s, w_ref[0], preferred_element_type=F32, precision=lax.Precision.HIGHEST) + b_ref[0]


def _ada_vectors(cc, ada_w, ada_b):
    depth, d, n = ada_w.shape
    rows = cc.shape[0]
    tn = 1536
    return pl.pallas_call(
        _ada_kernel,
        grid=(depth, n // tn),
        in_specs=[pl.BlockSpec((rows, d), lambda i, j: (0, 0)),
                  pl.BlockSpec((1, d, tn), lambda i, j: (i, 0, j)),
                  pl.BlockSpec((1, 1, tn), lambda i, j: (i, 0, j))],
        out_specs=pl.BlockSpec((1, rows, tn), lambda i, j: (i, 0, j)),
        out_shape=jax.ShapeDtypeStruct((depth, rows, n), F32),
        compiler_params=_cparams(2),
        name="ada_vectors",
    )(cc, ada_w, ada_b.reshape(depth, 1, n))


def _nmm_kernel(x_ref, g_ref, sh_ref, sc_ref, w_ref, *o_refs):
    h = _norm_mod(x_ref[0], g_ref[...], sh_ref[0], sc_ref[0])
    acc = jnp.dot(h.astype(BF16), w_ref[...], preferred_element_type=F32)
    off = 0
    for o_ref in o_refs:
        n = o_ref.shape[-1]
        o_ref[0] = acc[:, off:off + n].astype(o_ref.dtype)
        off += n


def _mod_spec(arr):
    d = arr.shape[-1]
    if arr.shape[0] == 1:
        return pl.BlockSpec((1, 1, d), lambda b, i: (0, 0, 0))
    return pl.BlockSpec((1, 1, d), lambda b, i: (b, 0, 0))


def _norm_mod_matmul(x, g, sh, sc, w, splits, dtypes, tm, name):
    b, s, d = x.shape
    n = w.shape[1]
    assert sum(splits) == n
    return pl.pallas_call(
        _nmm_kernel,
        grid=(b, s // tm),
        in_specs=[pl.BlockSpec((1, tm, d), lambda bi, i: (bi, i, 0)),
                  pl.BlockSpec((1, d), lambda bi, i: (0, 0)),
                  _mod_spec(sh), _mod_spec(sc),
                  pl.BlockSpec((d, n), lambda bi, i: (0, 0))],
        out_specs=[pl.BlockSpec((1, tm, k), lambda bi, i: (bi, i, 0)) for k in splits],
        out_shape=[jax.ShapeDtypeStruct((b, s, k), dt) for k, dt in zip(splits, dtypes)],
        compiler_params=_cparams(2),
        name=name,
    )(x, g.reshape(1, d), sh, sc, w)


def _attn_kernel(qkv_ref, ctx_ref, bias_ref, o_ref):
    rows = qkv_ref.shape[1] // GRID_W
    nt = (((1,), (1,)), ((), ()))
    scale = HEAD_DIM ** -0.5

    def row_step(r, carry):
        start = jnp.clip(r - KH // 2, 0, rows - KH)
        off = start - r + (KH_MAX - 1)
        q0 = pl.multiple_of(r * GRID_W, GRID_W)
        k0 = pl.multiple_of(start * GRID_W, GRID_W)
        outs = []
        for h in range(NA_HEADS):
            lo = h * HEAD_DIM
            q = qkv_ref[0, pl.ds(q0, GRID_W), lo:lo + HEAD_DIM]
            kw = qkv_ref[0, pl.ds(k0, KH * GRID_W), NA_WIDTH + lo:NA_WIDTH + lo + HEAD_DIM]
            vw = qkv_ref[0, pl.ds(k0, KH * GRID_W), 2 * NA_WIDTH + lo:2 * NA_WIDTH + lo + HEAD_DIM]
            kc = ctx_ref[0, :, lo:lo + HEAD_DIM]
            vc = ctx_ref[0, :, NA_WIDTH + lo:NA_WIDTH + lo + HEAD_DIM]
            s_win = lax.dot_general(q, kw, nt, preferred_element_type=F32) * scale + bias_ref[h, off]
            s_ctx = lax.dot_general(q, kc, nt, preferred_element_type=F32) * scale
            m = jnp.maximum(jnp.max(s_win, axis=-1, keepdims=True), jnp.max(s_ctx, axis=-1, keepdims=True))
            p_win = jnp.exp(s_win - m)
            p_ctx = jnp.exp(s_ctx - m)
            l = jnp.sum(p_win, axis=-1, keepdims=True) + jnp.sum(p_ctx, axis=-1, keepdims=True)
            o = (jnp.dot(p_win.astype(BF16), vw, preferred_element_type=F32)
                 + jnp.dot(p_ctx.astype(BF16), vc, preferred_element_type=F32))
            outs.append(o / l)
        o_ref[0, pl.ds(q0, GRID_W), :] = jnp.concatenate(outs, axis=-1).astype(o_ref.dtype)
        return carry

    lax.fori_loop(0, rows, row_step, 0)


def _attention_bias(rpb):
    cols = np.arange(GRID_W)
    col_start = np.clip(cols - KW // 2, 0, GRID_W - KW)
    in_win = (cols[None, :] >= col_start[:, None]) & (cols[None, :] < col_start[:, None] + KW)
    col_off = np.clip(cols[None, :] - cols[:, None] + KW - 1, 0, 2 * KW - 2)
    onehot = jnp.asarray(col_off[None] == np.arange(2 * KW - 1)[:, None, None], F32)
    by_row = jnp.sum(rpb.astype(F32)[:, :, :, None, None] * onehot[None, None], axis=2)
    by_row = jnp.where(in_win[None, None], by_row, NEG_INF)
    b = jnp.stack([by_row[:, o:o + KH] for o in range(KH)], axis=1)
    return b.transpose(0, 1, 3, 2, 4).reshape(NA_HEADS, KH, GRID_W, KH * GRID_W)


def _attention(qkv, ctx_kv, bias):
    b, s, _ = qkv.shape
    l = ctx_kv.shape[1]
    return pl.pallas_call(
        _attn_kernel,
        grid=(b,),
        in_specs=[pl.BlockSpec((1, s, 3 * NA_WIDTH), lambda i: (i, 0, 0)),
                  pl.BlockSpec((1, l, 2 * NA_WIDTH), lambda i: (i, 0, 0)),
                  pl.BlockSpec(bias.shape, lambda i: (0, 0, 0, 0), pipeline_mode=pl.Buffered(1))],
        out_specs=pl.BlockSpec((1, s, NA_WIDTH), lambda i: (i, 0, 0)),
        out_shape=jax.ShapeDtypeStruct((b, s, NA_WIDTH), BF16),
        compiler_params=_cparams(1),
        name="nbr_attention",
    )(qkv, ctx_kv, bias)


FN_ROW_CHUNK = 256


def _fnet_kernel(f_ref, cn_ref, sn_ref, c64_ref, s64_ref, wbd_ref, o_ref, a_scr, b_scr, xa_scr, xb_scr):
    n = f_ref.shape[1]
    scale = (n * FN_GROUP_DIM) ** -0.5

    @pl.when(pl.program_id(0) == 0)
    def _():
        hi = lax.Precision.HIGHEST
        a_scr[...] = (jnp.dot(c64_ref[...], wbd_ref[...], preferred_element_type=F32, precision=hi) * scale).astype(BF16)
        b_scr[...] = (jnp.dot(s64_ref[...], wbd_ref[...], preferred_element_type=F32, precision=hi) * -scale).astype(BF16)

    x = f_ref[0]
    xa_scr[...] = jnp.dot(x, a_scr[...], preferred_element_type=F32).astype(BF16)
    xb_scr[...] = jnp.dot(x, b_scr[...], preferred_element_type=F32).astype(BF16)

    def chunk(i, carry):
        r0 = pl.multiple_of(i * FN_ROW_CHUNK, FN_ROW_CHUNK)
        y = (jnp.dot(cn_ref[pl.ds(r0, FN_ROW_CHUNK), :], xa_scr[...], preferred_element_type=F32)
             + jnp.dot(sn_ref[pl.ds(r0, FN_ROW_CHUNK), :], xb_scr[...], preferred_element_type=F32))
        o_ref[0, pl.ds(r0, FN_ROW_CHUNK), :] = y.astype(o_ref.dtype)
        return carry

    lax.fori_loop(0, n // FN_ROW_CHUNK, chunk, 0)


def _dft_tables(n):
    k = (np.arange(n)[:, None] * np.arange(n)[None, :]) % n
    ang = 2.0 * np.pi * k.astype(np.float64) / n
    return np.cos(ang), np.sin(ang)


def _fnet(f, fn_w):
    b, s, _ = f.shape
    cn, sn = _dft_tables(s)
    c64, s64 = _dft_tables(FN_GROUP_DIM)
    eye = np.eye(FN_GROUPS)
    c64bd = jnp.asarray(np.kron(eye, c64), F32)
    s64bd = jnp.asarray(np.kron(eye, s64), F32)
    wbd = (jnp.asarray(eye, F32)[:, None, :, None] * fn_w[:, :, None, :]).reshape(FN_WIDTH, FN_WIDTH)
    const = lambda shape: pl.BlockSpec(shape, lambda i: (0, 0), pipeline_mode=pl.Buffered(1))
    return pl.pallas_call(
        _fnet_kernel,
        grid=(b,),
        in_specs=[pl.BlockSpec((1, s, FN_WIDTH), lambda i: (i, 0, 0)),
                  const((s, s)), const((s, s)),
                  const((FN_WIDTH, FN_WIDTH)), const((FN_WIDTH, FN_WIDTH)), const((FN_WIDTH, FN_WIDTH))],
        out_specs=pl.BlockSpec((1, s, FN_WIDTH), lambda i: (i, 0, 0)),
        out_shape=jax.ShapeDtypeStruct((b, s, FN_WIDTH), BF16),
        scratch_shapes=[pltpu.VMEM((FN_WIDTH, FN_WIDTH), BF16), pltpu.VMEM((FN_WIDTH, FN_WIDTH), BF16),
                        pltpu.VMEM((s, FN_WIDTH), BF16), pltpu.VMEM((s, FN_WIDTH), BF16)],
        compiler_params=_cparams(1),
        name="fnet_mix",
    )(f, jnp.asarray(cn, BF16), jnp.asarray(sn, BF16), c64bd, s64bd, wbd)


def _proj_res_kernel(a_ref, f_ref, x_ref, g_ref, w_ref, o_ref):
    ka = a_ref.shape[-1]
    out = (jnp.dot(a_ref[0], w_ref[:ka, :], preferred_element_type=F32)
           + jnp.dot(f_ref[0], w_ref[ka:, :], preferred_element_type=F32))
    o_ref[0] = x_ref[0] + g_ref[0] * out


def _proj_residual(a, f, x, gate, w, tm):
    b, s, d = x.shape
    ka, kf = a.shape[-1], f.shape[-1]
    return pl.pallas_call(
        _proj_res_kernel,
        grid=(b, s // tm),
        in_specs=[pl.BlockSpec((1, tm, ka), lambda bi, i: (bi, i, 0)),
                  pl.BlockSpec((1, tm, kf), lambda bi, i: (bi, i, 0)),
                  pl.BlockSpec((1, tm, d), lambda bi, i: (bi, i, 0)),
                  _mod_spec(gate),
                  pl.BlockSpec((ka + kf, d), lambda bi, i: (0, 0))],
        out_specs=pl.BlockSpec((1, tm, d), lambda bi, i: (bi, i, 0)),
        out_shape=jax.ShapeDtypeStruct((b, s, d), F32),
        compiler_params=_cparams(2),
        name="proj_residual",
    )(a, f, x, gate, w)


def _conv_in_kernel(x_ref, g_ref, sh_ref, sc_ref, w_ref, bg_ref, u_ref):
    d = bg_ref.shape[-1]
    h = _norm_mod(x_ref[0], g_ref[...], sh_ref[0], sc_ref[0])
    acc = jnp.dot(h.astype(BF16), w_ref[...], preferred_element_type=F32)
    bg_ref[0] = acc[:, :d]
    u_ref[0] = acc[:, d:2 * d] * acc[:, 2 * d:]


def _conv_in(x, g, sh, sc, w, tm):
    b, s, d = x.shape
    n = w.shape[1]
    return pl.pallas_call(
        _conv_in_kernel,
        grid=(b, s // tm),
        in_specs=[pl.BlockSpec((1, tm, d), lambda bi, i: (bi, i, 0)),
                  pl.BlockSpec((1, d), lambda bi, i: (0, 0)),
                  _mod_spec(sh), _mod_spec(sc),
                  pl.BlockSpec((d, n), lambda bi, i: (0, 0))],
        out_specs=[pl.BlockSpec((1, tm, d), lambda bi, i: (bi, i, 0))] * 2,
        out_shape=[jax.ShapeDtypeStruct((b, s, d), F32)] * 2,
        compiler_params=_cparams(2),
        name="conv_in",
    )(x, g.reshape(1, d), sh, sc, w)


def _conv_out_kernel(bg_ref, u_ref, up_ref, un_ref, cw_ref, x_ref, g_ref, w_ref, o_ref):
    i = pl.program_id(1)
    last = pl.num_programs(1) - 1
    u = u_ref[0]
    tm = u.shape[0]
    row = lax.broadcasted_iota(jnp.int32, u.shape, 0)
    prev_row = jnp.where(i == 0, 0.0, up_ref[0, SUBLANES - 1:SUBLANES, :])
    next_row = jnp.where(i == last, 0.0, un_ref[0, 0:1, :])
    u_prev = jnp.where(row == 0, prev_row, pltpu.roll(u, 1, axis=0))
    u_next = jnp.where(row == tm - 1, next_row, pltpu.roll(u, tm - 1, axis=0))
    y = cw_ref[0:1, :] * u_prev + cw_ref[1:2, :] * u + cw_ref[2:3, :] * u_next
    z = (bg_ref[0] * y).astype(BF16)
    o_ref[0] = x_ref[0] + g_ref[0] * jnp.dot(z, w_ref[...], preferred_element_type=F32)


def _conv_out(bg, u, cw, x, gate, w, tm):
    b, s, d = x.shape
    hb = tm // SUBLANES
    nhb = s // SUBLANES
    tile = pl.BlockSpec((1, tm, d), lambda bi, i: (bi, i, 0))
    return pl.pallas_call(
        _conv_out_kernel,
        grid=(b, s // tm),
        in_specs=[tile, tile,
                  pl.BlockSpec((1, SUBLANES, d), lambda bi, i: (bi, jnp.maximum(i * hb - 1, 0), 0)),
                  pl.BlockSpec((1, SUBLANES, d), lambda bi, i: (bi, jnp.minimum((i + 1) * hb, nhb - 1), 0)),
                  pl.BlockSpec(cw.shape, lambda bi, i: (0, 0)),
                  tile, _mod_spec(gate),
                  pl.BlockSpec((d, d), lambda bi, i: (0, 0))],
        out_specs=tile,
        out_shape=jax.ShapeDtypeStruct((b, s, d), F32),
        compiler_params=_cparams(2),
        name="conv_out",
    )(bg, u, u, u, cw, x, gate, w)


PEER_TM = 256
KEY_TILES = PEER_NKEYS // SUBLANES
assert PEER_TOPK == 2 * SUBLANES


def _tree(op, xs):
    xs = list(xs)
    while len(xs) > 1:
        xs = [op(xs[i], xs[i + 1]) for i in range(0, len(xs) - 1, 2)] + ([xs[-1]] if len(xs) % 2 else [])
    return xs[0]


def _all_sublanes(op, x):
    for shift in (4, 2, 1):
        x = op(x, pltpu.roll(x, shift, axis=0))
    return x


def _top_keys(tiles, sub):
    key_id = [sub + SUBLANES * v for v in range(KEY_TILES)]
    out = []
    for _ in range(PEER_TOPK):
        m = _all_sublanes(jnp.maximum, _tree(jnp.maximum, tiles))
        idx = _all_sublanes(jnp.minimum, _tree(jnp.minimum, [jnp.where(t == m, k, PEER_NKEYS)
                                                             for t, k in zip(tiles, key_id)]))
        tiles = [jnp.where(k == idx, NEG_INF, t) for t, k in zip(tiles, key_id)]
        out.append((m, idx))
    return out


def _rows_of(ranked, sub, which, pick):
    t = ranked[pick(0)][which]
    for r in range(1, SUBLANES):
        t = jnp.where(sub == r, ranked[pick(r)][which], t)
    return t


def _product_key_topk(first, second, sub):
    def tiles(which):
        lo = _rows_of(second, sub, which, lambda r: r)
        hi = _rows_of(second, sub, which, lambda r: SUBLANES + r)
        quad = _rows_of(second, sub, which, lambda r: r % 4)
        a45 = _rows_of(first, sub, which, lambda r: 4 + r // 4)
        a67 = _rows_of(first, sub, which, lambda r: 6 + r // 4)
        ahi = _rows_of(first, sub, which, lambda r: SUBLANES + r)
        a = [first[k][which] for k in range(4)]
        return (a[0], lo), (a[0], hi), (a[1], lo), (a[2], lo), (a[3], lo), (a45, quad), (a67, quad), (ahi, second[0][which])

    cand = [x + y for x, y in tiles(0)]
    cidx = [x * PEER_NKEYS + y for x, y in tiles(1)]
    quad_pos = jnp.where(sub < 4, sub, sub + (PEER_TOPK - 4))
    pos = [sub, sub + 8, sub + 16, sub + 32, sub + 48, quad_pos + 64, quad_pos + 96, (sub + 8) * PEER_TOPK]
    out = []
    for _ in range(PEER_TOPK):
        m = _all_sublanes(jnp.maximum, _tree(jnp.maximum, cand))
        psel = _all_sublanes(jnp.minimum, _tree(jnp.minimum, [jnp.where(c == m, p, PEER_TOPK * PEER_TOPK)
                                                              for c, p in zip(cand, pos)]))
        hits = [p == psel for p in pos]
        e = _all_sublanes(jnp.maximum, _tree(jnp.maximum, [jnp.where(hh, x, -1) for hh, x in zip(hits, cidx)]))
        cand = [jnp.where(hh, NEG_INF, c) for hh, c in zip(hits, cand)]
        out.append((m, e))
    return out


def _peer_score_kernel(x_ref, g_ref, sh_ref, sc_ref, wq_ref, keys_ref, h_ref, eidx_ref, gate_ref,
                       q_scr, s_scr, e_scr, p_scr):
    tm = x_ref.shape[1]
    nt = (((1,), (1,)), ((), ()))
    h = _norm_mod(x_ref[0], g_ref[...], sh_ref[0], sc_ref[0])
    h_ref[0] = h
    q = jnp.dot(h.astype(BF16), wq_ref[...], preferred_element_type=F32)
    for c in range(2 * PEER_HEADS):
        q_scr[c] = q[:, c * PEER_DK_HALF:(c + 1) * PEER_DK_HALF].astype(BF16)
    sub = lax.broadcasted_iota(jnp.int32, (SUBLANES, LANES), 0)

    def head_step(hd, carry):
        for p in range(2):
            s_scr[p] = lax.dot_general(keys_ref[hd, p], q_scr[2 * hd + p], nt, preferred_element_type=F32)
        row0 = pl.multiple_of(hd * PEER_TOPK, PEER_TOPK)
        for lg in range(tm // LANES):
            cols = slice(lg * LANES, (lg + 1) * LANES)
            ranked = [_top_keys([s_scr[p, v * SUBLANES:(v + 1) * SUBLANES, cols] for v in range(KEY_TILES)], sub)
                      for p in range(2)]
            top = _product_key_topk(ranked[0], ranked[1], sub)
            ex = [(jnp.exp(m - top[0][0]), e) for m, e in top]
            denom = _tree(jnp.add, [v for v, _ in ex])
            for half in range(2):
                pick = lambda r, half=half: half * SUBLANES + r
                rows = pl.ds(row0 + half * SUBLANES, SUBLANES)
                e_scr[rows, cols] = _rows_of(ex, sub, 1, pick)
                p_scr[rows, cols] = _rows_of(ex, sub, 0, pick) / denom
        return carry

    lax.fori_loop(0, PEER_HEADS, head_step, 0)
    eidx_ref[0] = e_scr[...].T
    gate_ref[0] = p_scr[...].T


def _peer_score(x, g, sh, sc, wq, keys):
    b, s, d = x.shape
    tm = PEER_TM
    n = wq.shape[1]
    tile = lambda k: pl.BlockSpec((1, tm, k), lambda bi, i: (bi, i, 0))
    return pl.pallas_call(
        _peer_score_kernel,
        grid=(b, s // tm),
        in_specs=[tile(d),
                  pl.BlockSpec((1, d), lambda bi, i: (0, 0)),
                  _mod_spec(sh), _mod_spec(sc),
                  pl.BlockSpec((d, n), lambda bi, i: (0, 0)),
                  pl.BlockSpec(keys.shape, lambda bi, i: (0, 0, 0, 0))],
        out_specs=[tile(d), tile(PEER_SEL), tile(PEER_SEL)],
        out_shape=[jax.ShapeDtypeStruct((b, s, d), F32),
                   jax.ShapeDtypeStruct((b, s, PEER_SEL), jnp.int32),
                   jax.ShapeDtypeStruct((b, s, PEER_SEL), F32)],
        scratch_shapes=[pltpu.VMEM((2 * PEER_HEADS, tm, PEER_DK_HALF), BF16),
                        pltpu.VMEM((2, PEER_NKEYS, tm), F32),
                        pltpu.VMEM((PEER_SEL, tm), jnp.int32),
                        pltpu.VMEM((PEER_SEL, tm), F32)],
        compiler_params=_cparams(2),
        name="peer_score",
    )(x, g.reshape(1, d), sh, sc, wq, keys)


HALF_ROWS = SUBLANES // 2


def _pack_table(t):
    e, d = t.shape
    half = d // 2
    assert half == HALF_ROWS * LANES
    bits = lax.bitcast_convert_type(t.astype(BF16), jnp.uint16).astype(jnp.uint32)
    words = ((bits[:, :half] << 16) | bits[:, half:]).reshape(e * HALF_ROWS, LANES)
    pad = jnp.zeros((HALF_ROWS, LANES), jnp.uint32)
    return jnp.concatenate([pad, words, pad], axis=0)


def _load_rows(eidx, on_low_sublanes):
    return HALF_ROWS * eidx + jnp.where(on_low_sublanes, HALF_ROWS, 0)


def _unpack(words):
    return (pltpu.bitcast(words & jnp.uint32(0xFFFF0000), F32), pltpu.bitcast(words << 16, F32))


_FOLD_POS = (6, 2, 4, 0, 7, 3, 5, 1)


def _fold_halves(p, sub):
    m = [jnp.where(sub < HALF_ROWS, p[2 * k], p[2 * k + 1]) for k in range(4)]
    n = [jnp.where((sub & 2) != 0, m[2 * k] + pltpu.roll(m[2 * k], 2, axis=0),
                   m[2 * k + 1] + pltpu.roll(m[2 * k + 1], 6, axis=0)) for k in range(2)]
    return jnp.where((sub & 1) != 0, n[0] + pltpu.roll(n[0], 1, axis=0), n[1] + pltpu.roll(n[1], 7, axis=0))


DOWN_TM = 256
DOWN_BLOCK = 8


def _token_tiles(h_ref, t0, sub):
    chunks = [h_ref[pl.ds(t0, SUBLANES), r * LANES:(r + 1) * LANES] for r in range(SUBLANES)]
    tiles = []
    for s in range(SUBLANES):
        tile = None
        for r in range(SUBLANES):
            piece = jnp.broadcast_to(chunks[r][s:s + 1, :], (SUBLANES, LANES))
            tile = piece if tile is None else jnp.where(sub == r, piece, tile)
        tiles.append(tile)
    return tiles


def _peer_down_kernel(*refs):
    row_refs = refs[:SUBLANES]
    h_ref, gate_ref, tbl_ref, o_ref = refs[SUBLANES:]
    tm = h_ref.shape[0]
    lane = lax.broadcasted_iota(jnp.int32, (SUBLANES, LANES), 1)
    sub = lax.broadcasted_iota(jnp.int32, (SUBLANES, LANES), 0)

    def group(gi, carry):
        t0 = pl.multiple_of(gi * SUBLANES, SUBLANES)
        hs = []
        for s, tile in enumerate(_token_tiles(h_ref, t0, sub)):
            swapped = pltpu.roll(tile, HALF_ROWS, axis=0)
            hs.append((tile, swapped) if _FOLD_POS[s] % 2 == 0 else (swapped, tile))
        gbase = gi * PEER_SEL

        def folded(j):
            prods = [None] * SUBLANES
            for s in range(SUBLANES):
                hi, lo = _unpack(tbl_ref[pl.ds(row_refs[s][gbase + j], SUBLANES), :])
                prods[_FOLD_POS[s]] = hi * hs[s][0] + lo * hs[s][1]
            return _fold_halves(prods, sub)

        def place(acc, j0, folds):
            sums = [jnp.sum(f, axis=-1, keepdims=True) for f in folds]
            for k, tot in enumerate(sums):
                acc = jnp.where(lane == j0 + k, tot, acc)
            return acc

        def block(jb, carry2):
            acc, prev = carry2
            j0 = jb * DOWN_BLOCK
            new = tuple(folded(j0 + k) for k in range(DOWN_BLOCK))
            return place(acc, j0 - DOWN_BLOCK, prev), new

        first = tuple(folded(k) for k in range(DOWN_BLOCK))
        acc, last = lax.fori_loop(1, PEER_SEL // DOWN_BLOCK, block, (jnp.zeros((SUBLANES, LANES), F32), first))
        act = place(acc, PEER_SEL - DOWN_BLOCK, last)
        gelu = act * (lax.erf(act / np.sqrt(2).astype(np.float32)) + 1.0) / 2.0
        o_ref[pl.ds(t0, SUBLANES), :] = gate_ref[pl.ds(t0, SUBLANES), :] * gelu
        return carry

    lax.fori_loop(0, tm // SUBLANES, group, 0)


def _peer_down(eidx, h, gate, tbl):
    t, d = h.shape
    tm = DOWN_TM
    low = (jnp.asarray(_FOLD_POS, jnp.int32) % 2 == 0)[None, :, None]
    rows = _load_rows(eidx.reshape(t // SUBLANES, SUBLANES, PEER_SEL), low)
    rows = rows.transpose(1, 0, 2).reshape(SUBLANES, -1)
    smem = pl.BlockSpec((tm // SUBLANES * PEER_SEL,), lambda i: (i,), memory_space=pltpu.SMEM)
    return pl.pallas_call(
        _peer_down_kernel,
        grid=(t // tm,),
        in_specs=[smem] * SUBLANES + [
            pl.BlockSpec((tm, d), lambda i: (i, 0)),
            pl.BlockSpec((tm, PEER_SEL), lambda i: (i, 0)),
            pl.BlockSpec(tbl.shape, lambda i: (0, 0), pipeline_mode=pl.Buffered(1))],
        out_specs=pl.BlockSpec((tm, PEER_SEL), lambda i: (i, 0)),
        out_shape=jax.ShapeDtypeStruct((t, PEER_SEL), F32),
        compiler_params=_cparams(1),
        name="peer_down",
    )(*[rows[s] for s in range(SUBLANES)], h, gate, tbl)


UP_TM = 128
UP_STEP = 4
UP_CHUNK = 128


def _peer_up_kernel(*refs):
    row_refs = refs[:UP_STEP]
    w_ref, eye_ref, tbl_ref, o_ref, stage_scr, wv_even, wv_odd = refs[UP_STEP:]
    wv_scr = (wv_even, wv_odd)
    tm = o_ref.shape[0]
    steps = PEER_SEL // UP_STEP
    sub = lax.broadcasted_iota(jnp.int32, (SUBLANES, LANES), 0)
    low = sub < HALF_ROWS
    ones = jnp.ones((PEER_SEL, LANES), BF16)
    zero = jnp.zeros((SUBLANES, LANES), F32)

    def spread(t, c, slot):
        wrow = w_ref[pl.ds(t, 1), :]
        p0 = wrow.astype(BF16)
        r1 = wrow - p0.astype(F32)
        p1 = r1.astype(BF16)
        p2 = (r1 - p1.astype(F32)).astype(BF16)
        rows = pl.ds(pl.multiple_of(c * UP_CHUNK, UP_CHUNK), UP_CHUNK)
        eye = eye_ref[rows, :]
        wv_scr[slot][rows, :] = (jnp.dot(eye * p0, ones, preferred_element_type=F32)
                                 + jnp.dot(eye * p1, ones, preferred_element_type=F32)
                                 + jnp.dot(eye * p2, ones, preferred_element_type=F32))

    def accumulate(t, slot):
        t_next = jnp.minimum(t + 1, tm - 1)

        def chunk(c, accs):
            accs = list(accs)
            spread(t_next, c, 1 - slot)
            kbase = t * steps + c * (UP_CHUNK // UP_STEP)
            wv_rows = wv_scr[slot].at[pl.ds(pl.multiple_of(c * UP_CHUNK, UP_CHUNK), UP_CHUNK)]
            for jj in range(UP_CHUNK // UP_STEP):
                k = kbase + jj
                for u in range(0, UP_STEP, 2):
                    j = jj * UP_STEP + u
                    words = jnp.where(low, tbl_ref[pl.ds(row_refs[u][k], SUBLANES), :],
                                      tbl_ref[pl.ds(row_refs[u + 1][k], SUBLANES), :])
                    wv = jnp.where(low, jnp.broadcast_to(wv_rows[j:j + 1, :], (SUBLANES, LANES)),
                                   jnp.broadcast_to(wv_rows[j + 1:j + 2, :], (SUBLANES, LANES)))
                    hi, lo = _unpack(words)
                    accs[u] = accs[u] + wv * hi
                    accs[u + 1] = accs[u + 1] + wv * lo
            return tuple(accs)

        accs = lax.fori_loop(0, PEER_SEL // UP_CHUNK, chunk, (zero,) * UP_STEP)
        hi = _tree(jnp.add, accs[0::2])
        lo = _tree(jnp.add, accs[1::2])
        out = jnp.where(low, hi + pltpu.roll(hi, HALF_ROWS, axis=0), lo + pltpu.roll(lo, HALF_ROWS, axis=0))
        t_in_group = t % SUBLANES
        stage_scr[pl.ds(pl.multiple_of(t_in_group * SUBLANES, SUBLANES), SUBLANES), :] = out
        return t_in_group

    def token_pair(i, carry):
        t0 = 2 * i
        accumulate(t0, 0)
        t_in_group = accumulate(t0 + 1, 1)

        @pl.when(t_in_group == SUBLANES - 1)
        def _():
            g0 = pl.multiple_of(t0 + 1 - (SUBLANES - 1), SUBLANES)
            for r in range(SUBLANES):
                o_ref[pl.ds(g0, SUBLANES), r * LANES:(r + 1) * LANES] = stage_scr[pl.ds(r, SUBLANES, stride=SUBLANES), :]

        return carry

    for c in range(PEER_SEL // UP_CHUNK):
        spread(0, c, 0)
    lax.fori_loop(0, tm // 2, token_pair, 0)


def _peer_up(eidx, w, tbl, d):
    t = w.shape[0]
    tm = UP_TM
    rows = _load_rows(eidx, (jnp.arange(PEER_SEL) % 2 == 0)[None, :])
    rows = rows.reshape(t, PEER_SEL // UP_STEP, UP_STEP).transpose(2, 0, 1).reshape(UP_STEP, -1)
    smem = pl.BlockSpec((tm * PEER_SEL // UP_STEP,), lambda i: (i,), memory_space=pltpu.SMEM)
    return pl.pallas_call(
        _peer_up_kernel,
        grid=(t // tm,),
        in_specs=[smem] * UP_STEP + [
            pl.BlockSpec((tm, PEER_SEL), lambda i: (i, 0)),
            pl.BlockSpec((PEER_SEL, PEER_SEL), lambda i: (0, 0)),
            pl.BlockSpec(tbl.shape, lambda i: (0, 0), pipeline_mode=pl.Buffered(1))],
        out_specs=pl.BlockSpec((tm, d), lambda i: (i, 0)),
        out_shape=jax.ShapeDtypeStruct((t, d), F32),
        scratch_shapes=[pltpu.VMEM((SUBLANES * SUBLANES, LANES), F32),
                        pltpu.VMEM((PEER_SEL, LANES), F32), pltpu.VMEM((PEER_SEL, LANES), F32)],
        compiler_params=_cparams(1),
        name="peer_up",
    )(*[rows[u] for u in range(UP_STEP)], w, jnp.eye(PEER_SEL, dtype=BF16), tbl)


def _residual_kernel(x_ref, p_ref, g_ref, o_ref):
    o_ref[0] = x_ref[0] + g_ref[0] * p_ref[0]


def _residual_norm_kernel(x_ref, p_ref, g_ref, fg_ref, o_ref):
    x = x_ref[0] + g_ref[0] * p_ref[0]
    o_ref[0] = (x * lax.rsqrt(jnp.mean(x * x, axis=-1, keepdims=True) + EPS)) * fg_ref[...]


def _residual(x, p, gate, final_g, tm):
    b, s, d = x.shape
    tile = pl.BlockSpec((1, tm, d), lambda bi, i: (bi, i, 0))
    in_specs = [tile, tile, _mod_spec(gate)]
    args = [x, p, gate]
    body = _residual_kernel
    if final_g is not None:
        in_specs.append(pl.BlockSpec((1, d), lambda bi, i: (0, 0)))
        args.append(final_g.reshape(1, d))
        body = _residual_norm_kernel
    return pl.pallas_call(
        body,
        grid=(b, s // tm),
        in_specs=in_specs,
        out_specs=tile,
        out_shape=jax.ShapeDtypeStruct((b, s, d), F32),
        compiler_params=_cparams(2),
        name="residual_norm" if final_g is not None else "residual",
    )(*args)


def _peer_block(x, g, sh, sc, gate2, wq, keys, down, up, final_g):
    b, s, d = x.shape
    t = b * s
    h, eidx, gsm = _peer_score(x, g, sh, sc, wq.astype(BF16), keys.astype(BF16))
    eidx = eidx.reshape(t, PEER_SEL)
    w = _peer_down(eidx, h.reshape(t, d), gsm.reshape(t, PEER_SEL), _pack_table(down))
    po = _peer_up(eidx, w, _pack_table(up), d).reshape(b, s, d)
    return _residual(x, po, gate2, final_g, 512)


def kernel(x, c, ctx, c_ctx, ada_w, ada_b, norm1_g, norm2_g, final_g, ab_w_in, ab_w_out, na_rpb, fn_w,
           cv_w_in, cv_w, cv_w_out, peer_w_q, peer_keys, peer_down, peer_up):
    b, s, d = x.shape
    depth = ada_w.shape[0]
    rows = -(-(b + 1) // SUBLANES) * SUBLANES
    cc = jnp.concatenate([c, c_ctx[None], jnp.zeros((rows - b - 1, d), F32)], axis=0)
    mod = _ada_vectors(cc, ada_w, ada_b)

    def chunks(i, lo, hi):
        m = mod[i, lo:hi].reshape(hi - lo, 1, 6, d)
        return [m[:, :, k] for k in range(6)]

    for i in range(depth):
        sh1, sc1, g1, sh2, sc2, g2 = chunks(i, 0, b)
        if i % 2 == 0:
            e = i // 2
            csh1, csc1 = chunks(i, b, b + 1)[:2]
            w_in = ab_w_in[e].astype(BF16)
            qkv, f = _norm_mod_matmul(x, norm1_g[i], sh1, sc1, w_in, (3 * NA_WIDTH, FN_WIDTH), (BF16, BF16),
                                      512, "ab_in")
            (ctx_kv,) = _norm_mod_matmul(ctx, norm1_g[i], csh1, csc1, w_in[:, NA_WIDTH:3 * NA_WIDTH],
                                         (2 * NA_WIDTH,), (BF16,), ctx.shape[1], "ab_in_ctx")
            a = _attention(qkv, ctx_kv, _attention_bias(na_rpb[e]))
            fm = _fnet(f, fn_w[e])
            x = _proj_residual(a, fm, x, g1, ab_w_out[e].astype(BF16), 512)
        else:
            o = i // 2
            bg, u = _conv_in(x, norm1_g[i], sh1, sc1, cv_w_in[o].astype(BF16), 256)
            x = _conv_out(bg, u, cv_w[o], x, g1, cv_w_out[o].astype(BF16), 512)
        x = _peer_block(x, norm2_g[i], sh2, sc2, g2, peer_w_q[i], peer_keys[i], peer_down[i], peer_up[i],
                        final_g if i == depth - 1 else None)
    return x
```

```python
import functools

import jax
import jax.numpy as jnp
import numpy as np
from jax import lax
from jax.experimental import pallas as pl
from jax.experimental.pallas import tpu as pltpu

F32 = jnp.float32
BF16 = jnp.bfloat16
EPS = 1e-6

GRID_W = 64
HEAD_DIM = 64
NA_HEADS = 8
NA_WIDTH = NA_HEADS * HEAD_DIM
KH = 8
KH_MAX = 8
KW = 16
FN_GROUPS = 8
FN_GROUP_DIM = 64
FN_WIDTH = FN_GROUPS * FN_GROUP_DIM
PEER_HEADS = 8
PEER_NKEYS = 128
PEER_DK_HALF = 128
PEER_TOPK = 16
PEER_SEL = PEER_HEADS * PEER_TOPK
LANES = 128
SUBLANES = 8
VMEM_LIMIT = 56 * 1024 * 1024

NEG_INF = float("-inf")


def _cparams(n_axes):
    return pltpu.CompilerParams(dimension_semantics=("arbitrary",) * n_axes, vmem_limit_bytes=VMEM_LIMIT)


def _norm_mod(x, g, sh, sc):
    y = x * lax.rsqrt(jnp.mean(x * x, axis=-1, keepdims=True) + EPS)
    return (y * g) * (1.0 + sc) + sh


def _ada_kernel(c_ref, w_ref, b_ref, o_ref):
    c = c_ref[...]
    s = c / (1.0 + jnp.exp(-c))
    o_ref[0] = jnp.dot(s, w_ref[0], preferred_element_type=F32, precision=lax.Precision.HIGHEST) + b_ref[0]


def _ada_vectors(cc, ada_w, ada_b):
    depth, d, n = ada_w.shape
    rows = cc.shape[0]
    tn = 1536
    return pl.pallas_call(
        _ada_kernel,
        grid=(depth, n // tn),
        in_specs=[pl.BlockSpec((rows, d), lambda i, j: (0, 0)),
                  pl.BlockSpec((1, d, tn), lambda i, j: (i, 0, j)),
                  pl.BlockSpec((1, 1, tn), lambda i, j: (i, 0, j))],
        out_specs=pl.BlockSpec((1, rows, tn), lambda i, j: (i, 0, j)),
        out_shape=jax.ShapeDtypeStruct((depth, rows, n), F32),
        compiler_params=_cparams(2),
        name="ada_vectors",
    )(cc, ada_w, ada_b.reshape(depth, 1, n))


def _nmm_kernel(x_ref, g_ref, sh_ref, sc_ref, w_ref, *o_refs):
    h = _norm_mod(x_ref[0], g_ref[...], sh_ref[0], sc_ref[0])
    acc = jnp.dot(h.astype(BF16), w_ref[...], preferred_element_type=F32)
    off = 0
    for o_ref in o_refs:
        n = o_ref.shape[-1]
        o_ref[0] = acc[:, off:off + n].astype(o_ref.dtype)
        off += n


def _mod_spec(arr):
    d = arr.shape[-1]
    if arr.shape[0] == 1:
        return pl.BlockSpec((1, 1, d), lambda b, i: (0, 0, 0))
    return pl.BlockSpec((1, 1, d), lambda b, i: (b, 0, 0))


def _norm_mod_matmul(x, g, sh, sc, w, splits, dtypes, tm, name):
    b, s, d = x.shape
    n = w.shape[1]
    assert sum(splits) == n
    return pl.pallas_call(
        _nmm_kernel,
        grid=(b, s // tm),
        in_specs=[pl.BlockSpec((1, tm, d), lambda bi, i: (bi, i, 0)),
                  pl.BlockSpec((1, d), lambda bi, i: (0, 0)),
                  _mod_spec(sh), _mod_spec(sc),
                  pl.BlockSpec((d, n), lambda bi, i: (0, 0))],
        out_specs=[pl.BlockSpec((1, tm, k), lambda bi, i: (bi, i, 0)) for k in splits],
        out_shape=[jax.ShapeDtypeStruct((b, s, k), dt) for k, dt in zip(splits, dtypes)],
        compiler_params=_cparams(2),
        name=name,
    )(x, g.reshape(1, d), sh, sc, w)


def _attn_kernel(qkv_ref, ctx_ref, bias_ref, o_ref):
    rows = qkv_ref.shape[1] // GRID_W
    nt = (((1,), (1,)), ((), ()))
    scale = HEAD_DIM ** -0.5

    def row_step(r, carry):
        start = jnp.clip(r - KH // 2, 0, rows - KH)
        off = start - r + (KH_MAX - 1)
        q0 = pl.multiple_of(r * GRID_W, GRID_W)
        k0 = pl.multiple_of(start * GRID_W, GRID_W)
        outs = []
        for h in range(NA_HEADS):
            lo = h * HEAD_DIM
            q = qkv_ref[0, pl.ds(q0, GRID_W), lo:lo + HEAD_DIM]
            kw = qkv_ref[0, pl.ds(k0, KH * GRID_W), NA_WIDTH + lo:NA_WIDTH + lo + HEAD_DIM]
            vw = qkv_ref[0, pl.ds(k0, KH * GRID_W), 2 * NA_WIDTH + lo:2 * NA_WIDTH + lo + HEAD_DIM]
            kc = ctx_ref[0, :, lo:lo + HEAD_DIM]
            vc = ctx_ref[0, :, NA_WIDTH + lo:NA_WIDTH + lo + HEAD_DIM]
            s_win = lax.dot_general(q, kw, nt, preferred_element_type=F32) * scale + bias_ref[h, off]
            s_ctx = lax.dot_general(q, kc, nt, preferred_element_type=F32) * scale
            m = jnp.maximum(jnp.max(s_win, axis=-1, keepdims=True), jnp.max(s_ctx, axis=-1, keepdims=True))
            p_win = jnp.exp(s_win - m)
            p_ctx = jnp.exp(s_ctx - m)
            l = jnp.sum(p_win, axis=-1, keepdims=True) + jnp.sum(p_ctx, axis=-1, keepdims=True)
            o = (jnp.dot(p_win.astype(BF16), vw, preferred_element_type=F32)
                 + jnp.dot(p_ctx.astype(BF16), vc, preferred_element_type=F32))
            outs.append(o / l)
        o_ref[0, pl.ds(q0, GRID_W), :] = jnp.concatenate(outs, axis=-1).astype(o_ref.dtype)
        return carry

    lax.fori_loop(0, rows, row_step, 0)


def _attention_bias(rpb):
    cols = np.arange(GRID_W)
    col_start = np.clip(cols - KW // 2, 0, GRID_W - KW)
    in_win = (cols[None, :] >= col_start[:, None]) & (cols[None, :] < col_start[:, None] + KW)
    col_off = np.clip(cols[None, :] - cols[:, None] + KW - 1, 0, 2 * KW - 2)
    onehot = jnp.asarray(col_off[None] == np.arange(2 * KW - 1)[:, None, None], F32)
    by_row = jnp.sum(rpb.astype(F32)[:, :, :, None, None] * onehot[None, None], axis=2)
    by_row = jnp.where(in_win[None, None], by_row, NEG_INF)
    b = jnp.stack([by_row[:, o:o + KH] for o in range(KH)], axis=1)
    return b.transpose(0, 1, 3, 2, 4).reshape(NA_HEADS, KH, GRID_W, KH * GRID_W)


def _attention(qkv, ctx_kv, bias):
    b, s, _ = qkv.shape
    l = ctx_kv.shape[1]
    return pl.pallas_call(
        _attn_kernel,
        grid=(b,),
        in_specs=[pl.BlockSpec((1, s, 3 * NA_WIDTH), lambda i: (i, 0, 0)),
                  pl.BlockSpec((1, l, 2 * NA_WIDTH), lambda i: (i, 0, 0)),
                  pl.BlockSpec(bias.shape, lambda i: (0, 0, 0, 0), pipeline_mode=pl.Buffered(1))],
        out_specs=pl.BlockSpec((1, s, NA_WIDTH), lambda i: (i, 0, 0)),
        out_shape=jax.ShapeDtypeStruct((b, s, NA_WIDTH), BF16),
        compiler_params=_cparams(1),
        name="nbr_attention",
    )(qkv, ctx_kv, bias)


FN_ROW_CHUNK = 256


def _fnet_kernel(f_ref, cn_ref, sn_ref, c64_ref, s64_ref, wbd_ref, o_ref, a_scr, b_scr, xa_scr, xb_scr):
    n = f_ref.shape[1]
    scale = (n * FN_GROUP_DIM) ** -0.5

    @pl.when(pl.program_id(0) == 0)
    def _():
        hi = lax.Precision.HIGHEST
        a_scr[...] = (jnp.dot(c64_ref[...], wbd_ref[...], preferred_element_type=F32, precision=hi) * scale).astype(BF16)
        b_scr[...] = (jnp.dot(s64_ref[...], wbd_ref[...], preferred_element_type=F32, precision=hi) * -scale).astype(BF16)

    x = f_ref[0]
    xa_scr[...] = jnp.dot(x, a_scr[...], preferred_element_type=F32).astype(BF16)
    xb_scr[...] = jnp.dot(x, b_scr[...], preferred_element_type=F32).astype(BF16)

    def chunk(i, carry):
        r0 = pl.multiple_of(i * FN_ROW_CHUNK, FN_ROW_CHUNK)
        y = (jnp.dot(cn_ref[pl.ds(r0, FN_ROW_CHUNK), :], xa_scr[...], preferred_element_type=F32)
             + jnp.dot(sn_ref[pl.ds(r0, FN_ROW_CHUNK), :], xb_scr[...], preferred_element_type=F32))
        o_ref[0, pl.ds(r0, FN_ROW_CHUNK), :] = y.astype(o_ref.dtype)
        return carry

    lax.fori_loop(0, n // FN_ROW_CHUNK, chunk, 0)


def _dft_tables(n):
    k = (np.arange(n)[:, None] * np.arange(n)[None, :]) % n
    ang = 2.0 * np.pi * k.astype(np.float64) / n
    return np.cos(ang), np.sin(ang)


def _fnet(f, fn_w):
    b, s, _ = f.shape
    cn, sn = _dft_tables(s)
    c64, s64 = _dft_tables(FN_GROUP_DIM)
    eye = np.eye(FN_GROUPS)
    c64bd = jnp.asarray(np.kron(eye, c64), F32)
    s64bd = jnp.asarray(np.kron(eye, s64), F32)
    wbd = (jnp.asarray(eye, F32)[:, None, :, None] * fn_w[:, :, None, :]).reshape(FN_WIDTH, FN_WIDTH)
    const = lambda shape: pl.BlockSpec(shape, lambda i: (0, 0), pipeline_mode=pl.Buffered(1))
    return pl.pallas_call(
        _fnet_kernel,
        grid=(b,),
        in_specs=[pl.BlockSpec((1, s, FN_WIDTH), lambda i: (i, 0, 0)),
                  const((s, s)), const((s, s)),
                  const((FN_WIDTH, FN_WIDTH)), const((FN_WIDTH, FN_WIDTH)), const((FN_WIDTH, FN_WIDTH))],
        out_specs=pl.BlockSpec((1, s, FN_WIDTH), lambda i: (i, 0, 0)),
        out_shape=jax.ShapeDtypeStruct((b, s, FN_WIDTH), BF16),
        scratch_shapes=[pltpu.VMEM((FN_WIDTH, FN_WIDTH), BF16), pltpu.VMEM((FN_WIDTH, FN_WIDTH), BF16),
                        pltpu.VMEM((s, FN_WIDTH), BF16), pltpu.VMEM((s, FN_WIDTH), BF16)],
        compiler_params=_cparams(1),
        name="fnet_mix",
    )(f, jnp.asarray(cn, BF16), jnp.asarray(sn, BF16), c64bd, s64bd, wbd)


def _proj_res_kernel(a_ref, f_ref, x_ref, g_ref, w_ref, o_ref):
    ka = a_ref.shape[-1]
    out = (jnp.dot(a_ref[0], w_ref[:ka, :], preferred_element_type=F32)
           + jnp.dot(f_ref[0], w_ref[ka:, :], preferred_element_type=F32))
    o_ref[0] = x_ref[0] + g_ref[0] * out


def _proj_residual(a, f, x, gate, w, tm):
    b, s, d = x.shape
    ka, kf = a.shape[-1], f.shape[-1]
    return pl.pallas_call(
        _proj_res_kernel,
        grid=(b, s // tm),
        in_specs=[pl.BlockSpec((1, tm, ka), lambda bi, i: (bi, i, 0)),
                  pl.BlockSpec((1, tm, kf), lambda bi, i: (bi, i, 0)),
                  pl.BlockSpec((1, tm, d), lambda bi, i: (bi, i, 0)),
                  _mod_spec(gate),
                  pl.BlockSpec((ka + kf, d), lambda bi, i: (0, 0))],
        out_specs=pl.BlockSpec((1, tm, d), lambda bi, i: (bi, i, 0)),
        out_shape=jax.ShapeDtypeStruct((b, s, d), F32),
        compiler_params=_cparams(2),
        name="proj_residual",
    )(a, f, x, gate, w)


def _conv_in_kernel(x_ref, g_ref, sh_ref, sc_ref, w_ref, bg_ref, u_ref):
    d = bg_ref.shape[-1]
    h = _norm_mod(x_ref[0], g_ref[...], sh_ref[0], sc_ref[0])
    acc = jnp.dot(h.astype(BF16), w_ref[...], preferred_element_type=F32)
    bg_ref[0] = acc[:, :d]
    u_ref[0] = acc[:, d:2 * d] * acc[:, 2 * d:]


def _conv_in(x, g, sh, sc, w, tm):
    b, s, d = x.shape
    n = w.shape[1]
    return pl.pallas_call(
        _conv_in_kernel,
        grid=(b, s // tm),
        in_specs=[pl.BlockSpec((1, tm, d), lambda bi, i: (bi, i, 0)),
                  pl.BlockSpec((1, d), lambda bi, i: (0, 0)),
                  _mod_spec(sh), _mod_spec(sc),
                  pl.BlockSpec((d, n), lambda bi, i: (0, 0))],
        out_specs=[pl.BlockSpec((1, tm, d), lambda bi, i: (bi, i, 0))] * 2,
        out_shape=[jax.ShapeDtypeStruct((b, s, d), F32)] * 2,
        compiler_params=_cparams(2),
        name="conv_in",
    )(x, g.reshape(1, d), sh, sc, w)


def _conv_out_kernel(bg_ref, u_ref, up_ref, un_ref, cw_ref, x_ref, g_ref, w_ref, o_ref):
    i = pl.program_id(1)
    last = pl.num_programs(1) - 1
    u = u_ref[0]
    tm = u.shape[0]
    row = lax.broadcasted_iota(jnp.int32, u.shape, 0)
    prev_row = jnp.where(i == 0, 0.0, up_ref[0, SUBLANES - 1:SUBLANES, :])
    next_row = jnp.where(i == last, 0.0, un_ref[0, 0:1, :])
    u_prev = jnp.where(row == 0, prev_row, pltpu.roll(u, 1, axis=0))
    u_next = jnp.where(row == tm - 1, next_row, pltpu.roll(u, tm - 1, axis=0))
    y = cw_ref[0:1, :] * u_prev + cw_ref[1:2, :] * u + cw_ref[2:3, :] * u_next
    z = (bg_ref[0] * y).astype(BF16)
    o_ref[0] = x_ref[0] + g_ref[0] * jnp.dot(z, w_ref[...], preferred_element_type=F32)


def _conv_out(bg, u, cw, x, gate, w, tm):
    b, s, d = x.shape
    hb = tm // SUBLANES
    nhb = s // SUBLANES
    tile = pl.BlockSpec((1, tm, d), lambda bi, i: (bi, i, 0))
    return pl.pallas_call(
        _conv_out_kernel,
        grid=(b, s // tm),
        in_specs=[tile, tile,
                  pl.BlockSpec((1, SUBLANES, d), lambda bi, i: (bi, jnp.maximum(i * hb - 1, 0), 0)),
                  pl.BlockSpec((1, SUBLANES, d), lambda bi, i: (bi, jnp.minimum((i + 1) * hb, nhb - 1), 0)),
                  pl.BlockSpec(cw.shape, lambda bi, i: (0, 0)),
                  tile, _mod_spec(gate),
                  pl.BlockSpec((d, d), lambda bi, i: (0, 0))],
        out_specs=tile,
        out_shape=jax.ShapeDtypeStruct((b, s, d), F32),
        compiler_params=_cparams(2),
        name="conv_out",
    )(bg, u, u, u, cw, x, gate, w)


PEER_TM = 256
KEY_TILES = PEER_NKEYS // SUBLANES
assert PEER_TOPK == 2 * SUBLANES


def _tree(op, xs):
    xs = list(xs)
    while len(xs) > 1:
        xs = [op(xs[i], xs[i + 1]) for i in range(0, len(xs) - 1, 2)] + ([xs[-1]] if len(xs) % 2 else [])
    return xs[0]


def _all_sublanes(op, x):
    for shift in (4, 2, 1):
        x = op(x, pltpu.roll(x, shift, axis=0))
    return x


def _top_keys(tiles, sub):
    key_id = [sub + SUBLANES * v for v in range(KEY_TILES)]
    out = []
    for _ in range(PEER_TOPK):
        m = _all_sublanes(jnp.maximum, _tree(jnp.maximum, tiles))
        idx = _all_sublanes(jnp.minimum, _tree(jnp.minimum, [jnp.where(t == m, k, PEER_NKEYS)
                                                             for t, k in zip(tiles, key_id)]))
        tiles = [jnp.where(k == idx, NEG_INF, t) for t, k in zip(tiles, key_id)]
        out.append((m, idx))
    return out


def _rows_of(ranked, sub, which, pick):
    t = ranked[pick(0)][which]
    for r in range(1, SUBLANES):
        t = jnp.where(sub == r, ranked[pick(r)][which], t)
    return t


def _product_key_topk(first, second, sub):
    def tiles(which):
        lo = _rows_of(second, sub, which, lambda r: r)
        hi = _rows_of(second, sub, which, lambda r: SUBLANES + r)
        quad = _rows_of(second, sub, which, lambda r: r % 4)
        a45 = _rows_of(first, sub, which, lambda r: 4 + r // 4)
        a67 = _rows_of(first, sub, which, lambda r: 6 + r // 4)
        ahi = _rows_of(first, sub, which, lambda r: SUBLANES + r)
        a = [first[k][which] for k in range(4)]
        return (a[0], lo), (a[0], hi), (a[1], lo), (a[2], lo), (a[3], lo), (a45, quad), (a67, quad), (ahi, second[0][which])

    cand = [x + y for x, y in tiles(0)]
    cidx = [x * PEER_NKEYS + y for x, y in tiles(1)]
    quad_pos = jnp.where(sub < 4, sub, sub + (PEER_TOPK - 4))
    pos = [sub, sub + 8, sub + 16, sub + 32, sub + 48, quad_pos + 64, quad_pos + 96, (sub + 8) * PEER_TOPK]
    out = []
    for _ in range(PEER_TOPK):
        m = _all_sublanes(jnp.maximum, _tree(jnp.maximum, cand))
        psel = _all_sublanes(jnp.minimum, _tree(jnp.minimum, [jnp.where(c == m, p, PEER_TOPK * PEER_TOPK)
                                                              for c, p in zip(cand, pos)]))
        hits = [p == psel for p in pos]
        e = _all_sublanes(jnp.maximum, _tree(jnp.maximum, [jnp.where(hh, x, -1) for hh, x in zip(hits, cidx)]))
        cand = [jnp.where(hh, NEG_INF, c) for hh, c in zip(hits, cand)]
        out.append((m, e))
    return out


def _peer_score_kernel(x_ref, g_ref, sh_ref, sc_ref, wq_ref, keys_ref, h_ref, eidx_ref, gate_ref,
                       q_scr, s_scr, e_scr, p_scr):
    tm = x_ref.shape[1]
    nt = (((1,), (1,)), ((), ()))
    h = _norm_mod(x_ref[0], g_ref[...], sh_ref[0], sc_ref[0])
    h_ref[0] = h
    q = jnp.dot(h.astype(BF16), wq_ref[...], preferred_element_type=F32)
    for c in range(2 * PEER_HEADS):
        q_scr[c] = q[:, c * PEER_DK_HALF:(c + 1) * PEER_DK_HALF].astype(BF16)
    sub = lax.broadcasted_iota(jnp.int32, (SUBLANES, LANES), 0)

    def head_step(hd, carry):
        for p in range(2):
            s_scr[p] = lax.dot_general(keys_ref[hd, p], q_scr[2 * hd + p], nt, preferred_element_type=F32)
        row0 = pl.multiple_of(hd * PEER_TOPK, PEER_TOPK)
        for lg in range(tm // LANES):
            cols = slice(lg * LANES, (lg + 1) * LANES)
            ranked = [_top_keys([s_scr[p, v * SUBLANES:(v + 1) * SUBLANES, cols] for v in range(KEY_TILES)], sub)
                      for p in range(2)]
            top = _product_key_topk(ranked[0], ranked[1], sub)
            ex = [(jnp.exp(m - top[0][0]), e) for m, e in top]
            denom = _tree(jnp.add, [v for v, _ in ex])
            for half in range(2):
                pick = lambda r, half=half: half * SUBLANES + r
                rows = pl.ds(row0 + half * SUBLANES, SUBLANES)
                e_scr[rows, cols] = _rows_of(ex, sub, 1, pick)
                p_scr[rows, cols] = _rows_of(ex, sub, 0, pick) / denom
        return carry

    lax.fori_loop(0, PEER_HEADS, head_step, 0)
    eidx_ref[0] = e_scr[...].T
    gate_ref[0] = p_scr[...].T


def _peer_score(x, g, sh, sc, wq, keys):
    b, s, d = x.shape
    tm = PEER_TM
    n = wq.shape[1]
    tile = lambda k: pl.BlockSpec((1, tm, k), lambda bi, i: (bi, i, 0))
    return pl.pallas_call(
        _peer_score_kernel,
        grid=(b, s // tm),
        in_specs=[tile(d),
                  pl.BlockSpec((1, d), lambda bi, i: (0, 0)),
                  _mod_spec(sh), _mod_spec(sc),
                  pl.BlockSpec((d, n), lambda bi, i: (0, 0)),
                  pl.BlockSpec(keys.shape, lambda bi, i: (0, 0, 0, 0))],
        out_specs=[tile(d), tile(PEER_SEL), tile(PEER_SEL)],
        out_shape=[jax.ShapeDtypeStruct((b, s, d), F32),
                   jax.ShapeDtypeStruct((b, s, PEER_SEL), jnp.int32),
                   jax.ShapeDtypeStruct((b, s, PEER_SEL), F32)],
        scratch_shapes=[pltpu.VMEM((2 * PEER_HEADS, tm, PEER_DK_HALF), BF16),
                        pltpu.VMEM((2, PEER_NKEYS, tm), F32),
                        pltpu.VMEM((PEER_SEL, tm), jnp.int32),
                        pltpu.VMEM((PEER_SEL, tm), F32)],
        compiler_params=_cparams(2),
        name="peer_score",
    )(x, g.reshape(1, d), sh, sc, wq, keys)


HALF_ROWS = SUBLANES // 2


PACK_EB = 256


def _pack_kernel(x_ref, o_ref):
    half = x_ref.shape[1] // 2
    is_tail = pl.program_id(0) == pl.num_programs(0) - 1
    row = lax.broadcasted_iota(jnp.int32, (PACK_EB, LANES), 0)
    for c in range(HALF_ROWS):
        hi = x_ref[:, c * LANES:(c + 1) * LANES].astype(BF16).astype(F32)
        lo = x_ref[:, half + c * LANES:half + (c + 1) * LANES].astype(BF16).astype(F32)
        words = pltpu.bitcast(hi, jnp.uint32) | (pltpu.bitcast(lo, jnp.uint32) >> 16)
        tail = jnp.where(row == 1, pltpu.roll(words, 1, axis=0), jnp.uint32(0))
        o_ref[pl.ds(c, PACK_EB, stride=HALF_ROWS), :] = jnp.where(is_tail, tail, words)


def _pack_table(t):
    e, d = t.shape
    assert d // 2 == HALF_ROWS * LANES and e % PACK_EB == 0
    nb = e // PACK_EB
    return pl.pallas_call(
        _pack_kernel,
        grid=(nb + 1,),
        in_specs=[pl.BlockSpec((PACK_EB, d), lambda i: (i % nb, 0))],
        out_specs=pl.BlockSpec((PACK_EB * HALF_ROWS, LANES), lambda i: (i, 0)),
        out_shape=jax.ShapeDtypeStruct(((e + PACK_EB) * HALF_ROWS, LANES), jnp.uint32),
        compiler_params=_cparams(1),
        name="pack_table",
    )(t)


def _load_rows(eidx, on_low_sublanes, n_experts):
    high_start = jnp.where(eidx == 0, HALF_ROWS * n_experts, HALF_ROWS * (eidx - 1))
    return jnp.where(on_low_sublanes, HALF_ROWS * eidx, high_start)


def _unpack(words):
    return (pltpu.bitcast(words & jnp.uint32(0xFFFF0000), F32), pltpu.bitcast(words << 16, F32))


_FOLD_POS = (6, 2, 4, 0, 7, 3, 5, 1)


def _fold_halves(p, sub):
    m = [jnp.where(sub < HALF_ROWS, p[2 * k], p[2 * k + 1]) for k in range(4)]
    n = [jnp.where((sub & 2) != 0, m[2 * k] + pltpu.roll(m[2 * k], 2, axis=0),
                   m[2 * k + 1] + pltpu.roll(m[2 * k + 1], 6, axis=0)) for k in range(2)]
    return jnp.where((sub & 1) != 0, n[0] + pltpu.roll(n[0], 1, axis=0), n[1] + pltpu.roll(n[1], 7, axis=0))


DOWN_TM = 256
DOWN_BLOCK = 8


def _token_tiles(h_ref, t0, sub):
    chunks = [h_ref[pl.ds(t0, SUBLANES), r * LANES:(r + 1) * LANES] for r in range(SUBLANES)]
    tiles = []
    for s in range(SUBLANES):
        tile = None
        for r in range(SUBLANES):
            piece = jnp.broadcast_to(chunks[r][s:s + 1, :], (SUBLANES, LANES))
            tile = piece if tile is None else jnp.where(sub == r, piece, tile)
        tiles.append(tile)
    return tiles


def _peer_down_kernel(*refs):
    row_refs = refs[:SUBLANES]
    h_ref, gate_ref, tbl_ref, o_ref = refs[SUBLANES:]
    tm = h_ref.shape[0]
    lane = lax.broadcasted_iota(jnp.int32, (SUBLANES, LANES), 1)
    sub = lax.broadcasted_iota(jnp.int32, (SUBLANES, LANES), 0)

    def group(gi, carry):
        t0 = pl.multiple_of(gi * SUBLANES, SUBLANES)
        hs = []
        for s, tile in enumerate(_token_tiles(h_ref, t0, sub)):
            swapped = pltpu.roll(tile, HALF_ROWS, axis=0)
            hs.append((tile, swapped) if _FOLD_POS[s] % 2 == 0 else (swapped, tile))
        gbase = gi * PEER_SEL

        def folded(j):
            prods = [None] * SUBLANES
            for s in range(SUBLANES):
                hi, lo = _unpack(tbl_ref[pl.ds(row_refs[s][gbase + j], SUBLANES), :])
                prods[_FOLD_POS[s]] = hi * hs[s][0] + lo * hs[s][1]
            return _fold_halves(prods, sub)

        def place(acc, j0, folds):
            sums = [jnp.sum(f, axis=-1, keepdims=True) for f in folds]
            for k, tot in enumerate(sums):
                acc = jnp.where(lane == j0 + k, tot, acc)
            return acc

        def block(jb, carry2):
            acc, prev = carry2
            j0 = jb * DOWN_BLOCK
            new = tuple(folded(j0 + k) for k in range(DOWN_BLOCK))
            return place(acc, j0 - DOWN_BLOCK, prev), new

        first = tuple(folded(k) for k in range(DOWN_BLOCK))
        acc, last = lax.fori_loop(1, PEER_SEL // DOWN_BLOCK, block, (jnp.zeros((SUBLANES, LANES), F32), first))
        act = place(acc, PEER_SEL - DOWN_BLOCK, last)
        gelu = act * (lax.erf(act / np.sqrt(2).astype(np.float32)) + 1.0) / 2.0
        o_ref[pl.ds(t0, SUBLANES), :] = gate_ref[pl.ds(t0, SUBLANES), :] * gelu
        return carry

    lax.fori_loop(0, tm // SUBLANES, group, 0)


def _peer_down(eidx, h, gate, tbl):
    t, d = h.shape
    tm = DOWN_TM
    low = (jnp.asarray(_FOLD_POS, jnp.int32) % 2 == 0)[None, :, None]
    rows = _load_rows(eidx.reshape(t // SUBLANES, SUBLANES, PEER_SEL), low, tbl.shape[0] // HALF_ROWS - PACK_EB)
    rows = rows.transpose(1, 0, 2).reshape(SUBLANES, -1)
    smem = pl.BlockSpec((tm // SUBLANES * PEER_SEL,), lambda i: (i,), memory_space=pltpu.SMEM)
    return pl.pallas_call(
        _peer_down_kernel,
        grid=(t // tm,),
        in_specs=[smem] * SUBLANES + [
            pl.BlockSpec((tm, d), lambda i: (i, 0)),
            pl.BlockSpec((tm, PEER_SEL), lambda i: (i, 0)),
            pl.BlockSpec(tbl.shape, lambda i: (0, 0), pipeline_mode=pl.Buffered(1))],
        out_specs=pl.BlockSpec((tm, PEER_SEL), lambda i: (i, 0)),
        out_shape=jax.ShapeDtypeStruct((t, PEER_SEL), F32),
        compiler_params=_cparams(1),
        name="peer_down",
    )(*[rows[s] for s in range(SUBLANES)], h, gate, tbl)


UP_TM = 128
UP_STEP = 4
UP_CHUNK = 128


def _peer_up_kernel(*refs, final_norm):
    row_refs = refs[:UP_STEP]
    w_ref, eye_ref, x_ref, gate_ref, fg_ref, tbl_ref, o_ref, stage_scr, wv_scr = refs[UP_STEP:]
    tm, d = o_ref.shape
    steps = PEER_SEL // UP_STEP
    sub = lax.broadcasted_iota(jnp.int32, (SUBLANES, LANES), 0)
    low = sub < HALF_ROWS
    ones = jnp.ones((PEER_SEL, LANES), BF16)
    zero = jnp.zeros((SUBLANES, LANES), F32)

    def spread(t, slot):
        wrow = w_ref[pl.ds(t, 1), :]
        p0 = wrow.astype(BF16)
        r1 = wrow - p0.astype(F32)
        p1 = r1.astype(BF16)
        p2 = (r1 - p1.astype(F32)).astype(BF16)
        eye = eye_ref[...]
        wv_scr[slot] = (jnp.dot(eye * p0, ones, preferred_element_type=F32)
                        + jnp.dot(eye * p1, ones, preferred_element_type=F32)
                        + jnp.dot(eye * p2, ones, preferred_element_type=F32))

    def accumulate(t, slot):
        wv_rows = wv_scr.at[slot]
        tbase = t * steps
        accs = [zero] * UP_STEP
        for jj in range(steps):
            k = tbase + jj
            for u in range(0, UP_STEP, 2):
                j = jj * UP_STEP + u
                words = jnp.where(low, tbl_ref[pl.ds(row_refs[u][k], SUBLANES), :],
                                  tbl_ref[pl.ds(row_refs[u + 1][k], SUBLANES), :])
                wv = jnp.where(low, jnp.broadcast_to(wv_rows[j:j + 1, :], (SUBLANES, LANES)),
                               jnp.broadcast_to(wv_rows[j + 1:j + 2, :], (SUBLANES, LANES)))
                hi, lo = _unpack(words)
                accs[u] = accs[u] + wv * hi
                accs[u + 1] = accs[u + 1] + wv * lo
        hi = _tree(jnp.add, accs[0::2])
        lo = _tree(jnp.add, accs[1::2])
        out = jnp.where(low, hi + pltpu.roll(hi, HALF_ROWS, axis=0), lo + pltpu.roll(lo, HALF_ROWS, axis=0))
        t_in_group = t % SUBLANES
        stage_scr[pl.ds(pl.multiple_of(t_in_group * SUBLANES, SUBLANES), SUBLANES), :] = out
        return t_in_group

    def token_pair(i, carry):
        t0 = 2 * i
        cur = 2 * (i % 2)
        accumulate(t0, cur)
        t_in_group = accumulate(t0 + 1, cur + 1)
        spread(jnp.minimum(t0 + 2, tm - 1), 2 - cur)
        spread(jnp.minimum(t0 + 3, tm - 1), 3 - cur)

        @pl.when(t_in_group == SUBLANES - 1)
        def _():
            g0 = pl.multiple_of(t0 + 1 - (SUBLANES - 1), SUBLANES)
            ys = []
            for r in range(SUBLANES):
                cols = slice(r * LANES, (r + 1) * LANES)
                ys.append(x_ref[pl.ds(g0, SUBLANES), cols]
                          + gate_ref[0, :, cols] * stage_scr[pl.ds(r, SUBLANES, stride=SUBLANES), :])
            if final_norm:
                ss = _tree(jnp.add, [jnp.sum(y * y, axis=-1, keepdims=True) for y in ys])
                scale = lax.rsqrt(ss / d + EPS)
                ys = [(y * scale) * fg_ref[:, r * LANES:(r + 1) * LANES] for r, y in enumerate(ys)]
            for r, y in enumerate(ys):
                o_ref[pl.ds(g0, SUBLANES), r * LANES:(r + 1) * LANES] = y

        return carry

    spread(0, 0)
    spread(1, 1)
    lax.fori_loop(0, tm // 2, token_pair, 0)


def _peer_up(eidx, w, tbl, x, gate, final_g):
    b, s, d = x.shape
    t = b * s
    tm = UP_TM
    assert s % tm == 0
    fg = jnp.ones((1, d), F32) if final_g is None else final_g.reshape(1, d)
    rows = _load_rows(eidx, (jnp.arange(PEER_SEL) % 2 == 0)[None, :], tbl.shape[0] // HALF_ROWS - PACK_EB)
    rows = rows.reshape(t, PEER_SEL // UP_STEP, UP_STEP).transpose(2, 0, 1).reshape(UP_STEP, -1)
    smem = pl.BlockSpec((tm * PEER_SEL // UP_STEP,), lambda i: (i,), memory_space=pltpu.SMEM)
    tiles_per_batch = s // tm
    out = pl.pallas_call(
        functools.partial(_peer_up_kernel, final_norm=final_g is not None),
        grid=(t // tm,),
        in_specs=[smem] * UP_STEP + [
            pl.BlockSpec((tm, PEER_SEL), lambda i: (i, 0)),
            pl.BlockSpec((PEER_SEL, PEER_SEL), lambda i: (0, 0)),
            pl.BlockSpec((tm, d), lambda i: (i, 0)),
            pl.BlockSpec((1, 1, d), lambda i: (i // tiles_per_batch, 0, 0)),
            pl.BlockSpec((1, d), lambda i: (0, 0)),
            pl.BlockSpec(tbl.shape, lambda i: (0, 0), pipeline_mode=pl.Buffered(1))],
        out_specs=pl.BlockSpec((tm, d), lambda i: (i, 0)),
        out_shape=jax.ShapeDtypeStruct((t, d), F32),
        scratch_shapes=[pltpu.VMEM((SUBLANES * SUBLANES, LANES), F32),
                        pltpu.VMEM((4, PEER_SEL, LANES), F32)],
        compiler_params=_cparams(1),
        name="peer_up",
    )(*[rows[u] for u in range(UP_STEP)], w, jnp.eye(PEER_SEL, dtype=BF16), x.reshape(t, d), gate, fg, tbl)
    return out.reshape(b, s, d)


def _peer_block(x, g, sh, sc, gate2, wq, keys, down, up, final_g):
    b, s, d = x.shape
    t = b * s
    h, eidx, gsm = _peer_score(x, g, sh, sc, wq.astype(BF16), keys.astype(BF16))
    eidx = eidx.reshape(t, PEER_SEL)
    w = _peer_down(eidx, h.reshape(t, d), gsm.reshape(t, PEER_SEL), _pack_table(down))
    return _peer_up(eidx, w, _pack_table(up), x, gate2, final_g)


def kernel(x, c, ctx, c_ctx, ada_w, ada_b, norm1_g, norm2_g, final_g, ab_w_in, ab_w_out, na_rpb, fn_w,
           cv_w_in, cv_w, cv_w_out, peer_w_q, peer_keys, peer_down, peer_up):
    b, s, d = x.shape
    depth = ada_w.shape[0]
    rows = -(-(b + 1) // SUBLANES) * SUBLANES
    cc = jnp.concatenate([c, c_ctx[None], jnp.zeros((rows - b - 1, d), F32)], axis=0)
    mod = _ada_vectors(cc, ada_w, ada_b)

    def chunks(i, lo, hi):
        m = mod[i, lo:hi].reshape(hi - lo, 1, 6, d)
        return [m[:, :, k] for k in range(6)]

    for i in range(depth):
        sh1, sc1, g1, sh2, sc2, g2 = chunks(i, 0, b)
        if i % 2 == 0:
            e = i // 2
            csh1, csc1 = chunks(i, b, b + 1)[:2]
            w_in = ab_w_in[e].astype(BF16)
            qkv, f = _norm_mod_matmul(x, norm1_g[i], sh1, sc1, w_in, (3 * NA_WIDTH, FN_WIDTH), (BF16, BF16),
                                      512, "ab_in")
            (ctx_kv,) = _norm_mod_matmul(ctx, norm1_g[i], csh1, csc1, w_in[:, NA_WIDTH:3 * NA_WIDTH],
                                         (2 * NA_WIDTH,), (BF16,), ctx.shape[1], "ab_in_ctx")
            a = _attention(qkv, ctx_kv, _attention_bias(na_rpb[e]))
            fm = _fnet(f, fn_w[e])
            x = _proj_residual(a, fm, x, g1, ab_w_out[e].astype(BF16), 512)
        else:
            o = i // 2
            bg, u = _conv_in(x, norm1_g[i], sh1, sc1, cv_w_in[o].astype(BF16), 256)
            x = _conv_out(bg, u, cv_w[o], x, g1, cv_w_out[o].astype(BF16), 512)
        x = _peer_block(x, norm2_g[i], sh2, sc2, g2, peer_w_q[i], peer_keys[i], peer_down[i], peer_up[i],
                        final_g if i == depth - 1 else None)
    return x
```

```python
import functools

import jax
import jax.numpy as jnp
import numpy as np
from jax import lax
from jax.experimental import pallas as pl
from jax.experimental.pallas import tpu as pltpu

F32 = jnp.float32
BF16 = jnp.bfloat16
EPS = 1e-6

GRID_W = 64
HEAD_DIM = 64
NA_HEADS = 8
NA_WIDTH = NA_HEADS * HEAD_DIM
KH = 8
KH_MAX = 8
KW = 16
FN_GROUPS = 8
FN_GROUP_DIM = 64
FN_WIDTH = FN_GROUPS * FN_GROUP_DIM
PEER_HEADS = 8
PEER_NKEYS = 128
PEER_DK_HALF = 128
PEER_TOPK = 16
PEER_SEL = PEER_HEADS * PEER_TOPK
LANES = 128
SUBLANES = 8
VMEM_LIMIT = 56 * 1024 * 1024

NEG_INF = float("-inf")


def _cparams(n_axes):
    return pltpu.CompilerParams(dimension_semantics=("arbitrary",) * n_axes, vmem_limit_bytes=VMEM_LIMIT)


def _norm_mod(x, g, sh, sc):
    y = x * lax.rsqrt(jnp.mean(x * x, axis=-1, keepdims=True) + EPS)
    return (y * g) * (1.0 + sc) + sh


def _ada_kernel(c_ref, w_ref, b_ref, o_ref):
    c = c_ref[...]
    s = c / (1.0 + jnp.exp(-c))
    o_ref[0] = jnp.dot(s, w_ref[0], preferred_element_type=F32, precision=lax.Precision.HIGHEST) + b_ref[0]


def _ada_vectors(cc, ada_w, ada_b):
    depth, d, n = ada_w.shape
    rows = cc.shape[0]
    tn = 1536
    return pl.pallas_call(
        _ada_kernel,
        grid=(depth, n // tn),
        in_specs=[pl.BlockSpec((rows, d), lambda i, j: (0, 0)),
                  pl.BlockSpec((1, d, tn), lambda i, j: (i, 0, j)),
                  pl.BlockSpec((1, 1, tn), lambda i, j: (i, 0, j))],
        out_specs=pl.BlockSpec((1, rows, tn), lambda i, j: (i, 0, j)),
        out_shape=jax.ShapeDtypeStruct((depth, rows, n), F32),
        compiler_params=_cparams(2),
        name="ada_vectors",
    )(cc, ada_w, ada_b.reshape(depth, 1, n))


def _nmm_kernel(x_ref, g_ref, sh_ref, sc_ref, w_ref, *o_refs):
    h = _norm_mod(x_ref[0], g_ref[...], sh_ref[0], sc_ref[0])
    acc = jnp.dot(h.astype(BF16), w_ref[...], preferred_element_type=F32)
    off = 0
    for o_ref in o_refs:
        n = o_ref.shape[-1]
        o_ref[0] = acc[:, off:off + n].astype(o_ref.dtype)
        off += n


def _mod_spec(arr):
    d = arr.shape[-1]
    if arr.shape[0] == 1:
        return pl.BlockSpec((1, 1, d), lambda b, i: (0, 0, 0))
    return pl.BlockSpec((1, 1, d), lambda b, i: (b, 0, 0))


def _norm_mod_matmul(x, g, sh, sc, w, splits, dtypes, tm, name):
    b, s, d = x.shape
    n = w.shape[1]
    assert sum(splits) == n
    return pl.pallas_call(
        _nmm_kernel,
        grid=(b, s // tm),
        in_specs=[pl.BlockSpec((1, tm, d), lambda bi, i: (bi, i, 0)),
                  pl.BlockSpec((1, d), lambda bi, i: (0, 0)),
                  _mod_spec(sh), _mod_spec(sc),
                  pl.BlockSpec((d, n), lambda bi, i: (0, 0))],
        out_specs=[pl.BlockSpec((1, tm, k), lambda bi, i: (bi, i, 0)) for k in splits],
        out_shape=[jax.ShapeDtypeStruct((b, s, k), dt) for k, dt in zip(splits, dtypes)],
        compiler_params=_cparams(2),
        name=name,
    )(x, g.reshape(1, d), sh, sc, w)


def _attn_kernel(qkv_ref, ctx_ref, bias_ref, o_ref):
    rows = qkv_ref.shape[1] // GRID_W
    nt = (((1,), (1,)), ((), ()))
    scale = HEAD_DIM ** -0.5

    def row_step(r, carry):
        start = jnp.clip(r - KH // 2, 0, rows - KH)
        off = start - r + (KH_MAX - 1)
        q0 = pl.multiple_of(r * GRID_W, GRID_W)
        k0 = pl.multiple_of(start * GRID_W, GRID_W)
        outs = []
        for h in range(NA_HEADS):
            lo = h * HEAD_DIM
            q = qkv_ref[0, pl.ds(q0, GRID_W), lo:lo + HEAD_DIM]
            kw = qkv_ref[0, pl.ds(k0, KH * GRID_W), NA_WIDTH + lo:NA_WIDTH + lo + HEAD_DIM]
            vw = qkv_ref[0, pl.ds(k0, KH * GRID_W), 2 * NA_WIDTH + lo:2 * NA_WIDTH + lo + HEAD_DIM]
            kc = ctx_ref[0, :, lo:lo + HEAD_DIM]
            vc = ctx_ref[0, :, NA_WIDTH + lo:NA_WIDTH + lo + HEAD_DIM]
            s_win = lax.dot_general(q, kw, nt, preferred_element_type=F32) * scale + bias_ref[h, off]
            s_ctx = lax.dot_general(q, kc, nt, preferred_element_type=F32) * scale
            m = jnp.maximum(jnp.max(s_win, axis=-1, keepdims=True), jnp.max(s_ctx, axis=-1, keepdims=True))
            p_win = jnp.exp(s_win - m)
            p_ctx = jnp.exp(s_ctx - m)
            l = jnp.sum(p_win, axis=-1, keepdims=True) + jnp.sum(p_ctx, axis=-1, keepdims=True)
            o = (jnp.dot(p_win.astype(BF16), vw, preferred_element_type=F32)
                 + jnp.dot(p_ctx.astype(BF16), vc, preferred_element_type=F32))
            outs.append(o / l)
        o_ref[0, pl.ds(q0, GRID_W), :] = jnp.concatenate(outs, axis=-1).astype(o_ref.dtype)
        return carry

    lax.fori_loop(0, rows, row_step, 0)


def _attention_bias(rpb):
    cols = np.arange(GRID_W)
    col_start = np.clip(cols - KW // 2, 0, GRID_W - KW)
    in_win = (cols[None, :] >= col_start[:, None]) & (cols[None, :] < col_start[:, None] + KW)
    col_off = np.clip(cols[None, :] - cols[:, None] + KW - 1, 0, 2 * KW - 2)
    onehot = jnp.asarray(col_off[None] == np.arange(2 * KW - 1)[:, None, None], F32)
    by_row = jnp.sum(rpb.astype(F32)[:, :, :, None, None] * onehot[None, None], axis=2)
    by_row = jnp.where(in_win[None, None], by_row, NEG_INF)
    b = jnp.stack([by_row[:, o:o + KH] for o in range(KH)], axis=1)
    return b.transpose(0, 1, 3, 2, 4).reshape(NA_HEADS, KH, GRID_W, KH * GRID_W)


def _attention(qkv, ctx_kv, bias):
    b, s, _ = qkv.shape
    l = ctx_kv.shape[1]
    return pl.pallas_call(
        _attn_kernel,
        grid=(b,),
        in_specs=[pl.BlockSpec((1, s, 3 * NA_WIDTH), lambda i: (i, 0, 0)),
                  pl.BlockSpec((1, l, 2 * NA_WIDTH), lambda i: (i, 0, 0)),
                  pl.BlockSpec(bias.shape, lambda i: (0, 0, 0, 0), pipeline_mode=pl.Buffered(1))],
        out_specs=pl.BlockSpec((1, s, NA_WIDTH), lambda i: (i, 0, 0)),
        out_shape=jax.ShapeDtypeStruct((b, s, NA_WIDTH), BF16),
        compiler_params=_cparams(1),
        name="nbr_attention",
    )(qkv, ctx_kv, bias)


FN_ROW_CHUNK = 256


def _fnet_kernel(f_ref, cn_ref, sn_ref, c64_ref, s64_ref, wbd_ref, o_ref, a_scr, b_scr, xa_scr, xb_scr):
    n = f_ref.shape[1]
    scale = (n * FN_GROUP_DIM) ** -0.5

    @pl.when(pl.program_id(0) == 0)
    def _():
        hi = lax.Precision.HIGHEST
        a_scr[...] = (jnp.dot(c64_ref[...], wbd_ref[...], preferred_element_type=F32, precision=hi) * scale).astype(BF16)
        b_scr[...] = (jnp.dot(s64_ref[...], wbd_ref[...], preferred_element_type=F32, precision=hi) * -scale).astype(BF16)

    x = f_ref[0]
    xa_scr[...] = jnp.dot(x, a_scr[...], preferred_element_type=F32).astype(BF16)
    xb_scr[...] = jnp.dot(x, b_scr[...], preferred_element_type=F32).astype(BF16)

    def chunk(i, carry):
        r0 = pl.multiple_of(i * FN_ROW_CHUNK, FN_ROW_CHUNK)
        y = (jnp.dot(cn_ref[pl.ds(r0, FN_ROW_CHUNK), :], xa_scr[...], preferred_element_type=F32)
             + jnp.dot(sn_ref[pl.ds(r0, FN_ROW_CHUNK), :], xb_scr[...], preferred_element_type=F32))
        o_ref[0, pl.ds(r0, FN_ROW_CHUNK), :] = y.astype(o_ref.dtype)
        return carry

    lax.fori_loop(0, n // FN_ROW_CHUNK, chunk, 0)


def _dft_tables(n):
    k = (np.arange(n)[:, None] * np.arange(n)[None, :]) % n
    ang = 2.0 * np.pi * k.astype(np.float64) / n
    return np.cos(ang), np.sin(ang)


def _fnet(f, fn_w):
    b, s, _ = f.shape
    cn, sn = _dft_tables(s)
    c64, s64 = _dft_tables(FN_GROUP_DIM)
    eye = np.eye(FN_GROUPS)
    c64bd = jnp.asarray(np.kron(eye, c64), F32)
    s64bd = jnp.asarray(np.kron(eye, s64), F32)
    wbd = (jnp.asarray(eye, F32)[:, None, :, None] * fn_w[:, :, None, :]).reshape(FN_WIDTH, FN_WIDTH)
    const = lambda shape: pl.BlockSpec(shape, lambda i: (0, 0), pipeline_mode=pl.Buffered(1))
    return pl.pallas_call(
        _fnet_kernel,
        grid=(b,),
        in_specs=[pl.BlockSpec((1, s, FN_WIDTH), lambda i: (i, 0, 0)),
                  const((s, s)), const((s, s)),
                  const((FN_WIDTH, FN_WIDTH)), const((FN_WIDTH, FN_WIDTH)), const((FN_WIDTH, FN_WIDTH))],
        out_specs=pl.BlockSpec((1, s, FN_WIDTH), lambda i: (i, 0, 0)),
        out_shape=jax.ShapeDtypeStruct((b, s, FN_WIDTH), BF16),
        scratch_shapes=[pltpu.VMEM((FN_WIDTH, FN_WIDTH), BF16), pltpu.VMEM((FN_WIDTH, FN_WIDTH), BF16),
                        pltpu.VMEM((s, FN_WIDTH), BF16), pltpu.VMEM((s, FN_WIDTH), BF16)],
        compiler_params=_cparams(1),
        name="fnet_mix",
    )(f, jnp.asarray(cn, BF16), jnp.asarray(sn, BF16), c64bd, s64bd, wbd)


def _proj_res_kernel(a_ref, f_ref, x_ref, g_ref, w_ref, o_ref):
    ka = a_ref.shape[-1]
    out = (jnp.dot(a_ref[0], w_ref[:ka, :], preferred_element_type=F32)
           + jnp.dot(f_ref[0], w_ref[ka:, :], preferred_element_type=F32))
    o_ref[0] = x_ref[0] + g_ref[0] * out


def _proj_residual(a, f, x, gate, w, tm):
    b, s, d = x.shape
    ka, kf = a.shape[-1], f.shape[-1]
    return pl.pallas_call(
        _proj_res_kernel,
        grid=(b, s // tm),
        in_specs=[pl.BlockSpec((1, tm, ka), lambda bi, i: (bi, i, 0)),
                  pl.BlockSpec((1, tm, kf), lambda bi, i: (bi, i, 0)),
                  pl.BlockSpec((1, tm, d), lambda bi, i: (bi, i, 0)),
                  _mod_spec(gate),
                  pl.BlockSpec((ka + kf, d), lambda bi, i: (0, 0))],
        out_specs=pl.BlockSpec((1, tm, d), lambda bi, i: (bi, i, 0)),
        out_shape=jax.ShapeDtypeStruct((b, s, d), F32),
        compiler_params=_cparams(2),
        name="proj_residual",
    )(a, f, x, gate, w)


def _conv_in_kernel(x_ref, g_ref, sh_ref, sc_ref, w_ref, bg_ref, u_ref):
    d = bg_ref.shape[-1]
    h = _norm_mod(x_ref[0], g_ref[...], sh_ref[0], sc_ref[0])
    acc = jnp.dot(h.astype(BF16), w_ref[...], preferred_element_type=F32)
    bg_ref[0] = acc[:, :d]
    u_ref[0] = acc[:, d:2 * d] * acc[:, 2 * d:]


def _conv_in(x, g, sh, sc, w, tm):
    b, s, d = x.shape
    n = w.shape[1]
    return pl.pallas_call(
        _conv_in_kernel,
        grid=(b, s // tm),
        in_specs=[pl.BlockSpec((1, tm, d), lambda bi, i: (bi, i, 0)),
                  pl.BlockSpec((1, d), lambda bi, i: (0, 0)),
                  _mod_spec(sh), _mod_spec(sc),
                  pl.BlockSpec((d, n), lambda bi, i: (0, 0))],
        out_specs=[pl.BlockSpec((1, tm, d), lambda bi, i: (bi, i, 0))] * 2,
        out_shape=[jax.ShapeDtypeStruct((b, s, d), F32)] * 2,
        compiler_params=_cparams(2),
        name="conv_in",
    )(x, g.reshape(1, d), sh, sc, w)


def _conv_out_kernel(bg_ref, u_ref, up_ref, un_ref, cw_ref, x_ref, g_ref, w_ref, o_ref):
    i = pl.program_id(1)
    last = pl.num_programs(1) - 1
    u = u_ref[0]
    tm = u.shape[0]
    row = lax.broadcasted_iota(jnp.int32, u.shape, 0)
    prev_row = jnp.where(i == 0, 0.0, up_ref[0, SUBLANES - 1:SUBLANES, :])
    next_row = jnp.where(i == last, 0.0, un_ref[0, 0:1, :])
    u_prev = jnp.where(row == 0, prev_row, pltpu.roll(u, 1, axis=0))
    u_next = jnp.where(row == tm - 1, next_row, pltpu.roll(u, tm - 1, axis=0))
    y = cw_ref[0:1, :] * u_prev + cw_ref[1:2, :] * u + cw_ref[2:3, :] * u_next
    z = (bg_ref[0] * y).astype(BF16)
    o_ref[0] = x_ref[0] + g_ref[0] * jnp.dot(z, w_ref[...], preferred_element_type=F32)


def _conv_out(bg, u, cw, x, gate, w, tm):
    b, s, d = x.shape
    hb = tm // SUBLANES
    nhb = s // SUBLANES
    tile = pl.BlockSpec((1, tm, d), lambda bi, i: (bi, i, 0))
    return pl.pallas_call(
        _conv_out_kernel,
        grid=(b, s // tm),
        in_specs=[tile, tile,
                  pl.BlockSpec((1, SUBLANES, d), lambda bi, i: (bi, jnp.maximum(i * hb - 1, 0), 0)),
                  pl.BlockSpec((1, SUBLANES, d), lambda bi, i: (bi, jnp.minimum((i + 1) * hb, nhb - 1), 0)),
                  pl.BlockSpec(cw.shape, lambda bi, i: (0, 0)),
                  tile, _mod_spec(gate),
                  pl.BlockSpec((d, d), lambda bi, i: (0, 0))],
        out_specs=tile,
        out_shape=jax.ShapeDtypeStruct((b, s, d), F32),
        compiler_params=_cparams(2),
        name="conv_out",
    )(bg, u, u, u, cw, x, gate, w)


PEER_TM = 256
KEY_TILES = PEER_NKEYS // SUBLANES
assert PEER_TOPK == 2 * SUBLANES


def _tree(op, xs):
    xs = list(xs)
    while len(xs) > 1:
        xs = [op(xs[i], xs[i + 1]) for i in range(0, len(xs) - 1, 2)] + ([xs[-1]] if len(xs) % 2 else [])
    return xs[0]


def _all_sublanes(op, x):
    for shift in (4, 2, 1):
        x = op(x, pltpu.roll(x, shift, axis=0))
    return x


def _top_keys(problems, sub):
    key_id = [sub + SUBLANES * v for v in range(KEY_TILES)]
    problems = [list(tiles) for tiles in problems]
    out = [[] for _ in problems]
    for _ in range(PEER_TOPK):
        for i, tiles in enumerate(problems):
            m = _all_sublanes(jnp.maximum, _tree(jnp.maximum, tiles))
            idx = _all_sublanes(jnp.minimum, _tree(jnp.minimum, [jnp.where(t == m, k, PEER_NKEYS)
                                                                 for t, k in zip(tiles, key_id)]))
            problems[i] = [jnp.where(k == idx, NEG_INF, t) for t, k in zip(tiles, key_id)]
            out[i].append((m, idx))
    return out


def _rows_of(ranked, sub, which, pick):
    t = ranked[pick(0)][which]
    for r in range(1, SUBLANES):
        t = jnp.where(sub == r, ranked[pick(r)][which], t)
    return t


def _product_key_topk(pairs, sub):
    def tiles(first, second, which):
        lo = _rows_of(second, sub, which, lambda r: r)
        hi = _rows_of(second, sub, which, lambda r: SUBLANES + r)
        quad = _rows_of(second, sub, which, lambda r: r % 4)
        a45 = _rows_of(first, sub, which, lambda r: 4 + r // 4)
        a67 = _rows_of(first, sub, which, lambda r: 6 + r // 4)
        ahi = _rows_of(first, sub, which, lambda r: SUBLANES + r)
        a = [first[k][which] for k in range(4)]
        return (a[0], lo), (a[0], hi), (a[1], lo), (a[2], lo), (a[3], lo), (a45, quad), (a67, quad), (ahi, second[0][which])

    cands = [[x + y for x, y in tiles(f, s, 0)] for f, s in pairs]
    cidxs = [[x * PEER_NKEYS + y for x, y in tiles(f, s, 1)] for f, s in pairs]
    quad_pos = jnp.where(sub < 4, sub, sub + (PEER_TOPK - 4))
    pos = [sub, sub + 8, sub + 16, sub + 32, sub + 48, quad_pos + 64, quad_pos + 96, (sub + 8) * PEER_TOPK]
    out = [[] for _ in pairs]
    for _ in range(PEER_TOPK):
        for i, (cand, cidx) in enumerate(zip(cands, cidxs)):
            m = _all_sublanes(jnp.maximum, _tree(jnp.maximum, cand))
            psel = _all_sublanes(jnp.minimum, _tree(jnp.minimum, [jnp.where(c == m, p, PEER_TOPK * PEER_TOPK)
                                                                  for c, p in zip(cand, pos)]))
            hits = [p == psel for p in pos]
            e = _all_sublanes(jnp.maximum, _tree(jnp.maximum, [jnp.where(hh, x, -1) for hh, x in zip(hits, cidx)]))
            cands[i] = [jnp.where(hh, NEG_INF, c) for hh, c in zip(hits, cand)]
            out[i].append((m, e))
    return out


def _peer_score_kernel(x_ref, g_ref, sh_ref, sc_ref, wq_ref, keys_ref, h_ref, eidx_ref, gate_ref,
                       q_scr, s_scr, e_scr, p_scr):
    tm = x_ref.shape[1]
    nt = (((1,), (1,)), ((), ()))
    h = _norm_mod(x_ref[0], g_ref[...], sh_ref[0], sc_ref[0])
    h_ref[0] = h
    q = jnp.dot(h.astype(BF16), wq_ref[...], preferred_element_type=F32)
    for c in range(2 * PEER_HEADS):
        q_scr[c] = q[:, c * PEER_DK_HALF:(c + 1) * PEER_DK_HALF].astype(BF16)
    sub = lax.broadcasted_iota(jnp.int32, (SUBLANES, LANES), 0)

    def head_step(hd, carry):
        for p in range(2):
            s_scr[p] = lax.dot_general(keys_ref[hd, p], q_scr[2 * hd + p], nt, preferred_element_type=F32)
        row0 = pl.multiple_of(hd * PEER_TOPK, PEER_TOPK)
        groups = tm // LANES
        cols = [slice(lg * LANES, (lg + 1) * LANES) for lg in range(groups)]
        ranked = _top_keys([[s_scr[p, v * SUBLANES:(v + 1) * SUBLANES, cols[lg]] for v in range(KEY_TILES)]
                            for lg in range(groups) for p in range(2)], sub)
        tops = _product_key_topk([(ranked[2 * lg], ranked[2 * lg + 1]) for lg in range(groups)], sub)
        for lg, top in enumerate(tops):
            ex = [(jnp.exp(m - top[0][0]), e) for m, e in top]
            denom = _tree(jnp.add, [v for v, _ in ex])
            for half in range(2):
                pick = lambda r, half=half: half * SUBLANES + r
                rows = pl.ds(row0 + half * SUBLANES, SUBLANES)
                e_scr[rows, cols[lg]] = _rows_of(ex, sub, 1, pick)
                p_scr[rows, cols[lg]] = _rows_of(ex, sub, 0, pick) / denom
        return carry

    lax.fori_loop(0, PEER_HEADS, head_step, 0)
    eidx_ref[0] = e_scr[...].T
    gate_ref[0] = p_scr[...].T


def _peer_score(x, g, sh, sc, wq, keys):
    b, s, d = x.shape
    tm = PEER_TM
    n = wq.shape[1]
    tile = lambda k: pl.BlockSpec((1, tm, k), lambda bi, i: (bi, i, 0))
    return pl.pallas_call(
        _peer_score_kernel,
        grid=(b, s // tm),
        in_specs=[tile(d),
                  pl.BlockSpec((1, d), lambda bi, i: (0, 0)),
                  _mod_spec(sh), _mod_spec(sc),
                  pl.BlockSpec((d, n), lambda bi, i: (0, 0)),
                  pl.BlockSpec(keys.shape, lambda bi, i: (0, 0, 0, 0))],
        out_specs=[tile(d), tile(PEER_SEL), tile(PEER_SEL)],
        out_shape=[jax.ShapeDtypeStruct((b, s, d), F32),
                   jax.ShapeDtypeStruct((b, s, PEER_SEL), jnp.int32),
                   jax.ShapeDtypeStruct((b, s, PEER_SEL), F32)],
        scratch_shapes=[pltpu.VMEM((2 * PEER_HEADS, tm, PEER_DK_HALF), BF16),
                        pltpu.VMEM((2, PEER_NKEYS, tm), F32),
                        pltpu.VMEM((PEER_SEL, tm), jnp.int32),
                        pltpu.VMEM((PEER_SEL, tm), F32)],
        compiler_params=_cparams(2),
        name="peer_score",
    )(x, g.reshape(1, d), sh, sc, wq, keys)


HALF_ROWS = SUBLANES // 2


PACK_EB = 256


def _pack_kernel(x_ref, o_ref):
    half = x_ref.shape[1] // 2
    is_tail = pl.program_id(0) == pl.num_programs(0) - 1
    row = lax.broadcasted_iota(jnp.int32, (PACK_EB, LANES), 0)
    for c in range(HALF_ROWS):
        lo = x_ref[:, half + c * LANES:half + (c + 1) * LANES].astype(BF16).astype(F32)
        lo_bits = pltpu.bitcast(lo, jnp.int32) >> 16 & 0xFFFF
        bits = pltpu.bitcast(x_ref[:, c * LANES:(c + 1) * LANES], jnp.int32)
        excess = lo_bits - (bits & 0xFFFF)
        step = jnp.where(excess > 0x8000, -1, jnp.where(excess < -0x8000, 1, 0))
        step = jnp.where((bits & 0x7FFF0000) == 0, jnp.maximum(step, 0), step)
        words = pltpu.bitcast(((bits & jnp.int32(-0x10000)) + (step << 16)) | lo_bits, jnp.uint32)
        tail = jnp.where(row == 1, pltpu.roll(words, 1, axis=0), jnp.uint32(0))
        o_ref[pl.ds(c, PACK_EB, stride=HALF_ROWS), :] = jnp.where(is_tail, tail, words)


def _pack_table(t):
    e, d = t.shape
    assert d // 2 == HALF_ROWS * LANES and e % PACK_EB == 0
    nb = e // PACK_EB
    return pl.pallas_call(
        _pack_kernel,
        grid=(nb + 1,),
        in_specs=[pl.BlockSpec((PACK_EB, d), lambda i: (i % nb, 0))],
        out_specs=pl.BlockSpec((PACK_EB * HALF_ROWS, LANES), lambda i: (i, 0)),
        out_shape=jax.ShapeDtypeStruct(((e + PACK_EB) * HALF_ROWS, LANES), jnp.uint32),
        compiler_params=_cparams(1),
        name="pack_table",
    )(t)


def _load_rows(eidx, on_low_sublanes, n_experts):
    high_start = jnp.where(eidx == 0, HALF_ROWS * n_experts, HALF_ROWS * (eidx - 1))
    return jnp.where(on_low_sublanes, HALF_ROWS * eidx, high_start)


def _unpack(words):
    return (pltpu.bitcast(words, F32), pltpu.bitcast(words << 16, F32))


_FOLD_POS = (6, 2, 4, 0, 7, 3, 5, 1)


def _fold_halves(p, sub):
    m = [jnp.where(sub < HALF_ROWS, p[2 * k], p[2 * k + 1]) for k in range(4)]
    n = [jnp.where((sub & 2) != 0, m[2 * k] + pltpu.roll(m[2 * k], 2, axis=0),
                   m[2 * k + 1] + pltpu.roll(m[2 * k + 1], 6, axis=0)) for k in range(2)]
    return jnp.where((sub & 1) != 0, n[0] + pltpu.roll(n[0], 1, axis=0), n[1] + pltpu.roll(n[1], 7, axis=0))


DOWN_TM = 256
DOWN_BLOCK = 16


def _token_tiles(h_ref, t0, sub):
    chunks = [h_ref[pl.ds(t0, SUBLANES), r * LANES:(r + 1) * LANES] for r in range(SUBLANES)]
    tiles = []
    for s in range(SUBLANES):
        tile = None
        for r in range(SUBLANES):
            piece = jnp.broadcast_to(chunks[r][s:s + 1, :], (SUBLANES, LANES))
            tile = piece if tile is None else jnp.where(sub == r, piece, tile)
        tiles.append(tile)
    return tiles


def _peer_down_kernel(*refs):
    row_refs = refs[:SUBLANES]
    h_ref, gate_ref, tbl_ref, o_ref = refs[SUBLANES:]
    tm = h_ref.shape[0]
    lane = lax.broadcasted_iota(jnp.int32, (SUBLANES, LANES), 1)
    sub = lax.broadcasted_iota(jnp.int32, (SUBLANES, LANES), 0)

    def group(gi, carry):
        t0 = pl.multiple_of(gi * SUBLANES, SUBLANES)
        hs = []
        for s, tile in enumerate(_token_tiles(h_ref, t0, sub)):
            swapped = pltpu.roll(tile, HALF_ROWS, axis=0)
            hs.append((tile, swapped) if _FOLD_POS[s] % 2 == 0 else (swapped, tile))
        gbase = gi * PEER_SEL

        def folded(j):
            prods = [None] * SUBLANES
            for s in range(SUBLANES):
                hi, lo = _unpack(tbl_ref[pl.ds(row_refs[s][gbase + j], SUBLANES), :])
                prods[_FOLD_POS[s]] = hi * hs[s][0] + lo * hs[s][1]
            return _fold_halves(prods, sub)

        def place(acc, j0, folds):
            sums = [jnp.sum(f, axis=-1, keepdims=True) for f in folds]
            for k, tot in enumerate(sums):
                acc = jnp.where(lane == j0 + k, tot, acc)
            return acc

        def block(jb, carry2):
            acc, prev = carry2
            j0 = jb * DOWN_BLOCK
            new = tuple(folded(j0 + k) for k in range(DOWN_BLOCK))
            return place(acc, j0 - DOWN_BLOCK, prev), new

        first = tuple(folded(k) for k in range(DOWN_BLOCK))
        acc, last = lax.fori_loop(1, PEER_SEL // DOWN_BLOCK, block, (jnp.zeros((SUBLANES, LANES), F32), first))
        act = place(acc, PEER_SEL - DOWN_BLOCK, last)
        gelu = act * (lax.erf(act / np.sqrt(2).astype(np.float32)) + 1.0) / 2.0
        o_ref[pl.ds(t0, SUBLANES), :] = gate_ref[pl.ds(t0, SUBLANES), :] * gelu
        return carry

    lax.fori_loop(0, tm // SUBLANES, group, 0)


def _peer_down(eidx, h, gate, tbl):
    t, d = h.shape
    tm = DOWN_TM
    low = (jnp.asarray(_FOLD_POS, jnp.int32) % 2 == 0)[None, :, None]
    rows = _load_rows(eidx.reshape(t // SUBLANES, SUBLANES, PEER_SEL), low, tbl.shape[0] // HALF_ROWS - PACK_EB)
    rows = rows.transpose(1, 0, 2).reshape(SUBLANES, -1)
    smem = pl.BlockSpec((tm // SUBLANES * PEER_SEL,), lambda i: (i,), memory_space=pltpu.SMEM)
    return pl.pallas_call(
        _peer_down_kernel,
        grid=(t // tm,),
        in_specs=[smem] * SUBLANES + [
            pl.BlockSpec((tm, d), lambda i: (i, 0)),
            pl.BlockSpec((tm, PEER_SEL), lambda i: (i, 0)),
            pl.BlockSpec(tbl.shape, lambda i: (0, 0), pipeline_mode=pl.Buffered(1))],
        out_specs=pl.BlockSpec((tm, PEER_SEL), lambda i: (i, 0)),
        out_shape=jax.ShapeDtypeStruct((t, PEER_SEL), F32),
        compiler_params=_cparams(1),
        name="peer_down",
    )(*[rows[s] for s in range(SUBLANES)], h, gate, tbl)


UP_TM = 128
UP_STEP = 4
UP_CHUNK = 128


def _peer_up_kernel(*refs, final_norm):
    row_refs = refs[:UP_STEP]
    w_ref, eye_ref, x_ref, gate_ref, fg_ref, tbl_ref, o_ref, stage_scr, wv_scr = refs[UP_STEP:]
    tm, d = o_ref.shape
    steps = PEER_SEL // UP_STEP
    sub = lax.broadcasted_iota(jnp.int32, (SUBLANES, LANES), 0)
    low = sub < HALF_ROWS
    ones = jnp.ones((PEER_SEL, LANES), BF16)
    zero = jnp.zeros((SUBLANES, LANES), F32)

    def spread(t, slot):
        wrow = w_ref[pl.ds(t, 1), :]
        p0 = wrow.astype(BF16)
        r1 = wrow - p0.astype(F32)
        p1 = r1.astype(BF16)
        p2 = (r1 - p1.astype(F32)).astype(BF16)
        eye = eye_ref[...]
        wv_scr[slot] = (jnp.dot(eye * p0, ones, preferred_element_type=F32)
                        + jnp.dot(eye * p1, ones, preferred_element_type=F32)
                        + jnp.dot(eye * p2, ones, preferred_element_type=F32))

    def accumulate(t, slot):
        wv_rows = wv_scr.at[slot]
        tbase = t * steps
        accs = [zero] * UP_STEP
        for jj in range(steps):
            k = tbase + jj
            for u in range(0, UP_STEP, 2):
                j = jj * UP_STEP + u
                words = jnp.where(low, tbl_ref[pl.ds(row_refs[u][k], SUBLANES), :],
                                  tbl_ref[pl.ds(row_refs[u + 1][k], SUBLANES), :])
                wv = jnp.where(low, jnp.broadcast_to(wv_rows[j:j + 1, :], (SUBLANES, LANES)),
                               jnp.broadcast_to(wv_rows[j + 1:j + 2, :], (SUBLANES, LANES)))
                hi, lo = _unpack(words)
                accs[u] = accs[u] + wv * hi
                accs[u + 1] = accs[u + 1] + wv * lo
        hi = _tree(jnp.add, accs[0::2])
        lo = _tree(jnp.add, accs[1::2])
        out = jnp.where(low, hi + pltpu.roll(hi, HALF_ROWS, axis=0), lo + pltpu.roll(lo, HALF_ROWS, axis=0))
        t_in_group = t % SUBLANES
        stage_scr[pl.ds(pl.multiple_of(t_in_group * SUBLANES, SUBLANES), SUBLANES), :] = out
        return t_in_group

    def token_pair(i, carry):
        t0 = 2 * i
        cur = 2 * (i % 2)
        accumulate(t0, cur)
        t_in_group = accumulate(t0 + 1, cur + 1)
        spread(jnp.minimum(t0 + 2, tm - 1), 2 - cur)
        spread(jnp.minimum(t0 + 3, tm - 1), 3 - cur)

        @pl.when(t_in_group == SUBLANES - 1)
        def _():
            g0 = pl.multiple_of(t0 + 1 - (SUBLANES - 1), SUBLANES)
            ys = []
            for r in range(SUBLANES):
                cols = slice(r * LANES, (r + 1) * LANES)
                ys.append(x_ref[pl.ds(g0, SUBLANES), cols]
                          + gate_ref[0, :, cols] * stage_scr[pl.ds(r, SUBLANES, stride=SUBLANES), :])
            if final_norm:
                ss = _tree(jnp.add, [jnp.sum(y * y, axis=-1, keepdims=True) for y in ys])
                scale = lax.rsqrt(ss / d + EPS)
                ys = [(y * scale) * fg_ref[:, r * LANES:(r + 1) * LANES] for r, y in enumerate(ys)]
            for r, y in enumerate(ys):
                o_ref[pl.ds(g0, SUBLANES), r * LANES:(r + 1) * LANES] = y

        return carry

    spread(0, 0)
    spread(1, 1)
    lax.fori_loop(0, tm // 2, token_pair, 0)


def _peer_up(eidx, w, tbl, x, gate, final_g):
    b, s, d = x.shape
    t = b * s
    tm = UP_TM
    assert s % tm == 0
    fg = jnp.ones((1, d), F32) if final_g is None else final_g.reshape(1, d)
    rows = _load_rows(eidx, (jnp.arange(PEER_SEL) % 2 == 0)[None, :], tbl.shape[0] // HALF_ROWS - PACK_EB)
    rows = rows.reshape(t, PEER_SEL // UP_STEP, UP_STEP).transpose(2, 0, 1).reshape(UP_STEP, -1)
    smem = pl.BlockSpec((tm * PEER_SEL // UP_STEP,), lambda i: (i,), memory_space=pltpu.SMEM)
    tiles_per_batch = s // tm
    out = pl.pallas_call(
        functools.partial(_peer_up_kernel, final_norm=final_g is not None),
        grid=(t // tm,),
        in_specs=[smem] * UP_STEP + [
            pl.BlockSpec((tm, PEER_SEL), lambda i: (i, 0)),
            pl.BlockSpec((PEER_SEL, PEER_SEL), lambda i: (0, 0)),
            pl.BlockSpec((tm, d), lambda i: (i, 0)),
            pl.BlockSpec((1, 1, d), lambda i: (i // tiles_per_batch, 0, 0)),
            pl.BlockSpec((1, d), lambda i: (0, 0)),
            pl.BlockSpec(tbl.shape, lambda i: (0, 0), pipeline_mode=pl.Buffered(1))],
        out_specs=pl.BlockSpec((tm, d), lambda i: (i, 0)),
        out_shape=jax.ShapeDtypeStruct((t, d), F32),
        scratch_shapes=[pltpu.VMEM((SUBLANES * SUBLANES, LANES), F32),
                        pltpu.VMEM((4, PEER_SEL, LANES), F32)],
        compiler_params=_cparams(1),
        name="peer_up",
    )(*[rows[u] for u in range(UP_STEP)], w, jnp.eye(PEER_SEL, dtype=BF16), x.reshape(t, d), gate, fg, tbl)
    return out.reshape(b, s, d)


def _peer_block(x, g, sh, sc, gate2, wq, keys, down, up, final_g):
    b, s, d = x.shape
    t = b * s
    h, eidx, gsm = _peer_score(x, g, sh, sc, wq.astype(BF16), keys.astype(BF16))
    eidx = eidx.reshape(t, PEER_SEL)
    w = _peer_down(eidx, h.reshape(t, d), gsm.reshape(t, PEER_SEL), _pack_table(down))
    return _peer_up(eidx, w, _pack_table(up), x, gate2, final_g)


def kernel(x, c, ctx, c_ctx, ada_w, ada_b, norm1_g, norm2_g, final_g, ab_w_in, ab_w_out, na_rpb, fn_w,
           cv_w_in, cv_w, cv_w_out, peer_w_q, peer_keys, peer_down, peer_up):
    b, s, d = x.shape
    depth = ada_w.shape[0]
    rows = -(-(b + 1) // SUBLANES) * SUBLANES
    cc = jnp.concatenate([c, c_ctx[None], jnp.zeros((rows - b - 1, d), F32)], axis=0)
    mod = _ada_vectors(cc, ada_w, ada_b)

    def chunks(i, lo, hi):
        m = mod[i, lo:hi].reshape(hi - lo, 1, 6, d)
        return [m[:, :, k] for k in range(6)]

    for i in range(depth):
        sh1, sc1, g1, sh2, sc2, g2 = chunks(i, 0, b)
        if i % 2 == 0:
            e = i // 2
            csh1, csc1 = chunks(i, b, b + 1)[:2]
            w_in = ab_w_in[e].astype(BF16)
            qkv, f = _norm_mod_matmul(x, norm1_g[i], sh1, sc1, w_in, (3 * NA_WIDTH, FN_WIDTH), (BF16, BF16),
                                      512, "ab_in")
            (ctx_kv,) = _norm_mod_matmul(ctx, norm1_g[i], csh1, csc1, w_in[:, NA_WIDTH:3 * NA_WIDTH],
                                         (2 * NA_WIDTH,), (BF16,), ctx.shape[1], "ab_in_ctx")
            a = _attention(qkv, ctx_kv, _attention_bias(na_rpb[e]))
            fm = _fnet(f, fn_w[e])
            x = _proj_residual(a, fm, x, g1, ab_w_out[e].astype(BF16), 512)
        else:
            o = i // 2
            bg, u = _conv_in(x, norm1_g[i], sh1, sc1, cv_w_in[o].astype(BF16), 256)
            x = _conv_out(bg, u, cv_w[o], x, g1, cv_w_out[o].astype(BF16), 512)
        x = _peer_block(x, norm2_g[i], sh2, sc2, g2, peer_w_q[i], peer_keys[i], peer_down[i], peer_up[i],
                        final_g if i == depth - 1 else None)
    return x
```

```python
import functools

import jax
import jax.numpy as jnp
import numpy as np
from jax import lax
from jax.experimental import pallas as pl
from jax.experimental.pallas import tpu as pltpu

F32 = jnp.float32
BF16 = jnp.bfloat16
EPS = 1e-6

GRID_W = 64
HEAD_DIM = 64
NA_HEADS = 8
NA_WIDTH = NA_HEADS * HEAD_DIM
KH = 8
KH_MAX = 8
KW = 16
FN_GROUPS = 8
FN_GROUP_DIM = 64
FN_WIDTH = FN_GROUPS * FN_GROUP_DIM
PEER_HEADS = 8
PEER_NKEYS = 128
PEER_DK_HALF = 128
PEER_TOPK = 16
PEER_SEL = PEER_HEADS * PEER_TOPK
LANES = 128
SUBLANES = 8
VMEM_LIMIT = 56 * 1024 * 1024

NEG_INF = float("-inf")


def _cparams(n_axes):
    return pltpu.CompilerParams(dimension_semantics=("arbitrary",) * n_axes, vmem_limit_bytes=VMEM_LIMIT)


def _norm_mod(x, g, sh, sc):
    y = x * lax.rsqrt(jnp.mean(x * x, axis=-1, keepdims=True) + EPS)
    return (y * g) * (1.0 + sc) + sh


def _ada_kernel(c_ref, w_ref, b_ref, o_ref):
    c = c_ref[...]
    s = c / (1.0 + jnp.exp(-c))
    o_ref[0] = jnp.dot(s, w_ref[0], preferred_element_type=F32, precision=lax.Precision.HIGHEST) + b_ref[0]


def _ada_vectors(cc, ada_w, ada_b):
    depth, d, n = ada_w.shape
    rows = cc.shape[0]
    tn = 1536
    return pl.pallas_call(
        _ada_kernel,
        grid=(depth, n // tn),
        in_specs=[pl.BlockSpec((rows, d), lambda i, j: (0, 0)),
                  pl.BlockSpec((1, d, tn), lambda i, j: (i, 0, j)),
                  pl.BlockSpec((1, 1, tn), lambda i, j: (i, 0, j))],
        out_specs=pl.BlockSpec((1, rows, tn), lambda i, j: (i, 0, j)),
        out_shape=jax.ShapeDtypeStruct((depth, rows, n), F32),
        compiler_params=_cparams(2),
        name="ada_vectors",
    )(cc, ada_w, ada_b.reshape(depth, 1, n))


def _nmm_kernel(x_ref, g_ref, sh_ref, sc_ref, w_ref, *o_refs):
    h = _norm_mod(x_ref[0], g_ref[...], sh_ref[0], sc_ref[0])
    acc = jnp.dot(h.astype(BF16), w_ref[...], preferred_element_type=F32)
    off = 0
    for o_ref in o_refs:
        n = o_ref.shape[-1]
        o_ref[0] = acc[:, off:off + n].astype(o_ref.dtype)
        off += n


def _mod_spec(arr):
    d = arr.shape[-1]
    if arr.shape[0] == 1:
        return pl.BlockSpec((1, 1, d), lambda b, i: (0, 0, 0))
    return pl.BlockSpec((1, 1, d), lambda b, i: (b, 0, 0))


def _norm_mod_matmul(x, g, sh, sc, w, splits, dtypes, tm, name):
    b, s, d = x.shape
    n = w.shape[1]
    assert sum(splits) == n
    return pl.pallas_call(
        _nmm_kernel,
        grid=(b, s // tm),
        in_specs=[pl.BlockSpec((1, tm, d), lambda bi, i: (bi, i, 0)),
                  pl.BlockSpec((1, d), lambda bi, i: (0, 0)),
                  _mod_spec(sh), _mod_spec(sc),
                  pl.BlockSpec((d, n), lambda bi, i: (0, 0))],
        out_specs=[pl.BlockSpec((1, tm, k), lambda bi, i: (bi, i, 0)) for k in splits],
        out_shape=[jax.ShapeDtypeStruct((b, s, k), dt) for k, dt in zip(splits, dtypes)],
        compiler_params=_cparams(2),
        name=name,
    )(x, g.reshape(1, d), sh, sc, w)


def _attn_kernel(qkv_ref, ctx_ref, bias_ref, o_ref):
    rows = qkv_ref.shape[1] // GRID_W
    nt = (((1,), (1,)), ((), ()))
    scale = HEAD_DIM ** -0.5

    def row_step(r, carry):
        start = jnp.clip(r - KH // 2, 0, rows - KH)
        off = start - r + (KH_MAX - 1)
        q0 = pl.multiple_of(r * GRID_W, GRID_W)
        k0 = pl.multiple_of(start * GRID_W, GRID_W)
        heads = range(NA_HEADS)
        cols = [slice(h * HEAD_DIM, (h + 1) * HEAD_DIM) for h in heads]
        kcols = [slice(NA_WIDTH + h * HEAD_DIM, NA_WIDTH + (h + 1) * HEAD_DIM) for h in heads]
        vcols = [slice(2 * NA_WIDTH + h * HEAD_DIM, 2 * NA_WIDTH + (h + 1) * HEAD_DIM) for h in heads]
        q = [qkv_ref[0, pl.ds(q0, GRID_W), cols[h]] for h in heads]
        s_win = [lax.dot_general(q[h], qkv_ref[0, pl.ds(k0, KH * GRID_W), kcols[h]], nt,
                                 preferred_element_type=F32) * scale + bias_ref[h, off] for h in heads]
        s_ctx = [lax.dot_general(q[h], ctx_ref[0, :, cols[h]], nt, preferred_element_type=F32) * scale
                 for h in heads]
        m = [jnp.maximum(jnp.max(s_win[h], axis=-1, keepdims=True), jnp.max(s_ctx[h], axis=-1, keepdims=True))
             for h in heads]
        p_win = [jnp.exp(s_win[h] - m[h]) for h in heads]
        p_ctx = [jnp.exp(s_ctx[h] - m[h]) for h in heads]
        l = [jnp.sum(p_win[h], axis=-1, keepdims=True) + jnp.sum(p_ctx[h], axis=-1, keepdims=True) for h in heads]
        o = [jnp.dot(p_win[h].astype(BF16), qkv_ref[0, pl.ds(k0, KH * GRID_W), vcols[h]], preferred_element_type=F32)
             + jnp.dot(p_ctx[h].astype(BF16), ctx_ref[0, :, kcols[h]], preferred_element_type=F32) for h in heads]
        outs = [o[h] / l[h] for h in heads]
        o_ref[0, pl.ds(q0, GRID_W), :] = jnp.concatenate(outs, axis=-1).astype(o_ref.dtype)
        return carry

    lax.fori_loop(0, rows, row_step, 0)


def _attention_bias(rpb):
    cols = np.arange(GRID_W)
    col_start = np.clip(cols - KW // 2, 0, GRID_W - KW)
    in_win = (cols[None, :] >= col_start[:, None]) & (cols[None, :] < col_start[:, None] + KW)
    col_off = np.clip(cols[None, :] - cols[:, None] + KW - 1, 0, 2 * KW - 2)
    onehot = jnp.asarray(col_off[None] == np.arange(2 * KW - 1)[:, None, None], F32)
    by_row = jnp.sum(rpb.astype(F32)[:, :, :, None, None] * onehot[None, None], axis=2)
    by_row = jnp.where(in_win[None, None], by_row, NEG_INF)
    b = jnp.stack([by_row[:, o:o + KH] for o in range(KH)], axis=1)
    return b.transpose(0, 1, 3, 2, 4).reshape(NA_HEADS, KH, GRID_W, KH * GRID_W)


def _attention(qkv, ctx_kv, bias):
    b, s, _ = qkv.shape
    l = ctx_kv.shape[1]
    return pl.pallas_call(
        _attn_kernel,
        grid=(b,),
        in_specs=[pl.BlockSpec((1, s, 3 * NA_WIDTH), lambda i: (i, 0, 0)),
                  pl.BlockSpec((1, l, 2 * NA_WIDTH), lambda i: (i, 0, 0)),
                  pl.BlockSpec(bias.shape, lambda i: (0, 0, 0, 0), pipeline_mode=pl.Buffered(1))],
        out_specs=pl.BlockSpec((1, s, NA_WIDTH), lambda i: (i, 0, 0)),
        out_shape=jax.ShapeDtypeStruct((b, s, NA_WIDTH), BF16),
        compiler_params=_cparams(1),
        name="nbr_attention",
    )(qkv, ctx_kv, bias)


FN_ROW_CHUNK = 256


def _fnet_kernel(f_ref, cn_ref, sn_ref, c64_ref, s64_ref, wbd_ref, o_ref, a_scr, b_scr, xa_scr, xb_scr):
    n = f_ref.shape[1]
    scale = (n * FN_GROUP_DIM) ** -0.5

    @pl.when(pl.program_id(0) == 0)
    def _():
        hi = lax.Precision.HIGHEST
        a_scr[...] = (jnp.dot(c64_ref[...], wbd_ref[...], preferred_element_type=F32, precision=hi) * scale).astype(BF16)
        b_scr[...] = (jnp.dot(s64_ref[...], wbd_ref[...], preferred_element_type=F32, precision=hi) * -scale).astype(BF16)

    x = f_ref[0]
    xa_scr[...] = jnp.dot(x, a_scr[...], preferred_element_type=F32).astype(BF16)
    xb_scr[...] = jnp.dot(x, b_scr[...], preferred_element_type=F32).astype(BF16)

    def chunk(i, carry):
        r0 = pl.multiple_of(i * FN_ROW_CHUNK, FN_ROW_CHUNK)
        y = (jnp.dot(cn_ref[pl.ds(r0, FN_ROW_CHUNK), :], xa_scr[...], preferred_element_type=F32)
             + jnp.dot(sn_ref[pl.ds(r0, FN_ROW_CHUNK), :], xb_scr[...], preferred_element_type=F32))
        o_ref[0, pl.ds(r0, FN_ROW_CHUNK), :] = y.astype(o_ref.dtype)
        return carry

    lax.fori_loop(0, n // FN_ROW_CHUNK, chunk, 0)


def _dft_tables(n):
    k = (np.arange(n)[:, None] * np.arange(n)[None, :]) % n
    ang = 2.0 * np.pi * k.astype(np.float64) / n
    return np.cos(ang), np.sin(ang)


def _fnet(f, fn_w):
    b, s, _ = f.shape
    cn, sn = _dft_tables(s)
    c64, s64 = _dft_tables(FN_GROUP_DIM)
    eye = np.eye(FN_GROUPS)
    c64bd = jnp.asarray(np.kron(eye, c64), F32)
    s64bd = jnp.asarray(np.kron(eye, s64), F32)
    wbd = (jnp.asarray(eye, F32)[:, None, :, None] * fn_w[:, :, None, :]).reshape(FN_WIDTH, FN_WIDTH)
    const = lambda shape: pl.BlockSpec(shape, lambda i: (0, 0), pipeline_mode=pl.Buffered(1))
    return pl.pallas_call(
        _fnet_kernel,
        grid=(b,),
        in_specs=[pl.BlockSpec((1, s, FN_WIDTH), lambda i: (i, 0, 0)),
                  const((s, s)), const((s, s)),
                  const((FN_WIDTH, FN_WIDTH)), const((FN_WIDTH, FN_WIDTH)), const((FN_WIDTH, FN_WIDTH))],
        out_specs=pl.BlockSpec((1, s, FN_WIDTH), lambda i: (i, 0, 0)),
        out_shape=jax.ShapeDtypeStruct((b, s, FN_WIDTH), BF16),
        scratch_shapes=[pltpu.VMEM((FN_WIDTH, FN_WIDTH), BF16), pltpu.VMEM((FN_WIDTH, FN_WIDTH), BF16),
                        pltpu.VMEM((s, FN_WIDTH), BF16), pltpu.VMEM((s, FN_WIDTH), BF16)],
        compiler_params=_cparams(1),
        name="fnet_mix",
    )(f, jnp.asarray(cn, BF16), jnp.asarray(sn, BF16), c64bd, s64bd, wbd)


def _proj_res_kernel(a_ref, f_ref, x_ref, g_ref, w_ref, o_ref):
    ka = a_ref.shape[-1]
    out = (jnp.dot(a_ref[0], w_ref[:ka, :], preferred_element_type=F32)
           + jnp.dot(f_ref[0], w_ref[ka:, :], preferred_element_type=F32))
    o_ref[0] = x_ref[0] + g_ref[0] * out


def _proj_residual(a, f, x, gate, w, tm):
    b, s, d = x.shape
    ka, kf = a.shape[-1], f.shape[-1]
    return pl.pallas_call(
        _proj_res_kernel,
        grid=(b, s // tm),
        in_specs=[pl.BlockSpec((1, tm, ka), lambda bi, i: (bi, i, 0)),
                  pl.BlockSpec((1, tm, kf), lambda bi, i: (bi, i, 0)),
                  pl.BlockSpec((1, tm, d), lambda bi, i: (bi, i, 0)),
                  _mod_spec(gate),
                  pl.BlockSpec((ka + kf, d), lambda bi, i: (0, 0))],
        out_specs=pl.BlockSpec((1, tm, d), lambda bi, i: (bi, i, 0)),
        out_shape=jax.ShapeDtypeStruct((b, s, d), F32),
        compiler_params=_cparams(2),
        name="proj_residual",
    )(a, f, x, gate, w)


def _conv_in_kernel(x_ref, g_ref, sh_ref, sc_ref, w_ref, bg_ref, u_ref):
    d = bg_ref.shape[-1]
    h = _norm_mod(x_ref[0], g_ref[...], sh_ref[0], sc_ref[0])
    acc = jnp.dot(h.astype(BF16), w_ref[...], preferred_element_type=F32)
    bg_ref[0] = acc[:, :d]
    u_ref[0] = acc[:, d:2 * d] * acc[:, 2 * d:]


def _conv_in(x, g, sh, sc, w, tm):
    b, s, d = x.shape
    n = w.shape[1]
    return pl.pallas_call(
        _conv_in_kernel,
        grid=(b, s // tm),
        in_specs=[pl.BlockSpec((1, tm, d), lambda bi, i: (bi, i, 0)),
                  pl.BlockSpec((1, d), lambda bi, i: (0, 0)),
                  _mod_spec(sh), _mod_spec(sc),
                  pl.BlockSpec((d, n), lambda bi, i: (0, 0))],
        out_specs=[pl.BlockSpec((1, tm, d), lambda bi, i: (bi, i, 0))] * 2,
        out_shape=[jax.ShapeDtypeStruct((b, s, d), F32)] * 2,
        compiler_params=_cparams(2),
        name="conv_in",
    )(x, g.reshape(1, d), sh, sc, w)


def _conv_out_kernel(bg_ref, u_ref, up_ref, un_ref, cw_ref, x_ref, g_ref, w_ref, o_ref):
    i = pl.program_id(1)
    last = pl.num_programs(1) - 1
    u = u_ref[0]
    tm = u.shape[0]
    row = lax.broadcasted_iota(jnp.int32, u.shape, 0)
    prev_row = jnp.where(i == 0, 0.0, up_ref[0, SUBLANES - 1:SUBLANES, :])
    next_row = jnp.where(i == last, 0.0, un_ref[0, 0:1, :])
    u_prev = jnp.where(row == 0, prev_row, pltpu.roll(u, 1, axis=0))
    u_next = jnp.where(row == tm - 1, next_row, pltpu.roll(u, tm - 1, axis=0))
    y = cw_ref[0:1, :] * u_prev + cw_ref[1:2, :] * u + cw_ref[2:3, :] * u_next
    z = (bg_ref[0] * y).astype(BF16)
    o_ref[0] = x_ref[0] + g_ref[0] * jnp.dot(z, w_ref[...], preferred_element_type=F32)


def _conv_out(bg, u, cw, x, gate, w, tm):
    b, s, d = x.shape
    hb = tm // SUBLANES
    nhb = s // SUBLANES
    tile = pl.BlockSpec((1, tm, d), lambda bi, i: (bi, i, 0))
    return pl.pallas_call(
        _conv_out_kernel,
        grid=(b, s // tm),
        in_specs=[tile, tile,
                  pl.BlockSpec((1, SUBLANES, d), lambda bi, i: (bi, jnp.maximum(i * hb - 1, 0), 0)),
                  pl.BlockSpec((1, SUBLANES, d), lambda bi, i: (bi, jnp.minimum((i + 1) * hb, nhb - 1), 0)),
                  pl.BlockSpec(cw.shape, lambda bi, i: (0, 0)),
                  tile, _mod_spec(gate),
                  pl.BlockSpec((d, d), lambda bi, i: (0, 0))],
        out_specs=tile,
        out_shape=jax.ShapeDtypeStruct((b, s, d), F32),
        compiler_params=_cparams(2),
        name="conv_out",
    )(bg, u, u, u, cw, x, gate, w)


PEER_TM = 256
KEY_TILES = PEER_NKEYS // SUBLANES
assert PEER_TOPK == 2 * SUBLANES


def _tree(op, xs):
    xs = list(xs)
    while len(xs) > 1:
        xs = [op(xs[i], xs[i + 1]) for i in range(0, len(xs) - 1, 2)] + ([xs[-1]] if len(xs) % 2 else [])
    return xs[0]


def _all_sublanes(op, x):
    for shift in (4, 2, 1):
        x = op(x, pltpu.roll(x, shift, axis=0))
    return x


def _top_keys(problems, sub):
    key_id = [sub + SUBLANES * v for v in range(KEY_TILES)]
    problems = [list(tiles) for tiles in problems]
    out = [[] for _ in problems]
    for _ in range(PEER_TOPK):
        for i, tiles in enumerate(problems):
            m = _all_sublanes(jnp.maximum, _tree(jnp.maximum, tiles))
            idx = _all_sublanes(jnp.minimum, _tree(jnp.minimum, [jnp.where(t == m, k, PEER_NKEYS)
                                                                 for t, k in zip(tiles, key_id)]))
            problems[i] = [jnp.where(k == idx, NEG_INF, t) for t, k in zip(tiles, key_id)]
            out[i].append((m, idx))
    return out


def _rows_of(ranked, sub, which, pick):
    t = ranked[pick(0)][which]
    for r in range(1, SUBLANES):
        t = jnp.where(sub == r, ranked[pick(r)][which], t)
    return t


def _product_key_topk(pairs, sub):
    def tiles(first, second, which):
        lo = _rows_of(second, sub, which, lambda r: r)
        hi = _rows_of(second, sub, which, lambda r: SUBLANES + r)
        quad = _rows_of(second, sub, which, lambda r: r % 4)
        a45 = _rows_of(first, sub, which, lambda r: 4 + r // 4)
        a67 = _rows_of(first, sub, which, lambda r: 6 + r // 4)
        ahi = _rows_of(first, sub, which, lambda r: SUBLANES + r)
        a = [first[k][which] for k in range(4)]
        return (a[0], lo), (a[0], hi), (a[1], lo), (a[2], lo), (a[3], lo), (a45, quad), (a67, quad), (ahi, second[0][which])

    cands = [[x + y for x, y in tiles(f, s, 0)] for f, s in pairs]
    cidxs = [[x * PEER_NKEYS + y for x, y in tiles(f, s, 1)] for f, s in pairs]
    quad_pos = jnp.where(sub < 4, sub, sub + (PEER_TOPK - 4))
    pos = [sub, sub + 8, sub + 16, sub + 32, sub + 48, quad_pos + 64, quad_pos + 96, (sub + 8) * PEER_TOPK]
    out = [[] for _ in pairs]
    for _ in range(PEER_TOPK):
        for i, (cand, cidx) in enumerate(zip(cands, cidxs)):
            m = _all_sublanes(jnp.maximum, _tree(jnp.maximum, cand))
            psel = _all_sublanes(jnp.minimum, _tree(jnp.minimum, [jnp.where(c == m, p, PEER_TOPK * PEER_TOPK)
                                                                  for c, p in zip(cand, pos)]))
            hits = [p == psel for p in pos]
            e = _all_sublanes(jnp.maximum, _tree(jnp.maximum, [jnp.where(hh, x, -1) for hh, x in zip(hits, cidx)]))
            cands[i] = [jnp.where(hh, NEG_INF, c) for hh, c in zip(hits, cand)]
            out[i].append((m, e))
    return out


def _peer_score_kernel(x_ref, g_ref, sh_ref, sc_ref, wq_ref, keys_ref, h_ref, eidx_ref, gate_ref,
                       q_scr, s_scr, e_scr, p_scr):
    tm = x_ref.shape[1]
    nt = (((1,), (1,)), ((), ()))
    h = _norm_mod(x_ref[0], g_ref[...], sh_ref[0], sc_ref[0])
    h_ref[0] = h
    q = jnp.dot(h.astype(BF16), wq_ref[...], preferred_element_type=F32)
    for c in range(2 * PEER_HEADS):
        q_scr[c] = q[:, c * PEER_DK_HALF:(c + 1) * PEER_DK_HALF].astype(BF16)
    sub = lax.broadcasted_iota(jnp.int32, (SUBLANES, LANES), 0)

    def head_step(hd, carry):
        for p in range(2):
            s_scr[p] = lax.dot_general(keys_ref[hd, p], q_scr[2 * hd + p], nt, preferred_element_type=F32)
        row0 = pl.multiple_of(hd * PEER_TOPK, PEER_TOPK)
        groups = tm // LANES
        cols = [slice(lg * LANES, (lg + 1) * LANES) for lg in range(groups)]
        ranked = _top_keys([[s_scr[p, v * SUBLANES:(v + 1) * SUBLANES, cols[lg]] for v in range(KEY_TILES)]
                            for lg in range(groups) for p in range(2)], sub)
        tops = _product_key_topk([(ranked[2 * lg], ranked[2 * lg + 1]) for lg in range(groups)], sub)
        for lg, top in enumerate(tops):
            ex = [(jnp.exp(m - top[0][0]), e) for m, e in top]
            denom = _tree(jnp.add, [v for v, _ in ex])
            for half in range(2):
                pick = lambda r, half=half: half * SUBLANES + r
                rows = pl.ds(row0 + half * SUBLANES, SUBLANES)
                e_scr[rows, cols[lg]] = _rows_of(ex, sub, 1, pick)
                p_scr[rows, cols[lg]] = _rows_of(ex, sub, 0, pick) / denom
        return carry

    lax.fori_loop(0, PEER_HEADS, head_step, 0)
    eidx_ref[0] = e_scr[...].T
    gate_ref[0] = p_scr[...].T


def _peer_score(x, g, sh, sc, wq, keys):
    b, s, d = x.shape
    tm = PEER_TM
    n = wq.shape[1]
    tile = lambda k: pl.BlockSpec((1, tm, k), lambda bi, i: (bi, i, 0))
    return pl.pallas_call(
        _peer_score_kernel,
        grid=(b, s // tm),
        in_specs=[tile(d),
                  pl.BlockSpec((1, d), lambda bi, i: (0, 0)),
                  _mod_spec(sh), _mod_spec(sc),
                  pl.BlockSpec((d, n), lambda bi, i: (0, 0)),
                  pl.BlockSpec(keys.shape, lambda bi, i: (0, 0, 0, 0))],
        out_specs=[tile(d), tile(PEER_SEL), tile(PEER_SEL)],
        out_shape=[jax.ShapeDtypeStruct((b, s, d), F32),
                   jax.ShapeDtypeStruct((b, s, PEER_SEL), jnp.int32),
                   jax.ShapeDtypeStruct((b, s, PEER_SEL), F32)],
        scratch_shapes=[pltpu.VMEM((2 * PEER_HEADS, tm, PEER_DK_HALF), BF16),
                        pltpu.VMEM((2, PEER_NKEYS, tm), F32),
                        pltpu.VMEM((PEER_SEL, tm), jnp.int32),
                        pltpu.VMEM((PEER_SEL, tm), F32)],
        compiler_params=_cparams(2),
        name="peer_score",
    )(x, g.reshape(1, d), sh, sc, wq, keys)


HALF_ROWS = SUBLANES // 2


PACK_EB = 256
DOWN_WHOLE_WORD = True
UP_WHOLE_WORD = False


def _pack_kernel(x_ref, o_ref, *, whole_word):
    half = x_ref.shape[1] // 2
    is_tail = pl.program_id(0) == pl.num_programs(0) - 1
    row = lax.broadcasted_iota(jnp.int32, (PACK_EB, LANES), 0)
    for c in range(HALF_ROWS):
        lo = x_ref[:, half + c * LANES:half + (c + 1) * LANES].astype(BF16).astype(F32)
        lo_bits = pltpu.bitcast(lo, jnp.int32) >> 16 & 0xFFFF
        hi = x_ref[:, c * LANES:(c + 1) * LANES]
        if whole_word:
            bits = pltpu.bitcast(hi, jnp.int32)
            excess = lo_bits - (bits & 0xFFFF)
            step = jnp.where(excess > 0x8000, -1, jnp.where(excess < -0x8000, 1, 0))
            step = jnp.where((bits & 0x7FFF0000) == 0, jnp.maximum(step, 0), step)
            hi_bits = (bits & jnp.int32(-0x10000)) + (step << 16)
        else:
            hi_bits = pltpu.bitcast(hi.astype(BF16).astype(F32), jnp.int32)
        words = pltpu.bitcast(hi_bits | lo_bits, jnp.uint32)
        tail = jnp.where(row == 1, pltpu.roll(words, 1, axis=0), jnp.uint32(0))
        o_ref[pl.ds(c, PACK_EB, stride=HALF_ROWS), :] = jnp.where(is_tail, tail, words)


def _pack_table(t, whole_word):
    e, d = t.shape
    assert d // 2 == HALF_ROWS * LANES and e % PACK_EB == 0
    nb = e // PACK_EB
    return pl.pallas_call(
        functools.partial(_pack_kernel, whole_word=whole_word),
        grid=(nb + 1,),
        in_specs=[pl.BlockSpec((PACK_EB, d), lambda i: (i % nb, 0))],
        out_specs=pl.BlockSpec((PACK_EB * HALF_ROWS, LANES), lambda i: (i, 0)),
        out_shape=jax.ShapeDtypeStruct(((e + PACK_EB) * HALF_ROWS, LANES), jnp.uint32),
        compiler_params=_cparams(1),
        name="pack_table",
    )(t)


def _load_rows(eidx, on_low_sublanes, n_experts):
    high_start = jnp.where(eidx == 0, HALF_ROWS * n_experts, HALF_ROWS * (eidx - 1))
    return jnp.where(on_low_sublanes, HALF_ROWS * eidx, high_start)


def _unpack(words, whole_word):
    first = words if whole_word else words & jnp.uint32(0xFFFF0000)
    return (pltpu.bitcast(first, F32), pltpu.bitcast(words << 16, F32))


_FOLD_POS = (6, 2, 4, 0, 7, 3, 5, 1)


def _fold_halves(p, sub):
    m = [jnp.where(sub < HALF_ROWS, p[2 * k], p[2 * k + 1]) for k in range(4)]
    n = [jnp.where((sub & 2) != 0, m[2 * k] + pltpu.roll(m[2 * k], 2, axis=0),
                   m[2 * k + 1] + pltpu.roll(m[2 * k + 1], 6, axis=0)) for k in range(2)]
    return jnp.where((sub & 1) != 0, n[0] + pltpu.roll(n[0], 1, axis=0), n[1] + pltpu.roll(n[1], 7, axis=0))


DOWN_TM = 256
DOWN_BLOCK = 16


def _token_tiles(h_ref, t0, sub):
    chunks = [h_ref[pl.ds(t0, SUBLANES), r * LANES:(r + 1) * LANES] for r in range(SUBLANES)]
    tiles = []
    for s in range(SUBLANES):
        tile = None
        for r in range(SUBLANES):
            piece = jnp.broadcast_to(chunks[r][s:s + 1, :], (SUBLANES, LANES))
            tile = piece if tile is None else jnp.where(sub == r, piece, tile)
        tiles.append(tile)
    return tiles


def _peer_down_kernel(*refs):
    row_refs = refs[:SUBLANES]
    h_ref, gate_ref, tbl_ref, o_ref = refs[SUBLANES:]
    tm = h_ref.shape[0]
    lane = lax.broadcasted_iota(jnp.int32, (SUBLANES, LANES), 1)
    sub = lax.broadcasted_iota(jnp.int32, (SUBLANES, LANES), 0)

    def group(gi, carry):
        t0 = pl.multiple_of(gi * SUBLANES, SUBLANES)
        hs = []
        for s, tile in enumerate(_token_tiles(h_ref, t0, sub)):
            swapped = pltpu.roll(tile, HALF_ROWS, axis=0)
            hs.append((tile, swapped) if _FOLD_POS[s] % 2 == 0 else (swapped, tile))
        gbase = gi * PEER_SEL

        def folded(j):
            prods = [None] * SUBLANES
            for s in range(SUBLANES):
                hi, lo = _unpack(tbl_ref[pl.ds(row_refs[s][gbase + j], SUBLANES), :], DOWN_WHOLE_WORD)
                prods[_FOLD_POS[s]] = hi * hs[s][0] + lo * hs[s][1]
            return _fold_halves(prods, sub)

        def place(acc, j0, folds):
            sums = [jnp.sum(f, axis=-1, keepdims=True) for f in folds]
            for k, tot in enumerate(sums):
                acc = jnp.where(lane == j0 + k, tot, acc)
            return acc

        def block(jb, carry2):
            acc, prev = carry2
            j0 = jb * DOWN_BLOCK
            new = tuple(folded(j0 + k) for k in range(DOWN_BLOCK))
            return place(acc, j0 - DOWN_BLOCK, prev), new

        first = tuple(folded(k) for k in range(DOWN_BLOCK))
        acc, last = lax.fori_loop(1, PEER_SEL // DOWN_BLOCK, block, (jnp.zeros((SUBLANES, LANES), F32), first))
        act = place(acc, PEER_SEL - DOWN_BLOCK, last)
        gelu = act * (lax.erf(act / np.sqrt(2).astype(np.float32)) + 1.0) / 2.0
        o_ref[pl.ds(t0, SUBLANES), :] = gate_ref[pl.ds(t0, SUBLANES), :] * gelu
        return carry

    lax.fori_loop(0, tm // SUBLANES, group, 0)


def _peer_down(eidx, h, gate, tbl):
    t, d = h.shape
    tm = DOWN_TM
    low = (jnp.asarray(_FOLD_POS, jnp.int32) % 2 == 0)[None, :, None]
    rows = _load_rows(eidx.reshape(t // SUBLANES, SUBLANES, PEER_SEL), low, tbl.shape[0] // HALF_ROWS - PACK_EB)
    rows = rows.transpose(1, 0, 2).reshape(SUBLANES, -1)
    smem = pl.BlockSpec((tm // SUBLANES * PEER_SEL,), lambda i: (i,), memory_space=pltpu.SMEM)
    return pl.pallas_call(
        _peer_down_kernel,
        grid=(t // tm,),
        in_specs=[smem] * SUBLANES + [
            pl.BlockSpec((tm, d), lambda i: (i, 0)),
            pl.BlockSpec((tm, PEER_SEL), lambda i: (i, 0)),
            pl.BlockSpec(tbl.shape, lambda i: (0, 0), pipeline_mode=pl.Buffered(1))],
        out_specs=pl.BlockSpec((tm, PEER_SEL), lambda i: (i, 0)),
        out_shape=jax.ShapeDtypeStruct((t, PEER_SEL), F32),
        compiler_params=_cparams(1),
        name="peer_down",
    )(*[rows[s] for s in range(SUBLANES)], h, gate, tbl)


UP_TM = 128
UP_STEP = 4
UP_CHUNK = 128


def _peer_up_kernel(*refs, final_norm):
    row_refs = refs[:UP_STEP]
    w_ref, eye_ref, x_ref, gate_ref, fg_ref, tbl_ref, o_ref, stage_scr, wv_scr = refs[UP_STEP:]
    tm, d = o_ref.shape
    steps = PEER_SEL // UP_STEP
    sub = lax.broadcasted_iota(jnp.int32, (SUBLANES, LANES), 0)
    low = sub < HALF_ROWS
    ones = jnp.ones((PEER_SEL, LANES), BF16)
    zero = jnp.zeros((SUBLANES, LANES), F32)

    def spread(t, slot):
        wrow = w_ref[pl.ds(t, 1), :]
        p0 = wrow.astype(BF16)
        r1 = wrow - p0.astype(F32)
        p1 = r1.astype(BF16)
        p2 = (r1 - p1.astype(F32)).astype(BF16)
        eye = eye_ref[...]
        wv_scr[slot] = (jnp.dot(eye * p0, ones, preferred_element_type=F32)
                        + jnp.dot(eye * p1, ones, preferred_element_type=F32)
                        + jnp.dot(eye * p2, ones, preferred_element_type=F32))

    def accumulate(t, slot):
        wv_rows = wv_scr.at[slot]
        tbase = t * steps
        accs = [zero] * UP_STEP
        for jj in range(steps):
            k = tbase + jj
            for u in range(0, UP_STEP, 2):
                j = jj * UP_STEP + u
                words = jnp.where(low, tbl_ref[pl.ds(row_refs[u][k], SUBLANES), :],
                                  tbl_ref[pl.ds(row_refs[u + 1][k], SUBLANES), :])
                wv = jnp.where(low, jnp.broadcast_to(wv_rows[j:j + 1, :], (SUBLANES, LANES)),
                               jnp.broadcast_to(wv_rows[j + 1:j + 2, :], (SUBLANES, LANES)))
                hi, lo = _unpack(words, UP_WHOLE_WORD)
                accs[u] = accs[u] + wv * hi
                accs[u + 1] = accs[u + 1] + wv * lo
        hi = _tree(jnp.add, accs[0::2])
        lo = _tree(jnp.add, accs[1::2])
        out = jnp.where(low, hi + pltpu.roll(hi, HALF_ROWS, axis=0), lo + pltpu.roll(lo, HALF_ROWS, axis=0))
        t_in_group = t % SUBLANES
        stage_scr[pl.ds(pl.multiple_of(t_in_group * SUBLANES, SUBLANES), SUBLANES), :] = out
        return t_in_group

    def token_pair(i, carry):
        t0 = 2 * i
        cur = 2 * (i % 2)
        accumulate(t0, cur)
        t_in_group = accumulate(t0 + 1, cur + 1)
        spread(jnp.minimum(t0 + 2, tm - 1), 2 - cur)
        spread(jnp.minimum(t0 + 3, tm - 1), 3 - cur)

        @pl.when(t_in_group == SUBLANES - 1)
        def _():
            g0 = pl.multiple_of(t0 + 1 - (SUBLANES - 1), SUBLANES)
            ys = []
            for r in range(SUBLANES):
                cols = slice(r * LANES, (r + 1) * LANES)
                ys.append(x_ref[pl.ds(g0, SUBLANES), cols]
                          + gate_ref[0, :, cols] * stage_scr[pl.ds(r, SUBLANES, stride=SUBLANES), :])
            if final_norm:
                ss = _tree(jnp.add, [jnp.sum(y * y, axis=-1, keepdims=True) for y in ys])
                scale = lax.rsqrt(ss / d + EPS)
                ys = [(y * scale) * fg_ref[:, r * LANES:(r + 1) * LANES] for r, y in enumerate(ys)]
            for r, y in enumerate(ys):
                o_ref[pl.ds(g0, SUBLANES), r * LANES:(r + 1) * LANES] = y

        return carry

    spread(0, 0)
    spread(1, 1)
    lax.fori_loop(0, tm // 2, token_pair, 0)


def _peer_up(eidx, w, tbl, x, gate, final_g):
    b, s, d = x.shape
    t = b * s
    tm = UP_TM
    assert s % tm == 0
    fg = jnp.ones((1, d), F32) if final_g is None else final_g.reshape(1, d)
    rows = _load_rows(eidx, (jnp.arange(PEER_SEL) % 2 == 0)[None, :], tbl.shape[0] // HALF_ROWS - PACK_EB)
    rows = rows.reshape(t, PEER_SEL // UP_STEP, UP_STEP).transpose(2, 0, 1).reshape(UP_STEP, -1)
    smem = pl.BlockSpec((tm * PEER_SEL // UP_STEP,), lambda i: (i,), memory_space=pltpu.SMEM)
    tiles_per_batch = s // tm
    out = pl.pallas_call(
        functools.partial(_peer_up_kernel, final_norm=final_g is not None),
        grid=(t // tm,),
        in_specs=[smem] * UP_STEP + [
            pl.BlockSpec((tm, PEER_SEL), lambda i: (i, 0)),
            pl.BlockSpec((PEER_SEL, PEER_SEL), lambda i: (0, 0)),
            pl.BlockSpec((tm, d), lambda i: (i, 0)),
            pl.BlockSpec((1, 1, d), lambda i: (i // tiles_per_batch, 0, 0)),
            pl.BlockSpec((1, d), lambda i: (0, 0)),
            pl.BlockSpec(tbl.shape, lambda i: (0, 0), pipeline_mode=pl.Buffered(1))],
        out_specs=pl.BlockSpec((tm, d), lambda i: (i, 0)),
        out_shape=jax.ShapeDtypeStruct((t, d), F32),
        scratch_shapes=[pltpu.VMEM((SUBLANES * SUBLANES, LANES), F32),
                        pltpu.VMEM((4, PEER_SEL, LANES), F32)],
        compiler_params=_cparams(1),
        name="peer_up",
    )(*[rows[u] for u in range(UP_STEP)], w, jnp.eye(PEER_SEL, dtype=BF16), x.reshape(t, d), gate, fg, tbl)
    return out.reshape(b, s, d)


def _peer_block(x, g, sh, sc, gate2, wq, keys, down, up, final_g):
    b, s, d = x.shape
    t = b * s
    h, eidx, gsm = _peer_score(x, g, sh, sc, wq.astype(BF16), keys.astype(BF16))
    eidx = eidx.reshape(t, PEER_SEL)
    w = _peer_down(eidx, h.reshape(t, d), gsm.reshape(t, PEER_SEL), _pack_table(down, DOWN_WHOLE_WORD))
    return _peer_up(eidx, w, _pack_table(up, UP_WHOLE_WORD), x, gate2, final_g)


def kernel(x, c, ctx, c_ctx, ada_w, ada_b, norm1_g, norm2_g, final_g, ab_w_in, ab_w_out, na_rpb, fn_w,
           cv_w_in, cv_w, cv_w_out, peer_w_q, peer_keys, peer_down, peer_up):
    b, s, d = x.shape
    depth = ada_w.shape[0]
    rows = -(-(b + 1) // SUBLANES) * SUBLANES
    cc = jnp.concatenate([c, c_ctx[None], jnp.zeros((rows - b - 1, d), F32)], axis=0)
    mod = _ada_vectors(cc, ada_w, ada_b)

    def chunks(i, lo, hi):
        m = mod[i, lo:hi].reshape(hi - lo, 1, 6, d)
        return [m[:, :, k] for k in range(6)]

    for i in range(depth):
        sh1, sc1, g1, sh2, sc2, g2 = chunks(i, 0, b)
        if i % 2 == 0:
            e = i // 2
            csh1, csc1 = chunks(i, b, b + 1)[:2]
            w_in = ab_w_in[e].astype(BF16)
            qkv, f = _norm_mod_matmul(x, norm1_g[i], sh1, sc1, w_in, (3 * NA_WIDTH, FN_WIDTH), (BF16, BF16),
                                      512, "ab_in")
            (ctx_kv,) = _norm_mod_matmul(ctx, norm1_g[i], csh1, csc1, w_in[:, NA_WIDTH:3 * NA_WIDTH],
                                         (2 * NA_WIDTH,), (BF16,), ctx.shape[1], "ab_in_ctx")
            a = _attention(qkv, ctx_kv, _attention_bias(na_rpb[e]))
            fm = _fnet(f, fn_w[e])
            x = _proj_residual(a, fm, x, g1, ab_w_out[e].astype(BF16), 512)
        else:
            o = i // 2
            bg, u = _conv_in(x, norm1_g[i], sh1, sc1, cv_w_in[o].astype(BF16), 256)
            x = _conv_out(bg, u, cv_w[o], x, g1, cv_w_out[o].astype(BF16), 512)
        x = _peer_block(x, norm2_g[i], sh2, sc2, g2, peer_w_q[i], peer_keys[i], peer_down[i], peer_up[i],
                        final_g if i == depth - 1 else None)
    return x
```

```python
import functools

import jax
import jax.numpy as jnp
import numpy as np
from jax import lax
from jax.experimental import pallas as pl
from jax.experimental.pallas import tpu as pltpu

F32 = jnp.float32
BF16 = jnp.bfloat16
EPS = 1e-6

GRID_W = 64
HEAD_DIM = 64
NA_HEADS = 8
NA_WIDTH = NA_HEADS * HEAD_DIM
KH = 8
KH_MAX = 8
KW = 16
FN_GROUPS = 8
FN_GROUP_DIM = 64
FN_WIDTH = FN_GROUPS * FN_GROUP_DIM
PEER_HEADS = 8
PEER_NKEYS = 128
PEER_DK_HALF = 128
PEER_TOPK = 16
PEER_SEL = PEER_HEADS * PEER_TOPK
LANES = 128
SUBLANES = 8
VMEM_LIMIT = 56 * 1024 * 1024

NEG_INF = float("-inf")


def _cparams(n_axes):
    return pltpu.CompilerParams(dimension_semantics=("arbitrary",) * n_axes, vmem_limit_bytes=VMEM_LIMIT)


def _norm_mod(x, g, sh, sc):
    y = x * lax.rsqrt(jnp.mean(x * x, axis=-1, keepdims=True) + EPS)
    return (y * g) * (1.0 + sc) + sh


def _ada_kernel(c_ref, w_ref, b_ref, o_ref):
    c = c_ref[...]
    s = c / (1.0 + jnp.exp(-c))
    o_ref[0] = jnp.dot(s, w_ref[0], preferred_element_type=F32, precision=lax.Precision.HIGHEST) + b_ref[0]


def _ada_vectors(cc, ada_w, ada_b):
    depth, d, n = ada_w.shape
    rows = cc.shape[0]
    tn = 1536
    return pl.pallas_call(
        _ada_kernel,
        grid=(depth, n // tn),
        in_specs=[pl.BlockSpec((rows, d), lambda i, j: (0, 0)),
                  pl.BlockSpec((1, d, tn), lambda i, j: (i, 0, j)),
                  pl.BlockSpec((1, 1, tn), lambda i, j: (i, 0, j))],
        out_specs=pl.BlockSpec((1, rows, tn), lambda i, j: (i, 0, j)),
        out_shape=jax.ShapeDtypeStruct((depth, rows, n), F32),
        compiler_params=_cparams(2),
        name="ada_vectors",
    )(cc, ada_w, ada_b.reshape(depth, 1, n))


def _nmm_kernel(x_ref, g_ref, sh_ref, sc_ref, w_ref, *o_refs):
    h = _norm_mod(x_ref[0], g_ref[...], sh_ref[0], sc_ref[0])
    acc = jnp.dot(h.astype(BF16), w_ref[...], preferred_element_type=F32)
    off = 0
    for o_ref in o_refs:
        n = o_ref.shape[-1]
        o_ref[0] = acc[:, off:off + n].astype(o_ref.dtype)
        off += n


def _mod_spec(arr):
    d = arr.shape[-1]
    if arr.shape[0] == 1:
        return pl.BlockSpec((1, 1, d), lambda b, i: (0, 0, 0))
    return pl.BlockSpec((1, 1, d), lambda b, i: (b, 0, 0))


def _norm_mod_matmul(x, g, sh, sc, w, splits, dtypes, tm, name):
    b, s, d = x.shape
    n = w.shape[1]
    assert sum(splits) == n
    return pl.pallas_call(
        _nmm_kernel,
        grid=(b, s // tm),
        in_specs=[pl.BlockSpec((1, tm, d), lambda bi, i: (bi, i, 0)),
                  pl.BlockSpec((1, d), lambda bi, i: (0, 0)),
                  _mod_spec(sh), _mod_spec(sc),
                  pl.BlockSpec((d, n), lambda bi, i: (0, 0))],
        out_specs=[pl.BlockSpec((1, tm, k), lambda bi, i: (bi, i, 0)) for k in splits],
        out_shape=[jax.ShapeDtypeStruct((b, s, k), dt) for k, dt in zip(splits, dtypes)],
        compiler_params=_cparams(2),
        name=name,
    )(x, g.reshape(1, d), sh, sc, w)


def _attn_kernel(qkv_ref, ctx_ref, bias_ref, o_ref):
    rows = qkv_ref.shape[1] // GRID_W
    nt = (((1,), (1,)), ((), ()))
    scale = HEAD_DIM ** -0.5

    def row_step(r, carry):
        start = jnp.clip(r - KH // 2, 0, rows - KH)
        off = start - r + (KH_MAX - 1)
        q0 = pl.multiple_of(r * GRID_W, GRID_W)
        k0 = pl.multiple_of(start * GRID_W, GRID_W)
        heads = range(NA_HEADS)
        cols = [slice(h * HEAD_DIM, (h + 1) * HEAD_DIM) for h in heads]
        kcols = [slice(NA_WIDTH + h * HEAD_DIM, NA_WIDTH + (h + 1) * HEAD_DIM) for h in heads]
        vcols = [slice(2 * NA_WIDTH + h * HEAD_DIM, 2 * NA_WIDTH + (h + 1) * HEAD_DIM) for h in heads]
        q = [qkv_ref[0, pl.ds(q0, GRID_W), cols[h]] for h in heads]
        s_win = [lax.dot_general(q[h], qkv_ref[0, pl.ds(k0, KH * GRID_W), kcols[h]], nt,
                                 preferred_element_type=F32) * scale + bias_ref[h, off] for h in heads]
        s_ctx = [lax.dot_general(q[h], ctx_ref[0, :, cols[h]], nt, preferred_element_type=F32) * scale
                 for h in heads]
        m = [jnp.maximum(jnp.max(s_win[h], axis=-1, keepdims=True), jnp.max(s_ctx[h], axis=-1, keepdims=True))
             for h in heads]
        p_win = [jnp.exp(s_win[h] - m[h]) for h in heads]
        p_ctx = [jnp.exp(s_ctx[h] - m[h]) for h in heads]
        l = [jnp.sum(p_win[h], axis=-1, keepdims=True) + jnp.sum(p_ctx[h], axis=-1, keepdims=True) for h in heads]
        o = [jnp.dot(p_win[h].astype(BF16), qkv_ref[0, pl.ds(k0, KH * GRID_W), vcols[h]], preferred_element_type=F32)
             + jnp.dot(p_ctx[h].astype(BF16), ctx_ref[0, :, kcols[h]], preferred_element_type=F32) for h in heads]
        outs = [o[h] / l[h] for h in heads]
        o_ref[0, pl.ds(q0, GRID_W), :] = jnp.concatenate(outs, axis=-1).astype(o_ref.dtype)
        return carry

    lax.fori_loop(0, rows, row_step, 0)


def _attention_bias(rpb):
    cols = np.arange(GRID_W)
    col_start = np.clip(cols - KW // 2, 0, GRID_W - KW)
    in_win = (cols[None, :] >= col_start[:, None]) & (cols[None, :] < col_start[:, None] + KW)
    col_off = np.clip(cols[None, :] - cols[:, None] + KW - 1, 0, 2 * KW - 2)
    onehot = jnp.asarray(col_off[None] == np.arange(2 * KW - 1)[:, None, None], F32)
    by_row = jnp.sum(rpb.astype(F32)[:, :, :, None, None] * onehot[None, None], axis=2)
    by_row = jnp.where(in_win[None, None], by_row, NEG_INF)
    b = jnp.stack([by_row[:, o:o + KH] for o in range(KH)], axis=1)
    return b.transpose(0, 1, 3, 2, 4).reshape(NA_HEADS, KH, GRID_W, KH * GRID_W)


def _attention(qkv, ctx_kv, bias):
    b, s, _ = qkv.shape
    l = ctx_kv.shape[1]
    return pl.pallas_call(
        _attn_kernel,
        grid=(b,),
        in_specs=[pl.BlockSpec((1, s, 3 * NA_WIDTH), lambda i: (i, 0, 0)),
                  pl.BlockSpec((1, l, 2 * NA_WIDTH), lambda i: (i, 0, 0)),
                  pl.BlockSpec(bias.shape, lambda i: (0, 0, 0, 0), pipeline_mode=pl.Buffered(1))],
        out_specs=pl.BlockSpec((1, s, NA_WIDTH), lambda i: (i, 0, 0)),
        out_shape=jax.ShapeDtypeStruct((b, s, NA_WIDTH), BF16),
        compiler_params=_cparams(1),
        name="nbr_attention",
    )(qkv, ctx_kv, bias)


FN_ROW_CHUNK = 256


def _fnet_kernel(f_ref, cn_ref, sn_ref, c64_ref, s64_ref, wbd_ref, o_ref, a_scr, b_scr, xa_scr, xb_scr):
    n = f_ref.shape[1]
    scale = (n * FN_GROUP_DIM) ** -0.5

    @pl.when(pl.program_id(0) == 0)
    def _():
        hi = lax.Precision.HIGHEST
        a_scr[...] = (jnp.dot(c64_ref[...], wbd_ref[...], preferred_element_type=F32, precision=hi) * scale).astype(BF16)
        b_scr[...] = (jnp.dot(s64_ref[...], wbd_ref[...], preferred_element_type=F32, precision=hi) * -scale).astype(BF16)

    x = f_ref[0]
    xa_scr[...] = jnp.dot(x, a_scr[...], preferred_element_type=F32).astype(BF16)
    xb_scr[...] = jnp.dot(x, b_scr[...], preferred_element_type=F32).astype(BF16)

    def chunk(i, carry):
        r0 = pl.multiple_of(i * FN_ROW_CHUNK, FN_ROW_CHUNK)
        y = (jnp.dot(cn_ref[pl.ds(r0, FN_ROW_CHUNK), :], xa_scr[...], preferred_element_type=F32)
             + jnp.dot(sn_ref[pl.ds(r0, FN_ROW_CHUNK), :], xb_scr[...], preferred_element_type=F32))
        o_ref[0, pl.ds(r0, FN_ROW_CHUNK), :] = y.astype(o_ref.dtype)
        return carry

    lax.fori_loop(0, n // FN_ROW_CHUNK, chunk, 0)


def _dft_tables(n):
    k = (np.arange(n)[:, None] * np.arange(n)[None, :]) % n
    ang = 2.0 * np.pi * k.astype(np.float64) / n
    return np.cos(ang), np.sin(ang)


def _fnet(f, fn_w):
    b, s, _ = f.shape
    cn, sn = _dft_tables(s)
    c64, s64 = _dft_tables(FN_GROUP_DIM)
    eye = np.eye(FN_GROUPS)
    c64bd = jnp.asarray(np.kron(eye, c64), F32)
    s64bd = jnp.asarray(np.kron(eye, s64), F32)
    wbd = (jnp.asarray(eye, F32)[:, None, :, None] * fn_w[:, :, None, :]).reshape(FN_WIDTH, FN_WIDTH)
    const = lambda shape: pl.BlockSpec(shape, lambda i: (0, 0), pipeline_mode=pl.Buffered(1))
    return pl.pallas_call(
        _fnet_kernel,
        grid=(b,),
        in_specs=[pl.BlockSpec((1, s, FN_WIDTH), lambda i: (i, 0, 0)),
                  const((s, s)), const((s, s)),
                  const((FN_WIDTH, FN_WIDTH)), const((FN_WIDTH, FN_WIDTH)), const((FN_WIDTH, FN_WIDTH))],
        out_specs=pl.BlockSpec((1, s, FN_WIDTH), lambda i: (i, 0, 0)),
        out_shape=jax.ShapeDtypeStruct((b, s, FN_WIDTH), BF16),
        scratch_shapes=[pltpu.VMEM((FN_WIDTH, FN_WIDTH), BF16), pltpu.VMEM((FN_WIDTH, FN_WIDTH), BF16),
                        pltpu.VMEM((s, FN_WIDTH), BF16), pltpu.VMEM((s, FN_WIDTH), BF16)],
        compiler_params=_cparams(1),
        name="fnet_mix",
    )(f, jnp.asarray(cn, BF16), jnp.asarray(sn, BF16), c64bd, s64bd, wbd)


def _proj_res_kernel(a_ref, f_ref, x_ref, g_ref, w_ref, o_ref):
    ka = a_ref.shape[-1]
    out = (jnp.dot(a_ref[0], w_ref[:ka, :], preferred_element_type=F32)
           + jnp.dot(f_ref[0], w_ref[ka:, :], preferred_element_type=F32))
    o_ref[0] = x_ref[0] + g_ref[0] * out


def _proj_residual(a, f, x, gate, w, tm):
    b, s, d = x.shape
    ka, kf = a.shape[-1], f.shape[-1]
    return pl.pallas_call(
        _proj_res_kernel,
        grid=(b, s // tm),
        in_specs=[pl.BlockSpec((1, tm, ka), lambda bi, i: (bi, i, 0)),
                  pl.BlockSpec((1, tm, kf), lambda bi, i: (bi, i, 0)),
                  pl.BlockSpec((1, tm, d), lambda bi, i: (bi, i, 0)),
                  _mod_spec(gate),
                  pl.BlockSpec((ka + kf, d), lambda bi, i: (0, 0))],
        out_specs=pl.BlockSpec((1, tm, d), lambda bi, i: (bi, i, 0)),
        out_shape=jax.ShapeDtypeStruct((b, s, d), F32),
        compiler_params=_cparams(2),
        name="proj_residual",
    )(a, f, x, gate, w)


def _conv_in_kernel(x_ref, g_ref, sh_ref, sc_ref, w_ref, bg_ref, u_ref):
    d = bg_ref.shape[-1]
    h = _norm_mod(x_ref[0], g_ref[...], sh_ref[0], sc_ref[0])
    acc = jnp.dot(h.astype(BF16), w_ref[...], preferred_element_type=F32)
    bg_ref[0] = acc[:, :d]
    u_ref[0] = acc[:, d:2 * d] * acc[:, 2 * d:]


def _conv_in(x, g, sh, sc, w, tm):
    b, s, d = x.shape
    n = w.shape[1]
    return pl.pallas_call(
        _conv_in_kernel,
        grid=(b, s // tm),
        in_specs=[pl.BlockSpec((1, tm, d), lambda bi, i: (bi, i, 0)),
                  pl.BlockSpec((1, d), lambda bi, i: (0, 0)),
                  _mod_spec(sh), _mod_spec(sc),
                  pl.BlockSpec((d, n), lambda bi, i: (0, 0))],
        out_specs=[pl.BlockSpec((1, tm, d), lambda bi, i: (bi, i, 0))] * 2,
        out_shape=[jax.ShapeDtypeStruct((b, s, d), F32)] * 2,
        compiler_params=_cparams(2),
        name="conv_in",
    )(x, g.reshape(1, d), sh, sc, w)


def _conv_out_kernel(bg_ref, u_ref, up_ref, un_ref, cw_ref, x_ref, g_ref, w_ref, o_ref):
    i = pl.program_id(1)
    last = pl.num_programs(1) - 1
    u = u_ref[0]
    tm = u.shape[0]
    row = lax.broadcasted_iota(jnp.int32, u.shape, 0)
    prev_row = jnp.where(i == 0, 0.0, up_ref[0, SUBLANES - 1:SUBLANES, :])
    next_row = jnp.where(i == last, 0.0, un_ref[0, 0:1, :])
    u_prev = jnp.where(row == 0, prev_row, pltpu.roll(u, 1, axis=0))
    u_next = jnp.where(row == tm - 1, next_row, pltpu.roll(u, tm - 1, axis=0))
    y = cw_ref[0:1, :] * u_prev + cw_ref[1:2, :] * u + cw_ref[2:3, :] * u_next
    z = (bg_ref[0] * y).astype(BF16)
    o_ref[0] = x_ref[0] + g_ref[0] * jnp.dot(z, w_ref[...], preferred_element_type=F32)


def _conv_out(bg, u, cw, x, gate, w, tm):
    b, s, d = x.shape
    hb = tm // SUBLANES
    nhb = s // SUBLANES
    tile = pl.BlockSpec((1, tm, d), lambda bi, i: (bi, i, 0))
    return pl.pallas_call(
        _conv_out_kernel,
        grid=(b, s // tm),
        in_specs=[tile, tile,
                  pl.BlockSpec((1, SUBLANES, d), lambda bi, i: (bi, jnp.maximum(i * hb - 1, 0), 0)),
                  pl.BlockSpec((1, SUBLANES, d), lambda bi, i: (bi, jnp.minimum((i + 1) * hb, nhb - 1), 0)),
                  pl.BlockSpec(cw.shape, lambda bi, i: (0, 0)),
                  tile, _mod_spec(gate),
                  pl.BlockSpec((d, d), lambda bi, i: (0, 0))],
        out_specs=tile,
        out_shape=jax.ShapeDtypeStruct((b, s, d), F32),
        compiler_params=_cparams(2),
        name="conv_out",
    )(bg, u, u, u, cw, x, gate, w)


PEER_TM = 256
KEY_TILES = PEER_NKEYS // SUBLANES
assert PEER_TOPK == 2 * SUBLANES


def _tree(op, xs):
    xs = list(xs)
    while len(xs) > 1:
        xs = [op(xs[i], xs[i + 1]) for i in range(0, len(xs) - 1, 2)] + ([xs[-1]] if len(xs) % 2 else [])
    return xs[0]


def _all_sublanes(op, x):
    for shift in (4, 2, 1):
        x = op(x, pltpu.roll(x, shift, axis=0))
    return x


def _top_keys(problems, sub):
    key_id = [sub + SUBLANES * v for v in range(KEY_TILES)]
    problems = [list(tiles) for tiles in problems]
    out = [[] for _ in problems]
    for _ in range(PEER_TOPK):
        for i, tiles in enumerate(problems):
            m = _all_sublanes(jnp.maximum, _tree(jnp.maximum, tiles))
            idx = _all_sublanes(jnp.minimum, _tree(jnp.minimum, [jnp.where(t == m, k, PEER_NKEYS)
                                                                 for t, k in zip(tiles, key_id)]))
            problems[i] = [jnp.where(k == idx, NEG_INF, t) for t, k in zip(tiles, key_id)]
            out[i].append((m, idx))
    return out


def _rows_of(ranked, sub, which, pick):
    t = ranked[pick(0)][which]
    for r in range(1, SUBLANES):
        t = jnp.where(sub == r, ranked[pick(r)][which], t)
    return t


def _product_key_topk(pairs, sub):
    def tiles(first, second, which):
        lo = _rows_of(second, sub, which, lambda r: r)
        hi = _rows_of(second, sub, which, lambda r: SUBLANES + r)
        quad = _rows_of(second, sub, which, lambda r: r % 4)
        a45 = _rows_of(first, sub, which, lambda r: 4 + r // 4)
        a67 = _rows_of(first, sub, which, lambda r: 6 + r // 4)
        ahi = _rows_of(first, sub, which, lambda r: SUBLANES + r)
        a = [first[k][which] for k in range(4)]
        return (a[0], lo), (a[0], hi), (a[1], lo), (a[2], lo), (a[3], lo), (a45, quad), (a67, quad), (ahi, second[0][which])

    cands = [[x + y for x, y in tiles(f, s, 0)] for f, s in pairs]
    cidxs = [[x * PEER_NKEYS + y for x, y in tiles(f, s, 1)] for f, s in pairs]
    quad_pos = jnp.where(sub < 4, sub, sub + (PEER_TOPK - 4))
    pos = [sub, sub + 8, sub + 16, sub + 32, sub + 48, quad_pos + 64, quad_pos + 96, (sub + 8) * PEER_TOPK]
    out = [[] for _ in pairs]
    for _ in range(PEER_TOPK):
        for i, (cand, cidx) in enumerate(zip(cands, cidxs)):
            m = _all_sublanes(jnp.maximum, _tree(jnp.maximum, cand))
            psel = _all_sublanes(jnp.minimum, _tree(jnp.minimum, [jnp.where(c == m, p, PEER_TOPK * PEER_TOPK)
                                                                  for c, p in zip(cand, pos)]))
            hits = [p == psel for p in pos]
            e = _all_sublanes(jnp.maximum, _tree(jnp.maximum, [jnp.where(hh, x, -1) for hh, x in zip(hits, cidx)]))
            cands[i] = [jnp.where(hh, NEG_INF, c) for hh, c in zip(hits, cand)]
            out[i].append((m, e))
    return out


def _peer_score_kernel(x_ref, g_ref, sh_ref, sc_ref, wq_ref, keys_ref, h_ref, eidx_ref, gate_ref,
                       q_scr, s_scr, e_scr, p_scr):
    tm = x_ref.shape[1]
    nt = (((1,), (1,)), ((), ()))
    h = _norm_mod(x_ref[0], g_ref[...], sh_ref[0], sc_ref[0])
    h_ref[0] = h
    q = jnp.dot(h.astype(BF16), wq_ref[...], preferred_element_type=F32)
    for c in range(2 * PEER_HEADS):
        q_scr[c] = q[:, c * PEER_DK_HALF:(c + 1) * PEER_DK_HALF].astype(BF16)
    sub = lax.broadcasted_iota(jnp.int32, (SUBLANES, LANES), 0)

    def head_step(hd, carry):
        for p in range(2):
            s_scr[p] = lax.dot_general(keys_ref[hd, p], q_scr[2 * hd + p], nt, preferred_element_type=F32)
        row0 = pl.multiple_of(hd * PEER_TOPK, PEER_TOPK)
        groups = tm // LANES
        cols = [slice(lg * LANES, (lg + 1) * LANES) for lg in range(groups)]
        ranked = _top_keys([[s_scr[p, v * SUBLANES:(v + 1) * SUBLANES, cols[lg]] for v in range(KEY_TILES)]
                            for lg in range(groups) for p in range(2)], sub)
        tops = _product_key_topk([(ranked[2 * lg], ranked[2 * lg + 1]) for lg in range(groups)], sub)
        for lg, top in enumerate(tops):
            ex = [(jnp.exp(m - top[0][0]), e) for m, e in top]
            denom = _tree(jnp.add, [v for v, _ in ex])
            for half in range(2):
                pick = lambda r, half=half: half * SUBLANES + r
                rows = pl.ds(row0 + half * SUBLANES, SUBLANES)
                e_scr[rows, cols[lg]] = _rows_of(ex, sub, 1, pick)
                p_scr[rows, cols[lg]] = _rows_of(ex, sub, 0, pick) / denom
        return carry

    lax.fori_loop(0, PEER_HEADS, head_step, 0)
    eidx_ref[0] = e_scr[...].T
    gate_ref[0] = p_scr[...].T


def _peer_score(x, g, sh, sc, wq, keys):
    b, s, d = x.shape
    tm = PEER_TM
    n = wq.shape[1]
    tile = lambda k: pl.BlockSpec((1, tm, k), lambda bi, i: (bi, i, 0))
    return pl.pallas_call(
        _peer_score_kernel,
        grid=(b, s // tm),
        in_specs=[tile(d),
                  pl.BlockSpec((1, d), lambda bi, i: (0, 0)),
                  _mod_spec(sh), _mod_spec(sc),
                  pl.BlockSpec((d, n), lambda bi, i: (0, 0)),
                  pl.BlockSpec(keys.shape, lambda bi, i: (0, 0, 0, 0))],
        out_specs=[tile(d), tile(PEER_SEL), tile(PEER_SEL)],
        out_shape=[jax.ShapeDtypeStruct((b, s, d), F32),
                   jax.ShapeDtypeStruct((b, s, PEER_SEL), jnp.int32),
                   jax.ShapeDtypeStruct((b, s, PEER_SEL), F32)],
        scratch_shapes=[pltpu.VMEM((2 * PEER_HEADS, tm, PEER_DK_HALF), BF16),
                        pltpu.VMEM((2, PEER_NKEYS, tm), F32),
                        pltpu.VMEM((PEER_SEL, tm), jnp.int32),
                        pltpu.VMEM((PEER_SEL, tm), F32)],
        compiler_params=_cparams(2),
        name="peer_score",
    )(x, g.reshape(1, d), sh, sc, wq, keys)


HALF_ROWS = SUBLANES // 2


PACK_EB = 256
DOWN_WHOLE_WORD = True
UP_WHOLE_WORD = False


def _pack_kernel(x_ref, o_ref, *, whole_word):
    x_ref = x_ref.at[0]
    half = x_ref.shape[1] // 2
    is_tail = pl.program_id(0) == pl.num_programs(0) - 1
    row = lax.broadcasted_iota(jnp.int32, (PACK_EB, LANES), 0)
    for c in range(HALF_ROWS):
        lo = x_ref[:, half + c * LANES:half + (c + 1) * LANES].astype(BF16).astype(F32)
        lo_bits = pltpu.bitcast(lo, jnp.int32) >> 16 & 0xFFFF
        hi = x_ref[:, c * LANES:(c + 1) * LANES]
        if whole_word:
            bits = pltpu.bitcast(hi, jnp.int32)
            excess = lo_bits - (bits & 0xFFFF)
            step = jnp.where(excess > 0x8000, -1, jnp.where(excess < -0x8000, 1, 0))
            step = jnp.where((bits & 0x7FFF0000) == 0, jnp.maximum(step, 0), step)
            hi_bits = (bits & jnp.int32(-0x10000)) + (step << 16)
        else:
            hi_bits = pltpu.bitcast(hi.astype(BF16).astype(F32), jnp.int32)
        words = pltpu.bitcast(hi_bits | lo_bits, jnp.uint32)
        tail = jnp.where(row == 1, pltpu.roll(words, 1, axis=0), jnp.uint32(0))
        o_ref[pl.ds(c, PACK_EB, stride=HALF_ROWS), :] = jnp.where(is_tail, tail, words)


def _pack_table(tables, layer, whole_word):
    _, e, d = tables.shape
    assert d // 2 == HALF_ROWS * LANES and e % PACK_EB == 0
    nb = e // PACK_EB
    return pl.pallas_call(
        functools.partial(_pack_kernel, whole_word=whole_word),
        grid=(nb + 1,),
        in_specs=[pl.BlockSpec((1, PACK_EB, d), lambda i: (layer, i % nb, 0))],
        out_specs=pl.BlockSpec((PACK_EB * HALF_ROWS, LANES), lambda i: (i, 0)),
        out_shape=jax.ShapeDtypeStruct(((e + PACK_EB) * HALF_ROWS, LANES), jnp.uint32),
        compiler_params=_cparams(1),
        name="pack_table",
    )(tables)


def _load_rows(eidx, on_low_sublanes, n_experts):
    high_start = jnp.where(eidx == 0, HALF_ROWS * n_experts, HALF_ROWS * (eidx - 1))
    return jnp.where(on_low_sublanes, HALF_ROWS * eidx, high_start)


def _unpack(words, whole_word):
    first = words if whole_word else words & jnp.uint32(0xFFFF0000)
    return (pltpu.bitcast(first, F32), pltpu.bitcast(words << 16, F32))


_FOLD_POS = (6, 2, 4, 0, 7, 3, 5, 1)


def _fold_halves(p, sub):
    m = [jnp.where(sub < HALF_ROWS, p[2 * k], p[2 * k + 1]) for k in range(4)]
    n = [jnp.where((sub & 2) != 0, m[2 * k] + pltpu.roll(m[2 * k], 2, axis=0),
                   m[2 * k + 1] + pltpu.roll(m[2 * k + 1], 6, axis=0)) for k in range(2)]
    return jnp.where((sub & 1) != 0, n[0] + pltpu.roll(n[0], 1, axis=0), n[1] + pltpu.roll(n[1], 7, axis=0))


DOWN_TM = 256
DOWN_BLOCK = 16


def _token_tiles(h_ref, t0, sub):
    chunks = [h_ref[pl.ds(t0, SUBLANES), r * LANES:(r + 1) * LANES] for r in range(SUBLANES)]
    tiles = []
    for s in range(SUBLANES):
        tile = None
        for r in range(SUBLANES):
            piece = jnp.broadcast_to(chunks[r][s:s + 1, :], (SUBLANES, LANES))
            tile = piece if tile is None else jnp.where(sub == r, piece, tile)
        tiles.append(tile)
    return tiles


def _peer_down_kernel(*refs):
    row_refs = refs[:SUBLANES]
    h_ref, gate_ref, tbl_ref, o_ref = refs[SUBLANES:]
    tm = h_ref.shape[0]
    lane = lax.broadcasted_iota(jnp.int32, (SUBLANES, LANES), 1)
    sub = lax.broadcasted_iota(jnp.int32, (SUBLANES, LANES), 0)

    def group(gi, carry):
        t0 = pl.multiple_of(gi * SUBLANES, SUBLANES)
        hs = []
        for s, tile in enumerate(_token_tiles(h_ref, t0, sub)):
            swapped = pltpu.roll(tile, HALF_ROWS, axis=0)
            hs.append((tile, swapped) if _FOLD_POS[s] % 2 == 0 else (swapped, tile))
        gbase = gi * PEER_SEL

        def folded(j):
            prods = [None] * SUBLANES
            for s in range(SUBLANES):
                hi, lo = _unpack(tbl_ref[pl.ds(row_refs[s][gbase + j], SUBLANES), :], DOWN_WHOLE_WORD)
                prods[_FOLD_POS[s]] = hi * hs[s][0] + lo * hs[s][1]
            return _fold_halves(prods, sub)

        def place(acc, j0, folds):
            sums = [jnp.sum(f, axis=-1, keepdims=True) for f in folds]
            for k, tot in enumerate(sums):
                acc = jnp.where(lane == j0 + k, tot, acc)
            return acc

        def block(jb, carry2):
            acc, prev = carry2
            j0 = jb * DOWN_BLOCK
            new = tuple(folded(j0 + k) for k in range(DOWN_BLOCK))
            return place(acc, j0 - DOWN_BLOCK, prev), new

        first = tuple(folded(k) for k in range(DOWN_BLOCK))
        acc, last = lax.fori_loop(1, PEER_SEL // DOWN_BLOCK, block, (jnp.zeros((SUBLANES, LANES), F32), first))
        act = place(acc, PEER_SEL - DOWN_BLOCK, last)
        gelu = act * (lax.erf(act / np.sqrt(2).astype(np.float32)) + 1.0) / 2.0
        o_ref[pl.ds(t0, SUBLANES), :] = gate_ref[pl.ds(t0, SUBLANES), :] * gelu
        return carry

    lax.fori_loop(0, tm // SUBLANES, group, 0)


def _peer_down(eidx, h, gate, tbl):
    t, d = h.shape
    tm = DOWN_TM
    low = (jnp.asarray(_FOLD_POS, jnp.int32) % 2 == 0)[None, :, None]
    rows = _load_rows(eidx.reshape(t // SUBLANES, SUBLANES, PEER_SEL), low, tbl.shape[0] // HALF_ROWS - PACK_EB)
    rows = rows.transpose(1, 0, 2).reshape(SUBLANES, -1)
    smem = pl.BlockSpec((tm // SUBLANES * PEER_SEL,), lambda i: (i,), memory_space=pltpu.SMEM)
    return pl.pallas_call(
        _peer_down_kernel,
        grid=(t // tm,),
        in_specs=[smem] * SUBLANES + [
            pl.BlockSpec((tm, d), lambda i: (i, 0)),
            pl.BlockSpec((tm, PEER_SEL), lambda i: (i, 0)),
            pl.BlockSpec(tbl.shape, lambda i: (0, 0), pipeline_mode=pl.Buffered(1))],
        out_specs=pl.BlockSpec((tm, PEER_SEL), lambda i: (i, 0)),
        out_shape=jax.ShapeDtypeStruct((t, PEER_SEL), F32),
        compiler_params=_cparams(1),
        name="peer_down",
    )(*[rows[s] for s in range(SUBLANES)], h, gate, tbl)


UP_TM = 128
UP_BLOCK = 64
UP_SPREAD = SUBLANES * UP_BLOCK // PEER_SEL


def _peer_up_kernel(*refs):
    row_refs = refs[:SUBLANES]
    w_ref, eye_ref, x_ref, gate_ref, tbl_ref, o_ref, stage_scr, wv_even, wv_odd = refs[SUBLANES:]
    wv_scr = (wv_even, wv_odd)
    tm, d = o_ref.shape
    n_groups = tm // SUBLANES
    sub = lax.broadcasted_iota(jnp.int32, (SUBLANES, LANES), 0)
    low = sub < HALF_ROWS
    ones = jnp.ones((PEER_SEL, LANES), BF16)

    def spread(t, slot, s):
        wrow = w_ref[pl.ds(t, 1), :]
        p0 = wrow.astype(BF16)
        r1 = wrow - p0.astype(F32)
        p1 = r1.astype(BF16)
        p2 = (r1 - p1.astype(F32)).astype(BF16)
        eye = eye_ref[...]
        wv_scr[slot][s] = (jnp.dot(eye * p0, ones, preferred_element_type=F32)
                           + jnp.dot(eye * p1, ones, preferred_element_type=F32)
                           + jnp.dot(eye * p2, ones, preferred_element_type=F32))

    def group(g, cur):
        t0 = pl.multiple_of(g * SUBLANES, SUBLANES)
        next_t0 = jnp.minimum(g + 1, n_groups - 1) * SUBLANES
        gbase = g * PEER_SEL

        def block(jb, accs):
            accs = list(accs)
            j0 = jb * UP_BLOCK
            for u in range(UP_SPREAD):
                s_next = jb * UP_SPREAD + u
                spread(next_t0 + s_next, 1 - cur, s_next)
            wv_rows = [wv_scr[cur].at[s, pl.ds(pl.multiple_of(j0, UP_BLOCK), UP_BLOCK)] for s in range(SUBLANES)]
            for k in range(0, UP_BLOCK, 2):
                for s in range(SUBLANES):
                    words = jnp.where(low, tbl_ref[pl.ds(row_refs[s][gbase + j0 + k], SUBLANES), :],
                                      tbl_ref[pl.ds(row_refs[s][gbase + j0 + k + 1], SUBLANES), :])
                    wv = jnp.where(low, jnp.broadcast_to(wv_rows[s][k:k + 1, :], (SUBLANES, LANES)),
                                   jnp.broadcast_to(wv_rows[s][k + 1:k + 2, :], (SUBLANES, LANES)))
                    hi, lo = _unpack(words, UP_WHOLE_WORD)
                    accs[2 * s] = accs[2 * s] + wv * hi
                    accs[2 * s + 1] = accs[2 * s + 1] + wv * lo
            return tuple(accs)

        zero = jnp.zeros((SUBLANES, LANES), F32)
        accs = lax.fori_loop(0, PEER_SEL // UP_BLOCK, block, (zero,) * (2 * SUBLANES))
        for s in range(SUBLANES):
            hi, lo = accs[2 * s], accs[2 * s + 1]
            stage_scr[s * SUBLANES:(s + 1) * SUBLANES, :] = jnp.where(
                low, hi + pltpu.roll(hi, HALF_ROWS, axis=0), lo + pltpu.roll(lo, HALF_ROWS, axis=0))
        for r in range(SUBLANES):
            cols = slice(r * LANES, (r + 1) * LANES)
            o_ref[pl.ds(t0, SUBLANES), cols] = (x_ref[pl.ds(t0, SUBLANES), cols]
                                                + gate_ref[0, :, cols] * stage_scr[pl.ds(r, SUBLANES, stride=SUBLANES), :])

    def group_pair(gp, carry):
        group(2 * gp, 0)
        group(2 * gp + 1, 1)
        return carry

    for s in range(SUBLANES):
        spread(s, 0, s)
    lax.fori_loop(0, n_groups // 2, group_pair, 0)


def _peer_up(eidx, w, tbl, x, gate):
    b, s, d = x.shape
    t = b * s
    tm = UP_TM
    assert s % tm == 0
    rows = _load_rows(eidx, (jnp.arange(PEER_SEL) % 2 == 0)[None, :], tbl.shape[0] // HALF_ROWS - PACK_EB)
    rows = rows.reshape(t // SUBLANES, SUBLANES, PEER_SEL).transpose(1, 0, 2).reshape(SUBLANES, -1)
    smem = pl.BlockSpec((tm // SUBLANES * PEER_SEL,), lambda i: (i,), memory_space=pltpu.SMEM)
    tiles_per_batch = s // tm
    out = pl.pallas_call(
        _peer_up_kernel,
        grid=(t // tm,),
        in_specs=[smem] * SUBLANES + [
            pl.BlockSpec((tm, PEER_SEL), lambda i: (i, 0)),
            pl.BlockSpec((PEER_SEL, PEER_SEL), lambda i: (0, 0)),
            pl.BlockSpec((tm, d), lambda i: (i, 0)),
            pl.BlockSpec((1, 1, d), lambda i: (i // tiles_per_batch, 0, 0)),
            pl.BlockSpec(tbl.shape, lambda i: (0, 0), pipeline_mode=pl.Buffered(1))],
        out_specs=pl.BlockSpec((tm, d), lambda i: (i, 0)),
        out_shape=jax.ShapeDtypeStruct((t, d), F32),
        scratch_shapes=[pltpu.VMEM((SUBLANES * SUBLANES, LANES), F32),
                        pltpu.VMEM((SUBLANES, PEER_SEL, LANES), F32), pltpu.VMEM((SUBLANES, PEER_SEL, LANES), F32)],
        compiler_params=_cparams(1),
        name="peer_up",
    )(*[rows[u] for u in range(SUBLANES)], w, jnp.eye(PEER_SEL, dtype=BF16), x.reshape(t, d), gate, tbl)
    return out.reshape(b, s, d)


def _peer_block(x, g, sh, sc, gate2, wq, keys, down, up, layer):
    b, s, d = x.shape
    t = b * s
    h, eidx, gsm = _peer_score(x, g, sh, sc, wq.astype(BF16), keys.astype(BF16))
    eidx = eidx.reshape(t, PEER_SEL)
    w = _peer_down(eidx, h.reshape(t, d), gsm.reshape(t, PEER_SEL), _pack_table(down, layer, DOWN_WHOLE_WORD))
    return _peer_up(eidx, w, _pack_table(up, layer, UP_WHOLE_WORD), x, gate2)


def _rmsnorm_kernel(x_ref, g_ref, o_ref):
    x = x_ref[0]
    o_ref[0] = (x * lax.rsqrt(jnp.mean(x * x, axis=-1, keepdims=True) + EPS)) * g_ref[...]


def _rmsnorm(x, g, tm):
    b, s, d = x.shape
    tile = pl.BlockSpec((1, tm, d), lambda bi, i: (bi, i, 0))
    return pl.pallas_call(
        _rmsnorm_kernel,
        grid=(b, s // tm),
        in_specs=[tile, pl.BlockSpec((1, d), lambda bi, i: (0, 0))],
        out_specs=tile,
        out_shape=jax.ShapeDtypeStruct((b, s, d), F32),
        compiler_params=_cparams(2),
        name="final_rmsnorm",
    )(x, g.reshape(1, d))


def kernel(x, c, ctx, c_ctx, ada_w, ada_b, norm1_g, norm2_g, final_g, ab_w_in, ab_w_out, na_rpb, fn_w,
           cv_w_in, cv_w, cv_w_out, peer_w_q, peer_keys, peer_down, peer_up):
    b, s, d = x.shape
    depth = ada_w.shape[0]
    rows = -(-(b + 1) // SUBLANES) * SUBLANES
    cc = jnp.concatenate([c, c_ctx[None], jnp.zeros((rows - b - 1, d), F32)], axis=0)
    mod = _ada_vectors(cc, ada_w, ada_b)

    def chunks(i, lo, hi):
        m = mod[i, lo:hi].reshape(hi - lo, 1, 6, d)
        return [m[:, :, k] for k in range(6)]

    for i in range(depth):
        sh1, sc1, g1, sh2, sc2, g2 = chunks(i, 0, b)
        if i % 2 == 0:
            e = i // 2
            csh1, csc1 = chunks(i, b, b + 1)[:2]
            w_in = ab_w_in[e].astype(BF16)
            qkv, f = _norm_mod_matmul(x, norm1_g[i], sh1, sc1, w_in, (3 * NA_WIDTH, FN_WIDTH), (BF16, BF16),
                                      512, "ab_in")
            (ctx_kv,) = _norm_mod_matmul(ctx, norm1_g[i], csh1, csc1, w_in[:, NA_WIDTH:3 * NA_WIDTH],
                                         (2 * NA_WIDTH,), (BF16,), ctx.shape[1], "ab_in_ctx")
            a = _attention(qkv, ctx_kv, _attention_bias(na_rpb[e]))
            fm = _fnet(f, fn_w[e])
            x = _proj_residual(a, fm, x, g1, ab_w_out[e].astype(BF16), 512)
        else:
            o = i // 2
            bg, u = _conv_in(x, norm1_g[i], sh1, sc1, cv_w_in[o].astype(BF16), 256)
            x = _conv_out(bg, u, cv_w[o], x, g1, cv_w_out[o].astype(BF16), 512)
        x = _peer_block(x, norm2_g[i], sh2, sc2, g2, peer_w_q[i], peer_keys[i], peer_down, peer_up, i)
    return _rmsnorm(x, final_g, 512)
```

```python
import functools

import jax
import jax.numpy as jnp
import numpy as np
from jax import lax
from jax.experimental import pallas as pl
from jax.experimental.pallas import tpu as pltpu

F32 = jnp.float32
BF16 = jnp.bfloat16
EPS = 1e-6

GRID_W = 64
HEAD_DIM = 64
NA_HEADS = 8
NA_WIDTH = NA_HEADS * HEAD_DIM
KH = 8
KH_MAX = 8
KW = 16
FN_GROUPS = 8
FN_GROUP_DIM = 64
FN_WIDTH = FN_GROUPS * FN_GROUP_DIM
PEER_HEADS = 8
PEER_NKEYS = 128
PEER_DK_HALF = 128
PEER_TOPK = 16
PEER_SEL = PEER_HEADS * PEER_TOPK
LANES = 128
SUBLANES = 8
VMEM_LIMIT = 56 * 1024 * 1024

NEG_INF = float("-inf")


def _cparams(n_axes):
    return pltpu.CompilerParams(dimension_semantics=("arbitrary",) * n_axes, vmem_limit_bytes=VMEM_LIMIT)


def _norm_mod(x, g, sh, sc):
    y = x * lax.rsqrt(jnp.mean(x * x, axis=-1, keepdims=True) + EPS)
    return (y * g) * (1.0 + sc) + sh


def _ada_kernel(c_ref, w_ref, b_ref, o_ref):
    c = c_ref[...]
    s = c / (1.0 + jnp.exp(-c))
    o_ref[0] = jnp.dot(s, w_ref[0], preferred_element_type=F32, precision=lax.Precision.HIGHEST) + b_ref[0]


def _ada_vectors(cc, ada_w, ada_b):
    depth, d, n = ada_w.shape
    rows = cc.shape[0]
    tn = 1536
    return pl.pallas_call(
        _ada_kernel,
        grid=(depth, n // tn),
        in_specs=[pl.BlockSpec((rows, d), lambda i, j: (0, 0)),
                  pl.BlockSpec((1, d, tn), lambda i, j: (i, 0, j)),
                  pl.BlockSpec((1, 1, tn), lambda i, j: (i, 0, j))],
        out_specs=pl.BlockSpec((1, rows, tn), lambda i, j: (i, 0, j)),
        out_shape=jax.ShapeDtypeStruct((depth, rows, n), F32),
        compiler_params=_cparams(2),
        name="ada_vectors",
    )(cc, ada_w, ada_b.reshape(depth, 1, n))


def _nmm_kernel(x_ref, g_ref, sh_ref, sc_ref, w_ref, *o_refs):
    h = _norm_mod(x_ref[0], g_ref[...], sh_ref[0], sc_ref[0])
    acc = jnp.dot(h.astype(BF16), w_ref[...], preferred_element_type=F32)
    off = 0
    for o_ref in o_refs:
        n = o_ref.shape[-1]
        o_ref[0] = acc[:, off:off + n].astype(o_ref.dtype)
        off += n


def _mod_spec(arr):
    d = arr.shape[-1]
    if arr.shape[0] == 1:
        return pl.BlockSpec((1, 1, d), lambda b, i: (0, 0, 0))
    return pl.BlockSpec((1, 1, d), lambda b, i: (b, 0, 0))


def _norm_mod_matmul(x, g, sh, sc, w, splits, dtypes, tm, name):
    b, s, d = x.shape
    n = w.shape[1]
    assert sum(splits) == n
    return pl.pallas_call(
        _nmm_kernel,
        grid=(b, s // tm),
        in_specs=[pl.BlockSpec((1, tm, d), lambda bi, i: (bi, i, 0)),
                  pl.BlockSpec((1, d), lambda bi, i: (0, 0)),
                  _mod_spec(sh), _mod_spec(sc),
                  pl.BlockSpec((d, n), lambda bi, i: (0, 0))],
        out_specs=[pl.BlockSpec((1, tm, k), lambda bi, i: (bi, i, 0)) for k in splits],
        out_shape=[jax.ShapeDtypeStruct((b, s, k), dt) for k, dt in zip(splits, dtypes)],
        compiler_params=_cparams(2),
        name=name,
    )(x, g.reshape(1, d), sh, sc, w)


def _attn_kernel(qkv_ref, ctx_ref, bias_ref, o_ref):
    rows = qkv_ref.shape[1] // GRID_W
    nt = (((1,), (1,)), ((), ()))
    scale = HEAD_DIM ** -0.5

    def row_step(r, carry):
        start = jnp.clip(r - KH // 2, 0, rows - KH)
        off = start - r + (KH_MAX - 1)
        q0 = pl.multiple_of(r * GRID_W, GRID_W)
        k0 = pl.multiple_of(start * GRID_W, GRID_W)
        heads = range(NA_HEADS)
        cols = [slice(h * HEAD_DIM, (h + 1) * HEAD_DIM) for h in heads]
        kcols = [slice(NA_WIDTH + h * HEAD_DIM, NA_WIDTH + (h + 1) * HEAD_DIM) for h in heads]
        vcols = [slice(2 * NA_WIDTH + h * HEAD_DIM, 2 * NA_WIDTH + (h + 1) * HEAD_DIM) for h in heads]
        q = [qkv_ref[0, pl.ds(q0, GRID_W), cols[h]] for h in heads]
        s_win = [lax.dot_general(q[h], qkv_ref[0, pl.ds(k0, KH * GRID_W), kcols[h]], nt,
                                 preferred_element_type=F32) * scale + bias_ref[h, off] for h in heads]
        s_ctx = [lax.dot_general(q[h], ctx_ref[0, :, cols[h]], nt, preferred_element_type=F32) * scale
                 for h in heads]
        m = [jnp.maximum(jnp.max(s_win[h], axis=-1, keepdims=True), jnp.max(s_ctx[h], axis=-1, keepdims=True))
             for h in heads]
        p_win = [jnp.exp(s_win[h] - m[h]) for h in heads]
        p_ctx = [jnp.exp(s_ctx[h] - m[h]) for h in heads]
        l = [jnp.sum(p_win[h], axis=-1, keepdims=True) + jnp.sum(p_ctx[h], axis=-1, keepdims=True) for h in heads]
        o = [jnp.dot(p_win[h].astype(BF16), qkv_ref[0, pl.ds(k0, KH * GRID_W), vcols[h]], preferred_element_type=F32)
             + jnp.dot(p_ctx[h].astype(BF16), ctx_ref[0, :, kcols[h]], preferred_element_type=F32) for h in heads]
        outs = [o[h] / l[h] for h in heads]
        o_ref[0, pl.ds(q0, GRID_W), :] = jnp.concatenate(outs, axis=-1).astype(o_ref.dtype)
        return carry

    lax.fori_loop(0, rows, row_step, 0)


def _attention_bias(rpb):
    cols = np.arange(GRID_W)
    col_start = np.clip(cols - KW // 2, 0, GRID_W - KW)
    in_win = (cols[None, :] >= col_start[:, None]) & (cols[None, :] < col_start[:, None] + KW)
    col_off = np.clip(cols[None, :] - cols[:, None] + KW - 1, 0, 2 * KW - 2)
    onehot = jnp.asarray(col_off[None] == np.arange(2 * KW - 1)[:, None, None], F32)
    by_row = jnp.sum(rpb.astype(F32)[:, :, :, None, None] * onehot[None, None], axis=2)
    by_row = jnp.where(in_win[None, None], by_row, NEG_INF)
    b = jnp.stack([by_row[:, o:o + KH] for o in range(KH)], axis=1)
    return b.transpose(0, 1, 3, 2, 4).reshape(NA_HEADS, KH, GRID_W, KH * GRID_W)


def _attention(qkv, ctx_kv, bias):
    b, s, _ = qkv.shape
    l = ctx_kv.shape[1]
    return pl.pallas_call(
        _attn_kernel,
        grid=(b,),
        in_specs=[pl.BlockSpec((1, s, 3 * NA_WIDTH), lambda i: (i, 0, 0)),
                  pl.BlockSpec((1, l, 2 * NA_WIDTH), lambda i: (i, 0, 0)),
                  pl.BlockSpec(bias.shape, lambda i: (0, 0, 0, 0), pipeline_mode=pl.Buffered(1))],
        out_specs=pl.BlockSpec((1, s, NA_WIDTH), lambda i: (i, 0, 0)),
        out_shape=jax.ShapeDtypeStruct((b, s, NA_WIDTH), BF16),
        compiler_params=_cparams(1),
        name="nbr_attention",
    )(qkv, ctx_kv, bias)


FN_ROW_CHUNK = 256


def _fnet_kernel(f_ref, cn_ref, sn_ref, c64_ref, s64_ref, wbd_ref, o_ref, a_scr, b_scr, xa_scr, xb_scr):
    n = f_ref.shape[1]
    scale = (n * FN_GROUP_DIM) ** -0.5

    @pl.when(pl.program_id(0) == 0)
    def _():
        hi = lax.Precision.HIGHEST
        a_scr[...] = (jnp.dot(c64_ref[...], wbd_ref[...], preferred_element_type=F32, precision=hi) * scale).astype(BF16)
        b_scr[...] = (jnp.dot(s64_ref[...], wbd_ref[...], preferred_element_type=F32, precision=hi) * -scale).astype(BF16)

    x = f_ref[0]
    xa_scr[...] = jnp.dot(x, a_scr[...], preferred_element_type=F32).astype(BF16)
    xb_scr[...] = jnp.dot(x, b_scr[...], preferred_element_type=F32).astype(BF16)

    def chunk(i, carry):
        r0 = pl.multiple_of(i * FN_ROW_CHUNK, FN_ROW_CHUNK)
        y = (jnp.dot(cn_ref[pl.ds(r0, FN_ROW_CHUNK), :], xa_scr[...], preferred_element_type=F32)
             + jnp.dot(sn_ref[pl.ds(r0, FN_ROW_CHUNK), :], xb_scr[...], preferred_element_type=F32))
        o_ref[0, pl.ds(r0, FN_ROW_CHUNK), :] = y.astype(o_ref.dtype)
        return carry

    lax.fori_loop(0, n // FN_ROW_CHUNK, chunk, 0)


def _dft_tables(n):
    k = (np.arange(n)[:, None] * np.arange(n)[None, :]) % n
    ang = 2.0 * np.pi * k.astype(np.float64) / n
    return np.cos(ang), np.sin(ang)


def _fnet(f, fn_w):
    b, s, _ = f.shape
    cn, sn = _dft_tables(s)
    c64, s64 = _dft_tables(FN_GROUP_DIM)
    eye = np.eye(FN_GROUPS)
    c64bd = jnp.asarray(np.kron(eye, c64), F32)
    s64bd = jnp.asarray(np.kron(eye, s64), F32)
    wbd = (jnp.asarray(eye, F32)[:, None, :, None] * fn_w[:, :, None, :]).reshape(FN_WIDTH, FN_WIDTH)
    const = lambda shape: pl.BlockSpec(shape, lambda i: (0, 0), pipeline_mode=pl.Buffered(1))
    return pl.pallas_call(
        _fnet_kernel,
        grid=(b,),
        in_specs=[pl.BlockSpec((1, s, FN_WIDTH), lambda i: (i, 0, 0)),
                  const((s, s)), const((s, s)),
                  const((FN_WIDTH, FN_WIDTH)), const((FN_WIDTH, FN_WIDTH)), const((FN_WIDTH, FN_WIDTH))],
        out_specs=pl.BlockSpec((1, s, FN_WIDTH), lambda i: (i, 0, 0)),
        out_shape=jax.ShapeDtypeStruct((b, s, FN_WIDTH), BF16),
        scratch_shapes=[pltpu.VMEM((FN_WIDTH, FN_WIDTH), BF16), pltpu.VMEM((FN_WIDTH, FN_WIDTH), BF16),
                        pltpu.VMEM((s, FN_WIDTH), BF16), pltpu.VMEM((s, FN_WIDTH), BF16)],
        compiler_params=_cparams(1),
        name="fnet_mix",
    )(f, jnp.asarray(cn, BF16), jnp.asarray(sn, BF16), c64bd, s64bd, wbd)


def _proj_res_kernel(a_ref, f_ref, x_ref, g_ref, w_ref, o_ref):
    ka = a_ref.shape[-1]
    out = (jnp.dot(a_ref[0], w_ref[:ka, :], preferred_element_type=F32)
           + jnp.dot(f_ref[0], w_ref[ka:, :], preferred_element_type=F32))
    o_ref[0] = x_ref[0] + g_ref[0] * out


def _proj_residual(a, f, x, gate, w, tm):
    b, s, d = x.shape
    ka, kf = a.shape[-1], f.shape[-1]
    return pl.pallas_call(
        _proj_res_kernel,
        grid=(b, s // tm),
        in_specs=[pl.BlockSpec((1, tm, ka), lambda bi, i: (bi, i, 0)),
                  pl.BlockSpec((1, tm, kf), lambda bi, i: (bi, i, 0)),
                  pl.BlockSpec((1, tm, d), lambda bi, i: (bi, i, 0)),
                  _mod_spec(gate),
                  pl.BlockSpec((ka + kf, d), lambda bi, i: (0, 0))],
        out_specs=pl.BlockSpec((1, tm, d), lambda bi, i: (bi, i, 0)),
        out_shape=jax.ShapeDtypeStruct((b, s, d), F32),
        compiler_params=_cparams(2),
        name="proj_residual",
    )(a, f, x, gate, w)


def _conv_in_kernel(x_ref, g_ref, sh_ref, sc_ref, w_ref, bg_ref, u_ref):
    d = bg_ref.shape[-1]
    h = _norm_mod(x_ref[0], g_ref[...], sh_ref[0], sc_ref[0])
    acc = jnp.dot(h.astype(BF16), w_ref[...], preferred_element_type=F32)
    bg_ref[0] = acc[:, :d]
    u_ref[0] = acc[:, d:2 * d] * acc[:, 2 * d:]


def _conv_in(x, g, sh, sc, w, tm):
    b, s, d = x.shape
    n = w.shape[1]
    return pl.pallas_call(
        _conv_in_kernel,
        grid=(b, s // tm),
        in_specs=[pl.BlockSpec((1, tm, d), lambda bi, i: (bi, i, 0)),
                  pl.BlockSpec((1, d), lambda bi, i: (0, 0)),
                  _mod_spec(sh), _mod_spec(sc),
                  pl.BlockSpec((d, n), lambda bi, i: (0, 0))],
        out_specs=[pl.BlockSpec((1, tm, d), lambda bi, i: (bi, i, 0))] * 2,
        out_shape=[jax.ShapeDtypeStruct((b, s, d), F32)] * 2,
        compiler_params=_cparams(2),
        name="conv_in",
    )(x, g.reshape(1, d), sh, sc, w)


def _conv_out_kernel(bg_ref, u_ref, up_ref, un_ref, cw_ref, x_ref, g_ref, w_ref, o_ref):
    i = pl.program_id(1)
    last = pl.num_programs(1) - 1
    u = u_ref[0]
    tm = u.shape[0]
    row = lax.broadcasted_iota(jnp.int32, u.shape, 0)
    prev_row = jnp.where(i == 0, 0.0, up_ref[0, SUBLANES - 1:SUBLANES, :])
    next_row = jnp.where(i == last, 0.0, un_ref[0, 0:1, :])
    u_prev = jnp.where(row == 0, prev_row, pltpu.roll(u, 1, axis=0))
    u_next = jnp.where(row == tm - 1, next_row, pltpu.roll(u, tm - 1, axis=0))
    y = cw_ref[0:1, :] * u_prev + cw_ref[1:2, :] * u + cw_ref[2:3, :] * u_next
    z = (bg_ref[0] * y).astype(BF16)
    o_ref[0] = x_ref[0] + g_ref[0] * jnp.dot(z, w_ref[...], preferred_element_type=F32)


def _conv_out(bg, u, cw, x, gate, w, tm):
    b, s, d = x.shape
    hb = tm // SUBLANES
    nhb = s // SUBLANES
    tile = pl.BlockSpec((1, tm, d), lambda bi, i: (bi, i, 0))
    return pl.pallas_call(
        _conv_out_kernel,
        grid=(b, s // tm),
        in_specs=[tile, tile,
                  pl.BlockSpec((1, SUBLANES, d), lambda bi, i: (bi, jnp.maximum(i * hb - 1, 0), 0)),
                  pl.BlockSpec((1, SUBLANES, d), lambda bi, i: (bi, jnp.minimum((i + 1) * hb, nhb - 1), 0)),
                  pl.BlockSpec(cw.shape, lambda bi, i: (0, 0)),
                  tile, _mod_spec(gate),
                  pl.BlockSpec((d, d), lambda bi, i: (0, 0))],
        out_specs=tile,
        out_shape=jax.ShapeDtypeStruct((b, s, d), F32),
        compiler_params=_cparams(2),
        name="conv_out",
    )(bg, u, u, u, cw, x, gate, w)


PEER_TM = 256
KEY_TILES = PEER_NKEYS // SUBLANES
assert PEER_TOPK == 2 * SUBLANES


def _tree(op, xs):
    xs = list(xs)
    while len(xs) > 1:
        xs = [op(xs[i], xs[i + 1]) for i in range(0, len(xs) - 1, 2)] + ([xs[-1]] if len(xs) % 2 else [])
    return xs[0]


def _all_sublanes(op, x):
    for shift in (4, 2, 1):
        x = op(x, pltpu.roll(x, shift, axis=0))
    return x


def _top_keys(problems, sub):
    key_id = [sub + SUBLANES * v for v in range(KEY_TILES)]
    problems = [list(tiles) for tiles in problems]
    out = [[] for _ in problems]
    for _ in range(PEER_TOPK):
        for i, tiles in enumerate(problems):
            m = _all_sublanes(jnp.maximum, _tree(jnp.maximum, tiles))
            idx = _all_sublanes(jnp.minimum, _tree(jnp.minimum, [jnp.where(t == m, k, PEER_NKEYS)
                                                                 for t, k in zip(tiles, key_id)]))
            problems[i] = [jnp.where(k == idx, NEG_INF, t) for t, k in zip(tiles, key_id)]
            out[i].append((m, idx))
    return out


def _rows_of(ranked, sub, which, pick):
    t = ranked[pick(0)][which]
    for r in range(1, SUBLANES):
        t = jnp.where(sub == r, ranked[pick(r)][which], t)
    return t


def _product_key_topk(pairs, sub):
    def tiles(first, second, which):
        lo = _rows_of(second, sub, which, lambda r: r)
        hi = _rows_of(second, sub, which, lambda r: SUBLANES + r)
        quad = _rows_of(second, sub, which, lambda r: r % 4)
        a45 = _rows_of(first, sub, which, lambda r: 4 + r // 4)
        a67 = _rows_of(first, sub, which, lambda r: 6 + r // 4)
        ahi = _rows_of(first, sub, which, lambda r: SUBLANES + r)
        a = [first[k][which] for k in range(4)]
        return (a[0], lo), (a[0], hi), (a[1], lo), (a[2], lo), (a[3], lo), (a45, quad), (a67, quad), (ahi, second[0][which])

    cands = [[x + y for x, y in tiles(f, s, 0)] for f, s in pairs]
    cidxs = [[x * PEER_NKEYS + y for x, y in tiles(f, s, 1)] for f, s in pairs]
    quad_pos = jnp.where(sub < 4, sub, sub + (PEER_TOPK - 4))
    pos = [sub, sub + 8, sub + 16, sub + 32, sub + 48, quad_pos + 64, quad_pos + 96, (sub + 8) * PEER_TOPK]
    out = [[] for _ in pairs]
    for _ in range(PEER_TOPK):
        for i, (cand, cidx) in enumerate(zip(cands, cidxs)):
            m = _all_sublanes(jnp.maximum, _tree(jnp.maximum, cand))
            psel = _all_sublanes(jnp.minimum, _tree(jnp.minimum, [jnp.where(c == m, p, PEER_TOPK * PEER_TOPK)
                                                                  for c, p in zip(cand, pos)]))
            hits = [p == psel for p in pos]
            e = _all_sublanes(jnp.maximum, _tree(jnp.maximum, [jnp.where(hh, x, -1) for hh, x in zip(hits, cidx)]))
            cands[i] = [jnp.where(hh, NEG_INF, c) for hh, c in zip(hits, cand)]
            out[i].append((m, e))
    return out


def _peer_score_kernel(x_ref, g_ref, sh_ref, sc_ref, wq_ref, keys_ref, h_ref, eidx_ref, gate_ref,
                       q_scr, s_scr, e_scr, p_scr):
    tm = x_ref.shape[1]
    nt = (((1,), (1,)), ((), ()))
    h = _norm_mod(x_ref[0], g_ref[...], sh_ref[0], sc_ref[0])
    h_ref[0] = h
    q = jnp.dot(h.astype(BF16), wq_ref[...], preferred_element_type=F32)
    for c in range(2 * PEER_HEADS):
        q_scr[c] = q[:, c * PEER_DK_HALF:(c + 1) * PEER_DK_HALF].astype(BF16)
    sub = lax.broadcasted_iota(jnp.int32, (SUBLANES, LANES), 0)

    def head_step(hd, carry):
        for p in range(2):
            s_scr[p] = lax.dot_general(keys_ref[hd, p], q_scr[2 * hd + p], nt, preferred_element_type=F32)
        row0 = pl.multiple_of(hd * PEER_TOPK, PEER_TOPK)
        groups = tm // LANES
        cols = [slice(lg * LANES, (lg + 1) * LANES) for lg in range(groups)]
        ranked = _top_keys([[s_scr[p, v * SUBLANES:(v + 1) * SUBLANES, cols[lg]] for v in range(KEY_TILES)]
                            for lg in range(groups) for p in range(2)], sub)
        tops = _product_key_topk([(ranked[2 * lg], ranked[2 * lg + 1]) for lg in range(groups)], sub)
        for lg, top in enumerate(tops):
            ex = [(jnp.exp(m - top[0][0]), e) for m, e in top]
            denom = _tree(jnp.add, [v for v, _ in ex])
            for half in range(2):
                pick = lambda r, half=half: half * SUBLANES + r
                rows = pl.ds(row0 + half * SUBLANES, SUBLANES)
                e_scr[rows, cols[lg]] = _rows_of(ex, sub, 1, pick)
                p_scr[rows, cols[lg]] = _rows_of(ex, sub, 0, pick) / denom
        return carry

    lax.fori_loop(0, PEER_HEADS, head_step, 0)
    eidx_ref[0] = e_scr[...].T
    gate_ref[0] = p_scr[...].T


def _peer_score(x, g, sh, sc, wq, keys):
    b, s, d = x.shape
    tm = PEER_TM
    n = wq.shape[1]
    tile = lambda k: pl.BlockSpec((1, tm, k), lambda bi, i: (bi, i, 0))
    return pl.pallas_call(
        _peer_score_kernel,
        grid=(b, s // tm),
        in_specs=[tile(d),
                  pl.BlockSpec((1, d), lambda bi, i: (0, 0)),
                  _mod_spec(sh), _mod_spec(sc),
                  pl.BlockSpec((d, n), lambda bi, i: (0, 0)),
                  pl.BlockSpec(keys.shape, lambda bi, i: (0, 0, 0, 0))],
        out_specs=[tile(d), tile(PEER_SEL), tile(PEER_SEL)],
        out_shape=[jax.ShapeDtypeStruct((b, s, d), F32),
                   jax.ShapeDtypeStruct((b, s, PEER_SEL), jnp.int32),
                   jax.ShapeDtypeStruct((b, s, PEER_SEL), F32)],
        scratch_shapes=[pltpu.VMEM((2 * PEER_HEADS, tm, PEER_DK_HALF), BF16),
                        pltpu.VMEM((2, PEER_NKEYS, tm), F32),
                        pltpu.VMEM((PEER_SEL, tm), jnp.int32),
                        pltpu.VMEM((PEER_SEL, tm), F32)],
        compiler_params=_cparams(2),
        name="peer_score",
    )(x, g.reshape(1, d), sh, sc, wq, keys)


HALF_ROWS = SUBLANES // 2


PACK_EB = 256
DOWN_WHOLE_WORD = True
UP_WHOLE_WORD = False


def _pack_kernel(x_ref, o_ref, *, whole_word):
    x_ref = x_ref.at[0]
    half = x_ref.shape[1] // 2
    is_tail = pl.program_id(0) == pl.num_programs(0) - 1
    row = lax.broadcasted_iota(jnp.int32, (PACK_EB, LANES), 0)
    for c in range(HALF_ROWS):
        lo = x_ref[:, half + c * LANES:half + (c + 1) * LANES].astype(BF16).astype(F32)
        lo_bits = pltpu.bitcast(lo, jnp.int32) >> 16 & 0xFFFF
        hi = x_ref[:, c * LANES:(c + 1) * LANES]
        if whole_word:
            bits = pltpu.bitcast(hi, jnp.int32)
            excess = lo_bits - (bits & 0xFFFF)
            step = jnp.where(excess > 0x8000, -1, jnp.where(excess < -0x8000, 1, 0))
            step = jnp.where((bits & 0x7FFF0000) == 0, jnp.maximum(step, 0), step)
            hi_bits = (bits & jnp.int32(-0x10000)) + (step << 16)
        else:
            hi_bits = pltpu.bitcast(hi.astype(BF16).astype(F32), jnp.int32)
        words = pltpu.bitcast(hi_bits | lo_bits, jnp.uint32)
        tail = jnp.where(row == 1, pltpu.roll(words, 1, axis=0), jnp.uint32(0))
        o_ref[pl.ds(c, PACK_EB, stride=HALF_ROWS), :] = jnp.where(is_tail, tail, words)


def _pack_table(tables, layer, whole_word):
    _, e, d = tables.shape
    assert d // 2 == HALF_ROWS * LANES and e % PACK_EB == 0
    nb = e // PACK_EB
    return pl.pallas_call(
        functools.partial(_pack_kernel, whole_word=whole_word),
        grid=(nb + 1,),
        in_specs=[pl.BlockSpec((1, PACK_EB, d), lambda i: (layer, i % nb, 0))],
        out_specs=pl.BlockSpec((PACK_EB * HALF_ROWS, LANES), lambda i: (i, 0)),
        out_shape=jax.ShapeDtypeStruct(((e + PACK_EB) * HALF_ROWS, LANES), jnp.uint32),
        compiler_params=_cparams(1),
        name="pack_table",
    )(tables)


def _load_rows(eidx, on_low_sublanes, n_experts):
    high_start = jnp.where(eidx == 0, HALF_ROWS * n_experts, HALF_ROWS * (eidx - 1))
    return jnp.where(on_low_sublanes, HALF_ROWS * eidx, high_start)


def _unpack(words, whole_word):
    first = words if whole_word else words & jnp.uint32(0xFFFF0000)
    return (pltpu.bitcast(first, F32), pltpu.bitcast(words << 16, F32))


_FOLD_POS = (6, 2, 4, 0, 7, 3, 5, 1)


def _fold_halves(p, sub):
    m = [jnp.where(sub < HALF_ROWS, p[2 * k], p[2 * k + 1]) for k in range(4)]
    n = [jnp.where((sub & 2) != 0, m[2 * k] + pltpu.roll(m[2 * k], 2, axis=0),
                   m[2 * k + 1] + pltpu.roll(m[2 * k + 1], 6, axis=0)) for k in range(2)]
    return jnp.where((sub & 1) != 0, n[0] + pltpu.roll(n[0], 1, axis=0), n[1] + pltpu.roll(n[1], 7, axis=0))


DOWN_TM = 256
DOWN_BLOCK = 16


def _token_tiles(h_ref, t0, sub):
    chunks = [h_ref[pl.ds(t0, SUBLANES), r * LANES:(r + 1) * LANES] for r in range(SUBLANES)]
    tiles = []
    for s in range(SUBLANES):
        tile = None
        for r in range(SUBLANES):
            piece = jnp.broadcast_to(chunks[r][s:s + 1, :], (SUBLANES, LANES))
            tile = piece if tile is None else jnp.where(sub == r, piece, tile)
        tiles.append(tile)
    return tiles


def _peer_down_kernel(*refs):
    row_refs = refs[:SUBLANES]
    h_ref, gate_ref, tbl_ref, o_ref = refs[SUBLANES:]
    tm = h_ref.shape[0]
    lane = lax.broadcasted_iota(jnp.int32, (SUBLANES, LANES), 1)
    sub = lax.broadcasted_iota(jnp.int32, (SUBLANES, LANES), 0)

    def group(gi, carry):
        t0 = pl.multiple_of(gi * SUBLANES, SUBLANES)
        hs = []
        for s, tile in enumerate(_token_tiles(h_ref, t0, sub)):
            swapped = pltpu.roll(tile, HALF_ROWS, axis=0)
            hs.append((tile, swapped) if _FOLD_POS[s] % 2 == 0 else (swapped, tile))
        gbase = gi * PEER_SEL

        def folded(j):
            prods = [None] * SUBLANES
            for s in range(SUBLANES):
                hi, lo = _unpack(tbl_ref[pl.ds(row_refs[s][gbase + j], SUBLANES), :], DOWN_WHOLE_WORD)
                prods[_FOLD_POS[s]] = hi * hs[s][0] + lo * hs[s][1]
            return _fold_halves(prods, sub)

        def place(acc, j0, folds):
            sums = [jnp.sum(f, axis=-1, keepdims=True) for f in folds]
            for k, tot in enumerate(sums):
                acc = jnp.where(lane == j0 + k, tot, acc)
            return acc

        def block(jb, carry2):
            acc, prev = carry2
            j0 = jb * DOWN_BLOCK
            new = tuple(folded(j0 + k) for k in range(DOWN_BLOCK))
            return place(acc, j0 - DOWN_BLOCK, prev), new

        first = tuple(folded(k) for k in range(DOWN_BLOCK))
        acc, last = lax.fori_loop(1, PEER_SEL // DOWN_BLOCK, block, (jnp.zeros((SUBLANES, LANES), F32), first))
        act = place(acc, PEER_SEL - DOWN_BLOCK, last)
        gelu = act * (lax.erf(act / np.sqrt(2).astype(np.float32)) + 1.0) / 2.0
        o_ref[pl.ds(t0, SUBLANES), :] = gate_ref[pl.ds(t0, SUBLANES), :] * gelu
        return carry

    lax.fori_loop(0, tm // SUBLANES, group, 0)


def _peer_down(eidx, h, gate, tbl):
    t, d = h.shape
    tm = DOWN_TM
    low = (jnp.asarray(_FOLD_POS, jnp.int32) % 2 == 0)[None, :, None]
    rows = _load_rows(eidx.reshape(t // SUBLANES, SUBLANES, PEER_SEL), low, tbl.shape[0] // HALF_ROWS - PACK_EB)
    rows = rows.transpose(1, 0, 2).reshape(SUBLANES, -1)
    smem = pl.BlockSpec((tm // SUBLANES * PEER_SEL,), lambda i: (i,), memory_space=pltpu.SMEM)
    return pl.pallas_call(
        _peer_down_kernel,
        grid=(t // tm,),
        in_specs=[smem] * SUBLANES + [
            pl.BlockSpec((tm, d), lambda i: (i, 0)),
            pl.BlockSpec((tm, PEER_SEL), lambda i: (i, 0)),
            pl.BlockSpec(tbl.shape, lambda i: (0, 0), pipeline_mode=pl.Buffered(1))],
        out_specs=pl.BlockSpec((tm, PEER_SEL), lambda i: (i, 0)),
        out_shape=jax.ShapeDtypeStruct((t, PEER_SEL), F32),
        compiler_params=_cparams(1),
        name="peer_down",
    )(*[rows[s] for s in range(SUBLANES)], h, gate, tbl)


UP_TM = 128
UP_BLOCK = 16
UP_SPREAD = SUBLANES * UP_BLOCK // PEER_SEL


def _peer_up_kernel(*refs):
    row_refs = refs[:SUBLANES]
    w_ref, x_ref, gate_ref, tbl_ref, o_ref, stage_scr, wt_scr, wv_even, wv_odd = refs[SUBLANES:]
    wv_scr = (wv_even, wv_odd)
    tm, d = o_ref.shape
    n_groups = tm // SUBLANES
    sub = lax.broadcasted_iota(jnp.int32, (SUBLANES, LANES), 0)
    low = sub < HALF_ROWS
    wt_scr[...] = w_ref[...].T

    def spread(t, slot, s):
        lanes = jnp.full((PEER_SEL, LANES), t, jnp.int32)
        wv_scr[slot][s] = jnp.take_along_axis(wt_scr[...], lanes, axis=1, mode="promise_in_bounds")

    def group(g, cur):
        t0 = pl.multiple_of(g * SUBLANES, SUBLANES)
        next_t0 = jnp.minimum(g + 1, n_groups - 1) * SUBLANES
        gbase = g * PEER_SEL

        def block(jb, accs):
            accs = list(accs)
            j0 = jb * UP_BLOCK
            for u in range(UP_SPREAD):
                s_next = jb * UP_SPREAD + u
                spread(next_t0 + s_next, 1 - cur, s_next)
            wv_rows = [wv_scr[cur].at[s, pl.ds(pl.multiple_of(j0, UP_BLOCK), UP_BLOCK)] for s in range(SUBLANES)]
            for k in range(0, UP_BLOCK, 2):
                for s in range(SUBLANES):
                    words = jnp.where(low, tbl_ref[pl.ds(row_refs[s][gbase + j0 + k], SUBLANES), :],
                                      tbl_ref[pl.ds(row_refs[s][gbase + j0 + k + 1], SUBLANES), :])
                    wv = jnp.where(low, jnp.broadcast_to(wv_rows[s][k:k + 1, :], (SUBLANES, LANES)),
                                   jnp.broadcast_to(wv_rows[s][k + 1:k + 2, :], (SUBLANES, LANES)))
                    hi, lo = _unpack(words, UP_WHOLE_WORD)
                    accs[2 * s] = accs[2 * s] + wv * hi
                    accs[2 * s + 1] = accs[2 * s + 1] + wv * lo
            return tuple(accs)

        zero = jnp.zeros((SUBLANES, LANES), F32)
        accs = lax.fori_loop(0, PEER_SEL // UP_BLOCK, block, (zero,) * (2 * SUBLANES))
        for s in range(SUBLANES):
            hi, lo = accs[2 * s], accs[2 * s + 1]
            stage_scr[s * SUBLANES:(s + 1) * SUBLANES, :] = jnp.where(
                low, hi + pltpu.roll(hi, HALF_ROWS, axis=0), lo + pltpu.roll(lo, HALF_ROWS, axis=0))
        for r in range(SUBLANES):
            cols = slice(r * LANES, (r + 1) * LANES)
            o_ref[pl.ds(t0, SUBLANES), cols] = (x_ref[pl.ds(t0, SUBLANES), cols]
                                                + gate_ref[0, :, cols] * stage_scr[pl.ds(r, SUBLANES, stride=SUBLANES), :])

    def group_pair(gp, carry):
        group(2 * gp, 0)
        group(2 * gp + 1, 1)
        return carry

    for s in range(SUBLANES):
        spread(s, 0, s)
    lax.fori_loop(0, n_groups // 2, group_pair, 0)


def _peer_up(eidx, w, tbl, x, gate):
    b, s, d = x.shape
    t = b * s
    tm = UP_TM
    assert s % tm == 0 and tm == LANES
    rows = _load_rows(eidx, (jnp.arange(PEER_SEL) % 2 == 0)[None, :], tbl.shape[0] // HALF_ROWS - PACK_EB)
    rows = rows.reshape(t // SUBLANES, SUBLANES, PEER_SEL).transpose(1, 0, 2).reshape(SUBLANES, -1)
    smem = pl.BlockSpec((tm // SUBLANES * PEER_SEL,), lambda i: (i,), memory_space=pltpu.SMEM)
    tiles_per_batch = s // tm
    out = pl.pallas_call(
        _peer_up_kernel,
        grid=(t // tm,),
        in_specs=[smem] * SUBLANES + [
            pl.BlockSpec((tm, PEER_SEL), lambda i: (i, 0)),
            pl.BlockSpec((tm, d), lambda i: (i, 0)),
            pl.BlockSpec((1, 1, d), lambda i: (i // tiles_per_batch, 0, 0)),
            pl.BlockSpec(tbl.shape, lambda i: (0, 0), pipeline_mode=pl.Buffered(1))],
        out_specs=pl.BlockSpec((tm, d), lambda i: (i, 0)),
        out_shape=jax.ShapeDtypeStruct((t, d), F32),
        scratch_shapes=[pltpu.VMEM((SUBLANES * SUBLANES, LANES), F32),
                        pltpu.VMEM((PEER_SEL, tm), F32),
                        pltpu.VMEM((SUBLANES, PEER_SEL, LANES), F32), pltpu.VMEM((SUBLANES, PEER_SEL, LANES), F32)],
        compiler_params=_cparams(1),
        name="peer_up",
    )(*[rows[u] for u in range(SUBLANES)], w, x.reshape(t, d), gate, tbl)
    return out.reshape(b, s, d)


def _peer_block(x, g, sh, sc, gate2, wq, keys, down, up, layer):
    b, s, d = x.shape
    t = b * s
    h, eidx, gsm = _peer_score(x, g, sh, sc, wq.astype(BF16), keys.astype(BF16))
    eidx = eidx.reshape(t, PEER_SEL)
    w = _peer_down(eidx, h.reshape(t, d), gsm.reshape(t, PEER_SEL), _pack_table(down, layer, DOWN_WHOLE_WORD))
    return _peer_up(eidx, w, _pack_table(up, layer, UP_WHOLE_WORD), x, gate2)


def _rmsnorm_kernel(x_ref, g_ref, o_ref):
    x = x_ref[0]
    o_ref[0] = (x * lax.rsqrt(jnp.mean(x * x, axis=-1, keepdims=True) + EPS)) * g_ref[...]


def _rmsnorm(x, g, tm):
    b, s, d = x.shape
    tile = pl.BlockSpec((1, tm, d), lambda bi, i: (bi, i, 0))
    return pl.pallas_call(
        _rmsnorm_kernel,
        grid=(b, s // tm),
        in_specs=[tile, pl.BlockSpec((1, d), lambda bi, i: (0, 0))],
        out_specs=tile,
        out_shape=jax.ShapeDtypeStruct((b, s, d), F32),
        compiler_params=_cparams(2),
        name="final_rmsnorm",
    )(x, g.reshape(1, d))


def kernel(x, c, ctx, c_ctx, ada_w, ada_b, norm1_g, norm2_g, final_g, ab_w_in, ab_w_out, na_rpb, fn_w,
           cv_w_in, cv_w, cv_w_out, peer_w_q, peer_keys, peer_down, peer_up):
    b, s, d = x.shape
    depth = ada_w.shape[0]
    rows = -(-(b + 1) // SUBLANES) * SUBLANES
    cc = jnp.concatenate([c, c_ctx[None], jnp.zeros((rows - b - 1, d), F32)], axis=0)
    mod = _ada_vectors(cc, ada_w, ada_b)

    def chunks(i, lo, hi):
        m = mod[i, lo:hi].reshape(hi - lo, 1, 6, d)
        return [m[:, :, k] for k in range(6)]

    for i in range(depth):
        sh1, sc1, g1, sh2, sc2, g2 = chunks(i, 0, b)
        if i % 2 == 0:
            e = i // 2
            csh1, csc1 = chunks(i, b, b + 1)[:2]
            w_in = ab_w_in[e].astype(BF16)
            qkv, f = _norm_mod_matmul(x, norm1_g[i], sh1, sc1, w_in, (3 * NA_WIDTH, FN_WIDTH), (BF16, BF16),
                                      512, "ab_in")
            (ctx_kv,) = _norm_mod_matmul(ctx, norm1_g[i], csh1, csc1, w_in[:, NA_WIDTH:3 * NA_WIDTH],
                                         (2 * NA_WIDTH,), (BF16,), ctx.shape[1], "ab_in_ctx")
            a = _attention(qkv, ctx_kv, _attention_bias(na_rpb[e]))
            fm = _fnet(f, fn_w[e])
            x = _proj_residual(a, fm, x, g1, ab_w_out[e].astype(BF16), 512)
        else:
            o = i // 2
            bg, u = _conv_in(x, norm1_g[i], sh1, sc1, cv_w_in[o].astype(BF16), 256)
            x = _conv_out(bg, u, cv_w[o], x, g1, cv_w_out[o].astype(BF16), 512)
        x = _peer_block(x, norm2_g[i], sh2, sc2, g2, peer_w_q[i], peer_keys[i], peer_down, peer_up, i)
    return _rmsnorm(x, final_g, 512)
```

```python
import functools

import jax
import jax.numpy as jnp
import numpy as np
from jax import lax
from jax.experimental import pallas as pl
from jax.experimental.pallas import tpu as pltpu

F32 = jnp.float32
BF16 = jnp.bfloat16
EPS = 1e-6

GRID_W = 64
HEAD_DIM = 64
NA_HEADS = 8
NA_WIDTH = NA_HEADS * HEAD_DIM
KH = 8
KH_MAX = 8
KW = 16
FN_GROUPS = 8
FN_GROUP_DIM = 64
FN_WIDTH = FN_GROUPS * FN_GROUP_DIM
PEER_HEADS = 8
PEER_NKEYS = 128
PEER_DK_HALF = 128
PEER_TOPK = 16
PEER_SEL = PEER_HEADS * PEER_TOPK
LANES = 128
SUBLANES = 8
VMEM_LIMIT = 56 * 1024 * 1024

NEG_INF = float("-inf")


def _cparams(n_axes):
    return pltpu.CompilerParams(dimension_semantics=("arbitrary",) * n_axes, vmem_limit_bytes=VMEM_LIMIT)


def _norm_mod(x, g, sh, sc):
    y = x * lax.rsqrt(jnp.mean(x * x, axis=-1, keepdims=True) + EPS)
    return (y * g) * (1.0 + sc) + sh


def _ada_kernel(c_ref, w_ref, b_ref, o_ref):
    c = c_ref[...]
    s = c / (1.0 + jnp.exp(-c))
    o_ref[0] = jnp.dot(s, w_ref[0], preferred_element_type=F32, precision=lax.Precision.HIGHEST) + b_ref[0]


def _ada_vectors(cc, ada_w, ada_b):
    depth, d, n = ada_w.shape
    rows = cc.shape[0]
    tn = 1536
    return pl.pallas_call(
        _ada_kernel,
        grid=(depth, n // tn),
        in_specs=[pl.BlockSpec((rows, d), lambda i, j: (0, 0)),
                  pl.BlockSpec((1, d, tn), lambda i, j: (i, 0, j)),
                  pl.BlockSpec((1, 1, tn), lambda i, j: (i, 0, j))],
        out_specs=pl.BlockSpec((1, rows, tn), lambda i, j: (i, 0, j)),
        out_shape=jax.ShapeDtypeStruct((depth, rows, n), F32),
        compiler_params=_cparams(2),
        name="ada_vectors",
    )(cc, ada_w, ada_b.reshape(depth, 1, n))


def _nmm_kernel(x_ref, g_ref, sh_ref, sc_ref, w_ref, *o_refs):
    h = _norm_mod(x_ref[0], g_ref[...], sh_ref[0], sc_ref[0])
    acc = jnp.dot(h.astype(BF16), w_ref[...], preferred_element_type=F32)
    off = 0
    for o_ref in o_refs:
        n = o_ref.shape[-1]
        o_ref[0] = acc[:, off:off + n].astype(o_ref.dtype)
        off += n


def _mod_spec(arr):
    d = arr.shape[-1]
    if arr.shape[0] == 1:
        return pl.BlockSpec((1, 1, d), lambda b, i: (0, 0, 0))
    return pl.BlockSpec((1, 1, d), lambda b, i: (b, 0, 0))


def _norm_mod_matmul(x, g, sh, sc, w, splits, dtypes, tm, name):
    b, s, d = x.shape
    n = w.shape[1]
    assert sum(splits) == n
    return pl.pallas_call(
        _nmm_kernel,
        grid=(b, s // tm),
        in_specs=[pl.BlockSpec((1, tm, d), lambda bi, i: (bi, i, 0)),
                  pl.BlockSpec((1, d), lambda bi, i: (0, 0)),
                  _mod_spec(sh), _mod_spec(sc),
                  pl.BlockSpec((d, n), lambda bi, i: (0, 0))],
        out_specs=[pl.BlockSpec((1, tm, k), lambda bi, i: (bi, i, 0)) for k in splits],
        out_shape=[jax.ShapeDtypeStruct((b, s, k), dt) for k, dt in zip(splits, dtypes)],
        compiler_params=_cparams(2),
        name=name,
    )(x, g.reshape(1, d), sh, sc, w)


def _attn_kernel(qkv_ref, ctx_ref, bias_ref, o_ref):
    rows = qkv_ref.shape[1] // GRID_W
    nt = (((1,), (1,)), ((), ()))
    scale = HEAD_DIM ** -0.5

    def row_step(r, carry):
        start = jnp.clip(r - KH // 2, 0, rows - KH)
        off = start - r + (KH_MAX - 1)
        q0 = pl.multiple_of(r * GRID_W, GRID_W)
        k0 = pl.multiple_of(start * GRID_W, GRID_W)
        heads = range(NA_HEADS)
        cols = [slice(h * HEAD_DIM, (h + 1) * HEAD_DIM) for h in heads]
        kcols = [slice(NA_WIDTH + h * HEAD_DIM, NA_WIDTH + (h + 1) * HEAD_DIM) for h in heads]
        vcols = [slice(2 * NA_WIDTH + h * HEAD_DIM, 2 * NA_WIDTH + (h + 1) * HEAD_DIM) for h in heads]
        q = [qkv_ref[0, pl.ds(q0, GRID_W), cols[h]] for h in heads]
        s_win = [lax.dot_general(q[h], qkv_ref[0, pl.ds(k0, KH * GRID_W), kcols[h]], nt,
                                 preferred_element_type=F32) * scale + bias_ref[h, off] for h in heads]
        s_ctx = [lax.dot_general(q[h], ctx_ref[0, :, cols[h]], nt, preferred_element_type=F32) * scale
                 for h in heads]
        m = [jnp.maximum(jnp.max(s_win[h], axis=-1, keepdims=True), jnp.max(s_ctx[h], axis=-1, keepdims=True))
             for h in heads]
        p_win = [jnp.exp(s_win[h] - m[h]) for h in heads]
        p_ctx = [jnp.exp(s_ctx[h] - m[h]) for h in heads]
        l = [jnp.sum(p_win[h], axis=-1, keepdims=True) + jnp.sum(p_ctx[h], axis=-1, keepdims=True) for h in heads]
        o = [jnp.dot(p_win[h].astype(BF16), qkv_ref[0, pl.ds(k0, KH * GRID_W), vcols[h]], preferred_element_type=F32)
             + jnp.dot(p_ctx[h].astype(BF16), ctx_ref[0, :, kcols[h]], preferred_element_type=F32) for h in heads]
        outs = [o[h] / l[h] for h in heads]
        o_ref[0, pl.ds(q0, GRID_W), :] = jnp.concatenate(outs, axis=-1).astype(o_ref.dtype)
        return carry

    lax.fori_loop(0, rows, row_step, 0)


def _attention_bias(rpb):
    cols = np.arange(GRID_W)
    col_start = np.clip(cols - KW // 2, 0, GRID_W - KW)
    in_win = (cols[None, :] >= col_start[:, None]) & (cols[None, :] < col_start[:, None] + KW)
    col_off = np.clip(cols[None, :] - cols[:, None] + KW - 1, 0, 2 * KW - 2)
    onehot = jnp.asarray(col_off[None] == np.arange(2 * KW - 1)[:, None, None], F32)
    by_row = jnp.sum(rpb.astype(F32)[:, :, :, None, None] * onehot[None, None], axis=2)
    by_row = jnp.where(in_win[None, None], by_row, NEG_INF)
    b = jnp.stack([by_row[:, o:o + KH] for o in range(KH)], axis=1)
    return b.transpose(0, 1, 3, 2, 4).reshape(NA_HEADS, KH, GRID_W, KH * GRID_W)


def _attention(qkv, ctx_kv, bias):
    b, s, _ = qkv.shape
    l = ctx_kv.shape[1]
    return pl.pallas_call(
        _attn_kernel,
        grid=(b,),
        in_specs=[pl.BlockSpec((1, s, 3 * NA_WIDTH), lambda i: (i, 0, 0)),
                  pl.BlockSpec((1, l, 2 * NA_WIDTH), lambda i: (i, 0, 0)),
                  pl.BlockSpec(bias.shape, lambda i: (0, 0, 0, 0), pipeline_mode=pl.Buffered(1))],
        out_specs=pl.BlockSpec((1, s, NA_WIDTH), lambda i: (i, 0, 0)),
        out_shape=jax.ShapeDtypeStruct((b, s, NA_WIDTH), BF16),
        compiler_params=_cparams(1),
        name="nbr_attention",
    )(qkv, ctx_kv, bias)


FN_ROW_CHUNK = 256


def _fnet_kernel(f_ref, cn_ref, sn_ref, c64_ref, s64_ref, wbd_ref, o_ref, a_scr, b_scr, xa_scr, xb_scr):
    n = f_ref.shape[1]
    scale = (n * FN_GROUP_DIM) ** -0.5

    @pl.when(pl.program_id(0) == 0)
    def _():
        hi = lax.Precision.HIGHEST
        a_scr[...] = (jnp.dot(c64_ref[...], wbd_ref[...], preferred_element_type=F32, precision=hi) * scale).astype(BF16)
        b_scr[...] = (jnp.dot(s64_ref[...], wbd_ref[...], preferred_element_type=F32, precision=hi) * -scale).astype(BF16)

    x = f_ref[0]
    xa_scr[...] = jnp.dot(x, a_scr[...], preferred_element_type=F32).astype(BF16)
    xb_scr[...] = jnp.dot(x, b_scr[...], preferred_element_type=F32).astype(BF16)

    def chunk(i, carry):
        r0 = pl.multiple_of(i * FN_ROW_CHUNK, FN_ROW_CHUNK)
        y = (jnp.dot(cn_ref[pl.ds(r0, FN_ROW_CHUNK), :], xa_scr[...], preferred_element_type=F32)
             + jnp.dot(sn_ref[pl.ds(r0, FN_ROW_CHUNK), :], xb_scr[...], preferred_element_type=F32))
        o_ref[0, pl.ds(r0, FN_ROW_CHUNK), :] = y.astype(o_ref.dtype)
        return carry

    lax.fori_loop(0, n // FN_ROW_CHUNK, chunk, 0)


def _dft_tables(n):
    k = (np.arange(n)[:, None] * np.arange(n)[None, :]) % n
    ang = 2.0 * np.pi * k.astype(np.float64) / n
    return np.cos(ang), np.sin(ang)


def _fnet(f, fn_w):
    b, s, _ = f.shape
    cn, sn = _dft_tables(s)
    c64, s64 = _dft_tables(FN_GROUP_DIM)
    eye = np.eye(FN_GROUPS)
    c64bd = jnp.asarray(np.kron(eye, c64), F32)
    s64bd = jnp.asarray(np.kron(eye, s64), F32)
    wbd = (jnp.asarray(eye, F32)[:, None, :, None] * fn_w[:, :, None, :]).reshape(FN_WIDTH, FN_WIDTH)
    const = lambda shape: pl.BlockSpec(shape, lambda i: (0, 0), pipeline_mode=pl.Buffered(1))
    return pl.pallas_call(
        _fnet_kernel,
        grid=(b,),
        in_specs=[pl.BlockSpec((1, s, FN_WIDTH), lambda i: (i, 0, 0)),
                  const((s, s)), const((s, s)),
                  const((FN_WIDTH, FN_WIDTH)), const((FN_WIDTH, FN_WIDTH)), const((FN_WIDTH, FN_WIDTH))],
        out_specs=pl.BlockSpec((1, s, FN_WIDTH), lambda i: (i, 0, 0)),
        out_shape=jax.ShapeDtypeStruct((b, s, FN_WIDTH), BF16),
        scratch_shapes=[pltpu.VMEM((FN_WIDTH, FN_WIDTH), BF16), pltpu.VMEM((FN_WIDTH, FN_WIDTH), BF16),
                        pltpu.VMEM((s, FN_WIDTH), BF16), pltpu.VMEM((s, FN_WIDTH), BF16)],
        compiler_params=_cparams(1),
        name="fnet_mix",
    )(f, jnp.asarray(cn, BF16), jnp.asarray(sn, BF16), c64bd, s64bd, wbd)


def _proj_res_kernel(a_ref, f_ref, x_ref, g_ref, w_ref, o_ref):
    ka = a_ref.shape[-1]
    out = (jnp.dot(a_ref[0], w_ref[:ka, :], preferred_element_type=F32)
           + jnp.dot(f_ref[0], w_ref[ka:, :], preferred_element_type=F32))
    o_ref[0] = x_ref[0] + g_ref[0] * out


def _proj_residual(a, f, x, gate, w, tm):
    b, s, d = x.shape
    ka, kf = a.shape[-1], f.shape[-1]
    return pl.pallas_call(
        _proj_res_kernel,
        grid=(b, s // tm),
        in_specs=[pl.BlockSpec((1, tm, ka), lambda bi, i: (bi, i, 0)),
                  pl.BlockSpec((1, tm, kf), lambda bi, i: (bi, i, 0)),
                  pl.BlockSpec((1, tm, d), lambda bi, i: (bi, i, 0)),
                  _mod_spec(gate),
                  pl.BlockSpec((ka + kf, d), lambda bi, i: (0, 0))],
        out_specs=pl.BlockSpec((1, tm, d), lambda bi, i: (bi, i, 0)),
        out_shape=jax.ShapeDtypeStruct((b, s, d), F32),
        compiler_params=_cparams(2),
        name="proj_residual",
    )(a, f, x, gate, w)


def _conv_in_kernel(x_ref, g_ref, sh_ref, sc_ref, w_ref, bg_ref, u_ref):
    d = bg_ref.shape[-1]
    h = _norm_mod(x_ref[0], g_ref[...], sh_ref[0], sc_ref[0])
    acc = jnp.dot(h.astype(BF16), w_ref[...], preferred_element_type=F32)
    bg_ref[0] = acc[:, :d]
    u_ref[0] = acc[:, d:2 * d] * acc[:, 2 * d:]


def _conv_in(x, g, sh, sc, w, tm):
    b, s, d = x.shape
    n = w.shape[1]
    return pl.pallas_call(
        _conv_in_kernel,
        grid=(b, s // tm),
        in_specs=[pl.BlockSpec((1, tm, d), lambda bi, i: (bi, i, 0)),
                  pl.BlockSpec((1, d), lambda bi, i: (0, 0)),
                  _mod_spec(sh), _mod_spec(sc),
                  pl.BlockSpec((d, n), lambda bi, i: (0, 0))],
        out_specs=[pl.BlockSpec((1, tm, d), lambda bi, i: (bi, i, 0))] * 2,
        out_shape=[jax.ShapeDtypeStruct((b, s, d), F32)] * 2,
        compiler_params=_cparams(2),
        name="conv_in",
    )(x, g.reshape(1, d), sh, sc, w)


def _conv_out_kernel(bg_ref, u_ref, up_ref, un_ref, cw_ref, x_ref, g_ref, w_ref, o_ref):
    i = pl.program_id(1)
    last = pl.num_programs(1) - 1
    u = u_ref[0]
    tm = u.shape[0]
    row = lax.broadcasted_iota(jnp.int32, u.shape, 0)
    prev_row = jnp.where(i == 0, 0.0, up_ref[0, SUBLANES - 1:SUBLANES, :])
    next_row = jnp.where(i == last, 0.0, un_ref[0, 0:1, :])
    u_prev = jnp.where(row == 0, prev_row, pltpu.roll(u, 1, axis=0))
    u_next = jnp.where(row == tm - 1, next_row, pltpu.roll(u, tm - 1, axis=0))
    y = cw_ref[0:1, :] * u_prev + cw_ref[1:2, :] * u + cw_ref[2:3, :] * u_next
    z = (bg_ref[0] * y).astype(BF16)
    o_ref[0] = x_ref[0] + g_ref[0] * jnp.dot(z, w_ref[...], preferred_element_type=F32)


def _conv_out(bg, u, cw, x, gate, w, tm):
    b, s, d = x.shape
    hb = tm // SUBLANES
    nhb = s // SUBLANES
    tile = pl.BlockSpec((1, tm, d), lambda bi, i: (bi, i, 0))
    return pl.pallas_call(
        _conv_out_kernel,
        grid=(b, s // tm),
        in_specs=[tile, tile,
                  pl.BlockSpec((1, SUBLANES, d), lambda bi, i: (bi, jnp.maximum(i * hb - 1, 0), 0)),
                  pl.BlockSpec((1, SUBLANES, d), lambda bi, i: (bi, jnp.minimum((i + 1) * hb, nhb - 1), 0)),
                  pl.BlockSpec(cw.shape, lambda bi, i: (0, 0)),
                  tile, _mod_spec(gate),
                  pl.BlockSpec((d, d), lambda bi, i: (0, 0))],
        out_specs=tile,
        out_shape=jax.ShapeDtypeStruct((b, s, d), F32),
        compiler_params=_cparams(2),
        name="conv_out",
    )(bg, u, u, u, cw, x, gate, w)


PEER_TM = 256
KEY_TILES = PEER_NKEYS // SUBLANES
assert PEER_TOPK == 2 * SUBLANES


def _tree(op, xs):
    xs = list(xs)
    while len(xs) > 1:
        xs = [op(xs[i], xs[i + 1]) for i in range(0, len(xs) - 1, 2)] + ([xs[-1]] if len(xs) % 2 else [])
    return xs[0]


def _all_sublanes(op, x):
    for shift in (4, 2, 1):
        x = op(x, pltpu.roll(x, shift, axis=0))
    return x


def _top_keys(problems, sub):
    key_id = [sub + SUBLANES * v for v in range(KEY_TILES)]
    problems = [list(tiles) for tiles in problems]
    out = [[] for _ in problems]
    for _ in range(PEER_TOPK):
        for i, tiles in enumerate(problems):
            m = _all_sublanes(jnp.maximum, _tree(jnp.maximum, tiles))
            idx = _all_sublanes(jnp.minimum, _tree(jnp.minimum, [jnp.where(t == m, k, PEER_NKEYS)
                                                                 for t, k in zip(tiles, key_id)]))
            problems[i] = [jnp.where(k == idx, NEG_INF, t) for t, k in zip(tiles, key_id)]
            out[i].append((m, idx))
    return out


def _rows_of(ranked, sub, which, pick):
    t = ranked[pick(0)][which]
    for r in range(1, SUBLANES):
        t = jnp.where(sub == r, ranked[pick(r)][which], t)
    return t


def _product_key_topk(pairs, sub):
    def tiles(first, second, which):
        lo = _rows_of(second, sub, which, lambda r: r)
        hi = _rows_of(second, sub, which, lambda r: SUBLANES + r)
        quad = _rows_of(second, sub, which, lambda r: r % 4)
        a45 = _rows_of(first, sub, which, lambda r: 4 + r // 4)
        a67 = _rows_of(first, sub, which, lambda r: 6 + r // 4)
        ahi = _rows_of(first, sub, which, lambda r: SUBLANES + r)
        a = [first[k][which] for k in range(4)]
        return (a[0], lo), (a[0], hi), (a[1], lo), (a[2], lo), (a[3], lo), (a45, quad), (a67, quad), (ahi, second[0][which])

    cands = [[x + y for x, y in tiles(f, s, 0)] for f, s in pairs]
    cidxs = [[x * PEER_NKEYS + y for x, y in tiles(f, s, 1)] for f, s in pairs]
    quad_pos = jnp.where(sub < 4, sub, sub + (PEER_TOPK - 4))
    pos = [sub, sub + 8, sub + 16, sub + 32, sub + 48, quad_pos + 64, quad_pos + 96, (sub + 8) * PEER_TOPK]
    out = [[] for _ in pairs]
    for _ in range(PEER_TOPK):
        for i, (cand, cidx) in enumerate(zip(cands, cidxs)):
            m = _all_sublanes(jnp.maximum, _tree(jnp.maximum, cand))
            psel = _all_sublanes(jnp.minimum, _tree(jnp.minimum, [jnp.where(c == m, p, PEER_TOPK * PEER_TOPK)
                                                                  for c, p in zip(cand, pos)]))
            hits = [p == psel for p in pos]
            e = _all_sublanes(jnp.maximum, _tree(jnp.maximum, [jnp.where(hh, x, -1) for hh, x in zip(hits, cidx)]))
            cands[i] = [jnp.where(hh, NEG_INF, c) for hh, c in zip(hits, cand)]
            out[i].append((m, e))
    return out


def _peer_score_kernel(x_ref, g_ref, sh_ref, sc_ref, wq_ref, keys_ref, h_ref, eidx_ref, gate_ref,
                       q_scr, s_scr, e_scr, p_scr):
    tm = x_ref.shape[1]
    nt = (((1,), (1,)), ((), ()))
    h = _norm_mod(x_ref[0], g_ref[...], sh_ref[0], sc_ref[0])
    h_ref[0] = h
    q = jnp.dot(h.astype(BF16), wq_ref[...], preferred_element_type=F32)
    for c in range(2 * PEER_HEADS):
        q_scr[c] = q[:, c * PEER_DK_HALF:(c + 1) * PEER_DK_HALF].astype(BF16)
    sub = lax.broadcasted_iota(jnp.int32, (SUBLANES, LANES), 0)

    def head_step(hd, carry):
        for p in range(2):
            s_scr[p] = lax.dot_general(keys_ref[hd, p], q_scr[2 * hd + p], nt, preferred_element_type=F32)
        row0 = pl.multiple_of(hd * PEER_TOPK, PEER_TOPK)
        groups = tm // LANES
        cols = [slice(lg * LANES, (lg + 1) * LANES) for lg in range(groups)]
        ranked = _top_keys([[s_scr[p, v * SUBLANES:(v + 1) * SUBLANES, cols[lg]] for v in range(KEY_TILES)]
                            for lg in range(groups) for p in range(2)], sub)
        tops = _product_key_topk([(ranked[2 * lg], ranked[2 * lg + 1]) for lg in range(groups)], sub)
        for lg, top in enumerate(tops):
            ex = [(jnp.exp(m - top[0][0]), e) for m, e in top]
            denom = _tree(jnp.add, [v for v, _ in ex])
            for half in range(2):
                pick = lambda r, half=half: half * SUBLANES + r
                rows = pl.ds(row0 + half * SUBLANES, SUBLANES)
                e_scr[rows, cols[lg]] = _rows_of(ex, sub, 1, pick)
                p_scr[rows, cols[lg]] = _rows_of(ex, sub, 0, pick) / denom
        return carry

    lax.fori_loop(0, PEER_HEADS, head_step, 0)
    eidx_ref[0] = e_scr[...].T
    gate_ref[0] = p_scr[...].T


def _peer_score(x, g, sh, sc, wq, keys):
    b, s, d = x.shape
    tm = PEER_TM
    n = wq.shape[1]
    tile = lambda k: pl.BlockSpec((1, tm, k), lambda bi, i: (bi, i, 0))
    return pl.pallas_call(
        _peer_score_kernel,
        grid=(b, s // tm),
        in_specs=[tile(d),
                  pl.BlockSpec((1, d), lambda bi, i: (0, 0)),
                  _mod_spec(sh), _mod_spec(sc),
                  pl.BlockSpec((d, n), lambda bi, i: (0, 0)),
                  pl.BlockSpec(keys.shape, lambda bi, i: (0, 0, 0, 0))],
        out_specs=[tile(d), tile(PEER_SEL), tile(PEER_SEL)],
        out_shape=[jax.ShapeDtypeStruct((b, s, d), F32),
                   jax.ShapeDtypeStruct((b, s, PEER_SEL), jnp.int32),
                   jax.ShapeDtypeStruct((b, s, PEER_SEL), F32)],
        scratch_shapes=[pltpu.VMEM((2 * PEER_HEADS, tm, PEER_DK_HALF), BF16),
                        pltpu.VMEM((2, PEER_NKEYS, tm), F32),
                        pltpu.VMEM((PEER_SEL, tm), jnp.int32),
                        pltpu.VMEM((PEER_SEL, tm), F32)],
        compiler_params=_cparams(2),
        name="peer_score",
    )(x, g.reshape(1, d), sh, sc, wq, keys)


HALF_ROWS = SUBLANES // 2


PACK_EB = 256
DOWN_WHOLE_WORD = True
UP_WHOLE_WORD = False


def _pack_kernel(x_ref, o_ref, *, whole_word):
    x_ref = x_ref.at[0]
    half = x_ref.shape[1] // 2
    is_tail = pl.program_id(0) == pl.num_programs(0) - 1
    row = lax.broadcasted_iota(jnp.int32, (PACK_EB, LANES), 0)
    for c in range(HALF_ROWS):
        lo = x_ref[:, half + c * LANES:half + (c + 1) * LANES].astype(BF16).astype(F32)
        lo_bits = pltpu.bitcast(lo, jnp.int32) >> 16 & 0xFFFF
        hi = x_ref[:, c * LANES:(c + 1) * LANES]
        if whole_word:
            bits = pltpu.bitcast(hi, jnp.int32)
            excess = lo_bits - (bits & 0xFFFF)
            step = jnp.where(excess > 0x8000, -1, jnp.where(excess < -0x8000, 1, 0))
            step = jnp.where((bits & 0x7FFF0000) == 0, jnp.maximum(step, 0), step)
            hi_bits = (bits & jnp.int32(-0x10000)) + (step << 16)
        else:
            hi_bits = pltpu.bitcast(hi.astype(BF16).astype(F32), jnp.int32)
        words = pltpu.bitcast(hi_bits | lo_bits, jnp.uint32)
        tail = jnp.where(row == 1, pltpu.roll(words, 1, axis=0), jnp.uint32(0))
        o_ref[pl.ds(c, PACK_EB, stride=HALF_ROWS), :] = jnp.where(is_tail, tail, words)


def _pack_table(tables, layer, whole_word):
    _, e, d = tables.shape
    assert d // 2 == HALF_ROWS * LANES and e % PACK_EB == 0
    nb = e // PACK_EB
    return pl.pallas_call(
        functools.partial(_pack_kernel, whole_word=whole_word),
        grid=(nb + 1,),
        in_specs=[pl.BlockSpec((1, PACK_EB, d), lambda i: (layer, i % nb, 0))],
        out_specs=pl.BlockSpec((PACK_EB * HALF_ROWS, LANES), lambda i: (i, 0)),
        out_shape=jax.ShapeDtypeStruct(((e + PACK_EB) * HALF_ROWS, LANES), jnp.uint32),
        compiler_params=_cparams(1),
        name="pack_table",
    )(tables)


def _load_rows(eidx, on_low_sublanes, n_experts):
    high_start = jnp.where(eidx == 0, HALF_ROWS * n_experts, HALF_ROWS * (eidx - 1))
    return jnp.where(on_low_sublanes, HALF_ROWS * eidx, high_start)


def _unpack(words, whole_word):
    first = words if whole_word else words & jnp.uint32(0xFFFF0000)
    return (pltpu.bitcast(first, F32), pltpu.bitcast(words << 16, F32))


_FOLD_POS = (6, 2, 4, 0, 7, 3, 5, 1)


MERGED = SUBLANES // 2


def _merge_halves(p, sub):
    return [jnp.where(sub < HALF_ROWS, p[2 * k], p[2 * k + 1]) for k in range(MERGED)]


def _fold_rows(m, sub):
    n = [jnp.where((sub & 2) != 0, m[2 * k] + pltpu.roll(m[2 * k], 2, axis=0),
                   m[2 * k + 1] + pltpu.roll(m[2 * k + 1], 6, axis=0)) for k in range(2)]
    return jnp.where((sub & 1) != 0, n[0] + pltpu.roll(n[0], 1, axis=0), n[1] + pltpu.roll(n[1], 7, axis=0))


DOWN_TM = 256
DOWN_BLOCK = 16


def _token_tiles(h_ref, t0, sub):
    chunks = [h_ref[pl.ds(t0, SUBLANES), r * LANES:(r + 1) * LANES] for r in range(SUBLANES)]
    tiles = []
    for s in range(SUBLANES):
        tile = None
        for r in range(SUBLANES):
            piece = jnp.broadcast_to(chunks[r][s:s + 1, :], (SUBLANES, LANES))
            tile = piece if tile is None else jnp.where(sub == r, piece, tile)
        tiles.append(tile)
    return tiles


def _peer_down_kernel(*refs):
    row_refs = refs[:SUBLANES]
    h_ref, gate_ref, tbl_ref, o_ref, pend_even, pend_odd = refs[SUBLANES:]
    pend_scr = (pend_even, pend_odd)
    tm = h_ref.shape[0]
    n_blocks = PEER_SEL // DOWN_BLOCK
    assert n_blocks % 2 == 0
    lane = lax.broadcasted_iota(jnp.int32, (SUBLANES, LANES), 1)
    sub = lax.broadcasted_iota(jnp.int32, (SUBLANES, LANES), 0)

    def group(gi, carry):
        t0 = pl.multiple_of(gi * SUBLANES, SUBLANES)
        hs = []
        for s, tile in enumerate(_token_tiles(h_ref, t0, sub)):
            swapped = pltpu.roll(tile, HALF_ROWS, axis=0)
            hs.append((tile, swapped) if _FOLD_POS[s] % 2 == 0 else (swapped, tile))
        gbase = gi * PEER_SEL

        def stash(slot, j0):
            for k in range(DOWN_BLOCK):
                prods = [None] * SUBLANES
                for s in range(SUBLANES):
                    hi, lo = _unpack(tbl_ref[pl.ds(row_refs[s][gbase + j0 + k], SUBLANES), :], DOWN_WHOLE_WORD)
                    prods[_FOLD_POS[s]] = hi * hs[s][0] + lo * hs[s][1]
                for q, tile in enumerate(_merge_halves(prods, sub)):
                    pend_scr[slot][MERGED * k + q] = tile

        def place(accs, slot, j0):
            accs = list(accs)
            for k in range(DOWN_BLOCK):
                for q in range(MERGED):
                    tot = jnp.sum(pend_scr[slot][MERGED * k + q], axis=-1, keepdims=True)
                    accs[q] = jnp.where(lane == j0 + k, tot, accs[q])
            return tuple(accs)

        def block_pair(i, accs):
            j0 = 2 * i * DOWN_BLOCK
            accs = place(accs, 0, j0)
            stash(1, j0 + DOWN_BLOCK)
            accs = place(accs, 1, j0 + DOWN_BLOCK)
            stash(0, j0 + 2 * DOWN_BLOCK)
            return accs

        stash(0, 0)
        accs = lax.fori_loop(0, n_blocks // 2 - 1, block_pair, (jnp.zeros((SUBLANES, LANES), F32),) * MERGED)
        accs = place(accs, 0, PEER_SEL - 2 * DOWN_BLOCK)
        stash(1, PEER_SEL - DOWN_BLOCK)
        accs = place(accs, 1, PEER_SEL - DOWN_BLOCK)
        act = _fold_rows(accs, sub)
        gelu = act * (lax.erf(act / np.sqrt(2).astype(np.float32)) + 1.0) / 2.0
        o_ref[pl.ds(t0, SUBLANES), :] = gate_ref[pl.ds(t0, SUBLANES), :] * gelu
        return carry

    lax.fori_loop(0, tm // SUBLANES, group, 0)


def _peer_down(eidx, h, gate, tbl):
    t, d = h.shape
    tm = DOWN_TM
    low = (jnp.asarray(_FOLD_POS, jnp.int32) % 2 == 0)[None, :, None]
    rows = _load_rows(eidx.reshape(t // SUBLANES, SUBLANES, PEER_SEL), low, tbl.shape[0] // HALF_ROWS - PACK_EB)
    rows = rows.transpose(1, 0, 2).reshape(SUBLANES, -1)
    smem = pl.BlockSpec((tm // SUBLANES * PEER_SEL,), lambda i: (i,), memory_space=pltpu.SMEM)
    return pl.pallas_call(
        _peer_down_kernel,
        grid=(t // tm,),
        in_specs=[smem] * SUBLANES + [
            pl.BlockSpec((tm, d), lambda i: (i, 0)),
            pl.BlockSpec((tm, PEER_SEL), lambda i: (i, 0)),
            pl.BlockSpec(tbl.shape, lambda i: (0, 0), pipeline_mode=pl.Buffered(1))],
        out_specs=pl.BlockSpec((tm, PEER_SEL), lambda i: (i, 0)),
        out_shape=jax.ShapeDtypeStruct((t, PEER_SEL), F32),
        scratch_shapes=[pltpu.VMEM((DOWN_BLOCK * MERGED, SUBLANES, LANES), F32)] * 2,
        compiler_params=_cparams(1),
        name="peer_down",
    )(*[rows[s] for s in range(SUBLANES)], h, gate, tbl)


UP_TM = 128
UP_BLOCK = 16
UP_SPREAD = SUBLANES * UP_BLOCK // PEER_SEL


def _peer_up_kernel(*refs):
    row_refs = refs[:SUBLANES]
    w_ref, x_ref, gate_ref, tbl_ref, o_ref, stage_scr, wt_scr, wv_even, wv_odd = refs[SUBLANES:]
    wv_scr = (wv_even, wv_odd)
    tm, d = o_ref.shape
    n_groups = tm // SUBLANES
    sub = lax.broadcasted_iota(jnp.int32, (SUBLANES, LANES), 0)
    low = sub < HALF_ROWS
    wt_scr[...] = w_ref[...].T

    def spread(t, slot, s):
        lanes = jnp.full((PEER_SEL, LANES), t, jnp.int32)
        wv_scr[slot][s] = jnp.take_along_axis(wt_scr[...], lanes, axis=1, mode="promise_in_bounds")

    def group(g, cur):
        t0 = pl.multiple_of(g * SUBLANES, SUBLANES)
        next_t0 = jnp.minimum(g + 1, n_groups - 1) * SUBLANES
        gbase = g * PEER_SEL

        def block(jb, accs):
            accs = list(accs)
            j0 = jb * UP_BLOCK
            for u in range(UP_SPREAD):
                s_next = jb * UP_SPREAD + u
                spread(next_t0 + s_next, 1 - cur, s_next)
            wv_rows = [wv_scr[cur].at[s, pl.ds(pl.multiple_of(j0, UP_BLOCK), UP_BLOCK)] for s in range(SUBLANES)]
            for k in range(0, UP_BLOCK, 2):
                for s in range(SUBLANES):
                    words = jnp.where(low, tbl_ref[pl.ds(row_refs[s][gbase + j0 + k], SUBLANES), :],
                                      tbl_ref[pl.ds(row_refs[s][gbase + j0 + k + 1], SUBLANES), :])
                    wv = jnp.where(low, jnp.broadcast_to(wv_rows[s][k:k + 1, :], (SUBLANES, LANES)),
                                   jnp.broadcast_to(wv_rows[s][k + 1:k + 2, :], (SUBLANES, LANES)))
                    hi, lo = _unpack(words, UP_WHOLE_WORD)
                    accs[2 * s] = accs[2 * s] + wv * hi
                    accs[2 * s + 1] = accs[2 * s + 1] + wv * lo
            return tuple(accs)

        zero = jnp.zeros((SUBLANES, LANES), F32)
        accs = lax.fori_loop(0, PEER_SEL // UP_BLOCK, block, (zero,) * (2 * SUBLANES))
        for s in range(SUBLANES):
            hi, lo = accs[2 * s], accs[2 * s + 1]
            stage_scr[s * SUBLANES:(s + 1) * SUBLANES, :] = jnp.where(
                low, hi + pltpu.roll(hi, HALF_ROWS, axis=0), lo + pltpu.roll(lo, HALF_ROWS, axis=0))
        for r in range(SUBLANES):
            cols = slice(r * LANES, (r + 1) * LANES)
            o_ref[pl.ds(t0, SUBLANES), cols] = (x_ref[pl.ds(t0, SUBLANES), cols]
                                                + gate_ref[0, :, cols] * stage_scr[pl.ds(r, SUBLANES, stride=SUBLANES), :])

    def group_pair(gp, carry):
        group(2 * gp, 0)
        group(2 * gp + 1, 1)
        return carry

    for s in range(SUBLANES):
        spread(s, 0, s)
    lax.fori_loop(0, n_groups // 2, group_pair, 0)


def _peer_up(eidx, w, tbl, x, gate):
    b, s, d = x.shape
    t = b * s
    tm = UP_TM
    assert s % tm == 0 and tm == LANES
    rows = _load_rows(eidx, (jnp.arange(PEER_SEL) % 2 == 0)[None, :], tbl.shape[0] // HALF_ROWS - PACK_EB)
    rows = rows.reshape(t // SUBLANES, SUBLANES, PEER_SEL).transpose(1, 0, 2).reshape(SUBLANES, -1)
    smem = pl.BlockSpec((tm // SUBLANES * PEER_SEL,), lambda i: (i,), memory_space=pltpu.SMEM)
    tiles_per_batch = s // tm
    out = pl.pallas_call(
        _peer_up_kernel,
        grid=(t // tm,),
        in_specs=[smem] * SUBLANES + [
            pl.BlockSpec((tm, PEER_SEL), lambda i: (i, 0)),
            pl.BlockSpec((tm, d), lambda i: (i, 0)),
            pl.BlockSpec((1, 1, d), lambda i: (i // tiles_per_batch, 0, 0)),
            pl.BlockSpec(tbl.shape, lambda i: (0, 0), pipeline_mode=pl.Buffered(1))],
        out_specs=pl.BlockSpec((tm, d), lambda i: (i, 0)),
        out_shape=jax.ShapeDtypeStruct((t, d), F32),
        scratch_shapes=[pltpu.VMEM((SUBLANES * SUBLANES, LANES), F32),
                        pltpu.VMEM((PEER_SEL, tm), F32),
                        pltpu.VMEM((SUBLANES, PEER_SEL, LANES), F32), pltpu.VMEM((SUBLANES, PEER_SEL, LANES), F32)],
        compiler_params=_cparams(1),
        name="peer_up",
    )(*[rows[u] for u in range(SUBLANES)], w, x.reshape(t, d), gate, tbl)
    return out.reshape(b, s, d)


def _peer_block(x, g, sh, sc, gate2, wq, keys, down, up, layer):
    b, s, d = x.shape
    t = b * s
    h, eidx, gsm = _peer_score(x, g, sh, sc, wq.astype(BF16), keys.astype(BF16))
    eidx = eidx.reshape(t, PEER_SEL)
    w = _peer_down(eidx, h.reshape(t, d), gsm.reshape(t, PEER_SEL), _pack_table(down, layer, DOWN_WHOLE_WORD))
    return _peer_up(eidx, w, _pack_table(up, layer, UP_WHOLE_WORD), x, gate2)


def _rmsnorm_kernel(x_ref, g_ref, o_ref):
    x = x_ref[0]
    o_ref[0] = (x * lax.rsqrt(jnp.mean(x * x, axis=-1, keepdims=True) + EPS)) * g_ref[...]


def _rmsnorm(x, g, tm):
    b, s, d = x.shape
    tile = pl.BlockSpec((1, tm, d), lambda bi, i: (bi, i, 0))
    return pl.pallas_call(
        _rmsnorm_kernel,
        grid=(b, s // tm),
        in_specs=[tile, pl.BlockSpec((1, d), lambda bi, i: (0, 0))],
        out_specs=tile,
        out_shape=jax.ShapeDtypeStruct((b, s, d), F32),
        compiler_params=_cparams(2),
        name="final_rmsnorm",
    )(x, g.reshape(1, d))


def kernel(x, c, ctx, c_ctx, ada_w, ada_b, norm1_g, norm2_g, final_g, ab_w_in, ab_w_out, na_rpb, fn_w,
           cv_w_in, cv_w, cv_w_out, peer_w_q, peer_keys, peer_down, peer_up):
    b, s, d = x.shape
    depth = ada_w.shape[0]
    rows = -(-(b + 1) // SUBLANES) * SUBLANES
    cc = jnp.concatenate([c, c_ctx[None], jnp.zeros((rows - b - 1, d), F32)], axis=0)
    mod = _ada_vectors(cc, ada_w, ada_b)

    def chunks(i, lo, hi):
        m = mod[i, lo:hi].reshape(hi - lo, 1, 6, d)
        return [m[:, :, k] for k in range(6)]

    for i in range(depth):
        sh1, sc1, g1, sh2, sc2, g2 = chunks(i, 0, b)
        if i % 2 == 0:
            e = i // 2
            csh1, csc1 = chunks(i, b, b + 1)[:2]
            w_in = ab_w_in[e].astype(BF16)
            qkv, f = _norm_mod_matmul(x, norm1_g[i], sh1, sc1, w_in, (3 * NA_WIDTH, FN_WIDTH), (BF16, BF16),
                                      512, "ab_in")
            (ctx_kv,) = _norm_mod_matmul(ctx, norm1_g[i], csh1, csc1, w_in[:, NA_WIDTH:3 * NA_WIDTH],
                                         (2 * NA_WIDTH,), (BF16,), ctx.shape[1], "ab_in_ctx")
            a = _attention(qkv, ctx_kv, _attention_bias(na_rpb[e]))
            fm = _fnet(f, fn_w[e])
            x = _proj_residual(a, fm, x, g1, ab_w_out[e].astype(BF16), 512)
        else:
            o = i // 2
            bg, u = _conv_in(x, norm1_g[i], sh1, sc1, cv_w_in[o].astype(BF16), 256)
            x = _conv_out(bg, u, cv_w[o], x, g1, cv_w_out[o].astype(BF16), 512)
        x = _peer_block(x, norm2_g[i], sh2, sc2, g2, peer_w_q[i], peer_keys[i], peer_down, peer_up, i)
    return _rmsnorm(x, final_g, 512)
```

```python
import functools

import jax
import jax.numpy as jnp
import numpy as np
from jax import lax
from jax.experimental import pallas as pl
from jax.experimental.pallas import tpu as pltpu

F32 = jnp.float32
BF16 = jnp.bfloat16
EPS = 1e-6

GRID_W = 64
HEAD_DIM = 64
NA_HEADS = 8
NA_WIDTH = NA_HEADS * HEAD_DIM
KH = 8
KH_MAX = 8
KW = 16
FN_GROUPS = 8
FN_GROUP_DIM = 64
FN_WIDTH = FN_GROUPS * FN_GROUP_DIM
PEER_HEADS = 8
PEER_NKEYS = 128
PEER_DK_HALF = 128
PEER_TOPK = 16
PEER_SEL = PEER_HEADS * PEER_TOPK
LANES = 128
SUBLANES = 8
VMEM_LIMIT = 56 * 1024 * 1024

NEG_INF = float("-inf")


def _cparams(n_axes):
    return pltpu.CompilerParams(dimension_semantics=("arbitrary",) * n_axes, vmem_limit_bytes=VMEM_LIMIT)


def _norm_mod(x, g, sh, sc):
    y = x * lax.rsqrt(jnp.mean(x * x, axis=-1, keepdims=True) + EPS)
    return (y * g) * (1.0 + sc) + sh


def _ada_kernel(c_ref, w_ref, b_ref, o_ref):
    c = c_ref[...]
    s = c / (1.0 + jnp.exp(-c))
    o_ref[0] = jnp.dot(s, w_ref[0], preferred_element_type=F32, precision=lax.Precision.HIGHEST) + b_ref[0]


def _ada_vectors(cc, ada_w, ada_b):
    depth, d, n = ada_w.shape
    rows = cc.shape[0]
    tn = 1536
    return pl.pallas_call(
        _ada_kernel,
        grid=(depth, n // tn),
        in_specs=[pl.BlockSpec((rows, d), lambda i, j: (0, 0)),
                  pl.BlockSpec((1, d, tn), lambda i, j: (i, 0, j)),
                  pl.BlockSpec((1, 1, tn), lambda i, j: (i, 0, j))],
        out_specs=pl.BlockSpec((1, rows, tn), lambda i, j: (i, 0, j)),
        out_shape=jax.ShapeDtypeStruct((depth, rows, n), F32),
        compiler_params=_cparams(2),
        name="ada_vectors",
    )(cc, ada_w, ada_b.reshape(depth, 1, n))


def _nmm_kernel(x_ref, g_ref, sh_ref, sc_ref, w_ref, *o_refs):
    h = _norm_mod(x_ref[0], g_ref[...], sh_ref[0], sc_ref[0])
    acc = jnp.dot(h.astype(BF16), w_ref[...], preferred_element_type=F32)
    off = 0
    for o_ref in o_refs:
        n = o_ref.shape[-1]
        o_ref[0] = acc[:, off:off + n].astype(o_ref.dtype)
        off += n


def _mod_spec(arr):
    d = arr.shape[-1]
    if arr.shape[0] == 1:
        return pl.BlockSpec((1, 1, d), lambda b, i: (0, 0, 0))
    return pl.BlockSpec((1, 1, d), lambda b, i: (b, 0, 0))


def _norm_mod_matmul(x, g, sh, sc, w, splits, dtypes, tm, name):
    b, s, d = x.shape
    n = w.shape[1]
    assert sum(splits) == n
    return pl.pallas_call(
        _nmm_kernel,
        grid=(b, s // tm),
        in_specs=[pl.BlockSpec((1, tm, d), lambda bi, i: (bi, i, 0)),
                  pl.BlockSpec((1, d), lambda bi, i: (0, 0)),
                  _mod_spec(sh), _mod_spec(sc),
                  pl.BlockSpec((d, n), lambda bi, i: (0, 0))],
        out_specs=[pl.BlockSpec((1, tm, k), lambda bi, i: (bi, i, 0)) for k in splits],
        out_shape=[jax.ShapeDtypeStruct((b, s, k), dt) for k, dt in zip(splits, dtypes)],
        compiler_params=_cparams(2),
        name=name,
    )(x, g.reshape(1, d), sh, sc, w)


def _attn_kernel(qkv_ref, ctx_ref, bias_ref, o_ref):
    rows = qkv_ref.shape[1] // GRID_W
    nt = (((1,), (1,)), ((), ()))
    scale = HEAD_DIM ** -0.5

    def row_step(r, carry):
        start = jnp.clip(r - KH // 2, 0, rows - KH)
        off = start - r + (KH_MAX - 1)
        q0 = pl.multiple_of(r * GRID_W, GRID_W)
        k0 = pl.multiple_of(start * GRID_W, GRID_W)
        heads = range(NA_HEADS)
        cols = [slice(h * HEAD_DIM, (h + 1) * HEAD_DIM) for h in heads]
        kcols = [slice(NA_WIDTH + h * HEAD_DIM, NA_WIDTH + (h + 1) * HEAD_DIM) for h in heads]
        vcols = [slice(2 * NA_WIDTH + h * HEAD_DIM, 2 * NA_WIDTH + (h + 1) * HEAD_DIM) for h in heads]
        q = [qkv_ref[0, pl.ds(q0, GRID_W), cols[h]] for h in heads]
        s_win = [lax.dot_general(q[h], qkv_ref[0, pl.ds(k0, KH * GRID_W), kcols[h]], nt,
                                 preferred_element_type=F32) * scale + bias_ref[h, off] for h in heads]
        s_ctx = [lax.dot_general(q[h], ctx_ref[0, :, cols[h]], nt, preferred_element_type=F32) * scale
                 for h in heads]
        m = [jnp.maximum(jnp.max(s_win[h], axis=-1, keepdims=True), jnp.max(s_ctx[h], axis=-1, keepdims=True))
             for h in heads]
        p_win = [jnp.exp(s_win[h] - m[h]) for h in heads]
        p_ctx = [jnp.exp(s_ctx[h] - m[h]) for h in heads]
        l = [jnp.sum(p_win[h], axis=-1, keepdims=True) + jnp.sum(p_ctx[h], axis=-1, keepdims=True) for h in heads]
        o = [jnp.dot(p_win[h].astype(BF16), qkv_ref[0, pl.ds(k0, KH * GRID_W), vcols[h]], preferred_element_type=F32)
             + jnp.dot(p_ctx[h].astype(BF16), ctx_ref[0, :, kcols[h]], preferred_element_type=F32) for h in heads]
        outs = [o[h] / l[h] for h in heads]
        o_ref[0, pl.ds(q0, GRID_W), :] = jnp.concatenate(outs, axis=-1).astype(o_ref.dtype)
        return carry

    lax.fori_loop(0, rows, row_step, 0)


def _attention_bias(rpb):
    cols = np.arange(GRID_W)
    col_start = np.clip(cols - KW // 2, 0, GRID_W - KW)
    in_win = (cols[None, :] >= col_start[:, None]) & (cols[None, :] < col_start[:, None] + KW)
    col_off = np.clip(cols[None, :] - cols[:, None] + KW - 1, 0, 2 * KW - 2)
    onehot = jnp.asarray(col_off[None] == np.arange(2 * KW - 1)[:, None, None], F32)
    by_row = jnp.sum(rpb.astype(F32)[:, :, :, None, None] * onehot[None, None], axis=2)
    by_row = jnp.where(in_win[None, None], by_row, NEG_INF)
    b = jnp.stack([by_row[:, o:o + KH] for o in range(KH)], axis=1)
    return b.transpose(0, 1, 3, 2, 4).reshape(NA_HEADS, KH, GRID_W, KH * GRID_W)


def _attention(qkv, ctx_kv, bias):
    b, s, _ = qkv.shape
    l = ctx_kv.shape[1]
    return pl.pallas_call(
        _attn_kernel,
        grid=(b,),
        in_specs=[pl.BlockSpec((1, s, 3 * NA_WIDTH), lambda i: (i, 0, 0)),
                  pl.BlockSpec((1, l, 2 * NA_WIDTH), lambda i: (i, 0, 0)),
                  pl.BlockSpec(bias.shape, lambda i: (0, 0, 0, 0), pipeline_mode=pl.Buffered(1))],
        out_specs=pl.BlockSpec((1, s, NA_WIDTH), lambda i: (i, 0, 0)),
        out_shape=jax.ShapeDtypeStruct((b, s, NA_WIDTH), BF16),
        compiler_params=_cparams(1),
        name="nbr_attention",
    )(qkv, ctx_kv, bias)


FN_ROW_CHUNK = 256


def _fnet_kernel(f_ref, cn_ref, sn_ref, c64_ref, s64_ref, wbd_ref, o_ref, a_scr, b_scr, xa_scr, xb_scr):
    n = f_ref.shape[1]
    scale = (n * FN_GROUP_DIM) ** -0.5

    @pl.when(pl.program_id(0) == 0)
    def _():
        hi = lax.Precision.HIGHEST
        a_scr[...] = (jnp.dot(c64_ref[...], wbd_ref[...], preferred_element_type=F32, precision=hi) * scale).astype(BF16)
        b_scr[...] = (jnp.dot(s64_ref[...], wbd_ref[...], preferred_element_type=F32, precision=hi) * -scale).astype(BF16)

    x = f_ref[0]
    xa_scr[...] = jnp.dot(x, a_scr[...], preferred_element_type=F32).astype(BF16)
    xb_scr[...] = jnp.dot(x, b_scr[...], preferred_element_type=F32).astype(BF16)

    def chunk(i, carry):
        r0 = pl.multiple_of(i * FN_ROW_CHUNK, FN_ROW_CHUNK)
        y = (jnp.dot(cn_ref[pl.ds(r0, FN_ROW_CHUNK), :], xa_scr[...], preferred_element_type=F32)
             + jnp.dot(sn_ref[pl.ds(r0, FN_ROW_CHUNK), :], xb_scr[...], preferred_element_type=F32))
        o_ref[0, pl.ds(r0, FN_ROW_CHUNK), :] = y.astype(o_ref.dtype)
        return carry

    lax.fori_loop(0, n // FN_ROW_CHUNK, chunk, 0)


def _dft_tables(n):
    k = (np.arange(n)[:, None] * np.arange(n)[None, :]) % n
    ang = 2.0 * np.pi * k.astype(np.float64) / n
    return np.cos(ang), np.sin(ang)


def _fnet(f, fn_w):
    b, s, _ = f.shape
    cn, sn = _dft_tables(s)
    c64, s64 = _dft_tables(FN_GROUP_DIM)
    eye = np.eye(FN_GROUPS)
    c64bd = jnp.asarray(np.kron(eye, c64), F32)
    s64bd = jnp.asarray(np.kron(eye, s64), F32)
    wbd = (jnp.asarray(eye, F32)[:, None, :, None] * fn_w[:, :, None, :]).reshape(FN_WIDTH, FN_WIDTH)
    const = lambda shape: pl.BlockSpec(shape, lambda i: (0, 0), pipeline_mode=pl.Buffered(1))
    return pl.pallas_call(
        _fnet_kernel,
        grid=(b,),
        in_specs=[pl.BlockSpec((1, s, FN_WIDTH), lambda i: (i, 0, 0)),
                  const((s, s)), const((s, s)),
                  const((FN_WIDTH, FN_WIDTH)), const((FN_WIDTH, FN_WIDTH)), const((FN_WIDTH, FN_WIDTH))],
        out_specs=pl.BlockSpec((1, s, FN_WIDTH), lambda i: (i, 0, 0)),
        out_shape=jax.ShapeDtypeStruct((b, s, FN_WIDTH), BF16),
        scratch_shapes=[pltpu.VMEM((FN_WIDTH, FN_WIDTH), BF16), pltpu.VMEM((FN_WIDTH, FN_WIDTH), BF16),
                        pltpu.VMEM((s, FN_WIDTH), BF16), pltpu.VMEM((s, FN_WIDTH), BF16)],
        compiler_params=_cparams(1),
        name="fnet_mix",
    )(f, jnp.asarray(cn, BF16), jnp.asarray(sn, BF16), c64bd, s64bd, wbd)


def _proj_res_kernel(a_ref, f_ref, x_ref, g_ref, w_ref, o_ref):
    ka = a_ref.shape[-1]
    out = (jnp.dot(a_ref[0], w_ref[:ka, :], preferred_element_type=F32)
           + jnp.dot(f_ref[0], w_ref[ka:, :], preferred_element_type=F32))
    o_ref[0] = x_ref[0] + g_ref[0] * out


def _proj_residual(a, f, x, gate, w, tm):
    b, s, d = x.shape
    ka, kf = a.shape[-1], f.shape[-1]
    return pl.pallas_call(
        _proj_res_kernel,
        grid=(b, s // tm),
        in_specs=[pl.BlockSpec((1, tm, ka), lambda bi, i: (bi, i, 0)),
                  pl.BlockSpec((1, tm, kf), lambda bi, i: (bi, i, 0)),
                  pl.BlockSpec((1, tm, d), lambda bi, i: (bi, i, 0)),
                  _mod_spec(gate),
                  pl.BlockSpec((ka + kf, d), lambda bi, i: (0, 0))],
        out_specs=pl.BlockSpec((1, tm, d), lambda bi, i: (bi, i, 0)),
        out_shape=jax.ShapeDtypeStruct((b, s, d), F32),
        compiler_params=_cparams(2),
        name="proj_residual",
    )(a, f, x, gate, w)


def _conv_in_kernel(x_ref, g_ref, sh_ref, sc_ref, w_ref, bg_ref, u_ref):
    d = bg_ref.shape[-1]
    h = _norm_mod(x_ref[0], g_ref[...], sh_ref[0], sc_ref[0])
    acc = jnp.dot(h.astype(BF16), w_ref[...], preferred_element_type=F32)
    bg_ref[0] = acc[:, :d]
    u_ref[0] = acc[:, d:2 * d] * acc[:, 2 * d:]


def _conv_in(x, g, sh, sc, w, tm):
    b, s, d = x.shape
    n = w.shape[1]
    return pl.pallas_call(
        _conv_in_kernel,
        grid=(b, s // tm),
        in_specs=[pl.BlockSpec((1, tm, d), lambda bi, i: (bi, i, 0)),
                  pl.BlockSpec((1, d), lambda bi, i: (0, 0)),
                  _mod_spec(sh), _mod_spec(sc),
                  pl.BlockSpec((d, n), lambda bi, i: (0, 0))],
        out_specs=[pl.BlockSpec((1, tm, d), lambda bi, i: (bi, i, 0))] * 2,
        out_shape=[jax.ShapeDtypeStruct((b, s, d), F32)] * 2,
        compiler_params=_cparams(2),
        name="conv_in",
    )(x, g.reshape(1, d), sh, sc, w)


def _conv_out_kernel(bg_ref, u_ref, up_ref, un_ref, cw_ref, x_ref, g_ref, w_ref, o_ref):
    i = pl.program_id(1)
    last = pl.num_programs(1) - 1
    u = u_ref[0]
    tm = u.shape[0]
    row = lax.broadcasted_iota(jnp.int32, u.shape, 0)
    prev_row = jnp.where(i == 0, 0.0, up_ref[0, SUBLANES - 1:SUBLANES, :])
    next_row = jnp.where(i == last, 0.0, un_ref[0, 0:1, :])
    u_prev = jnp.where(row == 0, prev_row, pltpu.roll(u, 1, axis=0))
    u_next = jnp.where(row == tm - 1, next_row, pltpu.roll(u, tm - 1, axis=0))
    y = cw_ref[0:1, :] * u_prev + cw_ref[1:2, :] * u + cw_ref[2:3, :] * u_next
    z = (bg_ref[0] * y).astype(BF16)
    o_ref[0] = x_ref[0] + g_ref[0] * jnp.dot(z, w_ref[...], preferred_element_type=F32)


def _conv_out(bg, u, cw, x, gate, w, tm):
    b, s, d = x.shape
    hb = tm // SUBLANES
    nhb = s // SUBLANES
    tile = pl.BlockSpec((1, tm, d), lambda bi, i: (bi, i, 0))
    return pl.pallas_call(
        _conv_out_kernel,
        grid=(b, s // tm),
        in_specs=[tile, tile,
                  pl.BlockSpec((1, SUBLANES, d), lambda bi, i: (bi, jnp.maximum(i * hb - 1, 0), 0)),
                  pl.BlockSpec((1, SUBLANES, d), lambda bi, i: (bi, jnp.minimum((i + 1) * hb, nhb - 1), 0)),
                  pl.BlockSpec(cw.shape, lambda bi, i: (0, 0)),
                  tile, _mod_spec(gate),
                  pl.BlockSpec((d, d), lambda bi, i: (0, 0))],
        out_specs=tile,
        out_shape=jax.ShapeDtypeStruct((b, s, d), F32),
        compiler_params=_cparams(2),
        name="conv_out",
    )(bg, u, u, u, cw, x, gate, w)


PEER_TM = 256
KEY_TILES = PEER_NKEYS // SUBLANES
assert PEER_TOPK == 2 * SUBLANES


def _tree(op, xs):
    xs = list(xs)
    while len(xs) > 1:
        xs = [op(xs[i], xs[i + 1]) for i in range(0, len(xs) - 1, 2)] + ([xs[-1]] if len(xs) % 2 else [])
    return xs[0]


def _all_sublanes(op, x):
    for shift in (4, 2, 1):
        x = op(x, pltpu.roll(x, shift, axis=0))
    return x


def _top_keys(problems, sub):
    key_id = [sub + SUBLANES * v for v in range(KEY_TILES)]
    problems = [list(tiles) for tiles in problems]
    out = [[] for _ in problems]
    for _ in range(PEER_TOPK):
        for i, tiles in enumerate(problems):
            m = _all_sublanes(jnp.maximum, _tree(jnp.maximum, tiles))
            idx = _all_sublanes(jnp.minimum, _tree(jnp.minimum, [jnp.where(t == m, k, PEER_NKEYS)
                                                                 for t, k in zip(tiles, key_id)]))
            problems[i] = [jnp.where(k == idx, NEG_INF, t) for t, k in zip(tiles, key_id)]
            out[i].append((m, idx))
    return out


def _rows_of(ranked, sub, which, pick):
    t = ranked[pick(0)][which]
    for r in range(1, SUBLANES):
        t = jnp.where(sub == r, ranked[pick(r)][which], t)
    return t


def _product_key_topk(pairs, sub):
    def tiles(first, second, which):
        lo = _rows_of(second, sub, which, lambda r: r)
        hi = _rows_of(second, sub, which, lambda r: SUBLANES + r)
        quad = _rows_of(second, sub, which, lambda r: r % 4)
        a45 = _rows_of(first, sub, which, lambda r: 4 + r // 4)
        a67 = _rows_of(first, sub, which, lambda r: 6 + r // 4)
        ahi = _rows_of(first, sub, which, lambda r: SUBLANES + r)
        a = [first[k][which] for k in range(4)]
        return (a[0], lo), (a[0], hi), (a[1], lo), (a[2], lo), (a[3], lo), (a45, quad), (a67, quad), (ahi, second[0][which])

    cands = [[x + y for x, y in tiles(f, s, 0)] for f, s in pairs]
    cidxs = [[x * PEER_NKEYS + y for x, y in tiles(f, s, 1)] for f, s in pairs]
    quad_pos = jnp.where(sub < 4, sub, sub + (PEER_TOPK - 4))
    pos = [sub, sub + 8, sub + 16, sub + 32, sub + 48, quad_pos + 64, quad_pos + 96, (sub + 8) * PEER_TOPK]
    out = [[] for _ in pairs]
    for _ in range(PEER_TOPK):
        for i, (cand, cidx) in enumerate(zip(cands, cidxs)):
            m = _all_sublanes(jnp.maximum, _tree(jnp.maximum, cand))
            psel = _all_sublanes(jnp.minimum, _tree(jnp.minimum, [jnp.where(c == m, p, PEER_TOPK * PEER_TOPK)
                                                                  for c, p in zip(cand, pos)]))
            hits = [p == psel for p in pos]
            e = _all_sublanes(jnp.maximum, _tree(jnp.maximum, [jnp.where(hh, x, -1) for hh, x in zip(hits, cidx)]))
            cands[i] = [jnp.where(hh, NEG_INF, c) for hh, c in zip(hits, cand)]
            out[i].append((m, e))
    return out


def _peer_score_kernel(x_ref, g_ref, sh_ref, sc_ref, wq_ref, keys_ref, h_ref, eidx_ref, gate_ref,
                       q_scr, s_scr, e_scr, p_scr):
    tm = x_ref.shape[1]
    nt = (((1,), (1,)), ((), ()))
    h = _norm_mod(x_ref[0], g_ref[...], sh_ref[0], sc_ref[0])
    h_ref[0] = h
    q = jnp.dot(h.astype(BF16), wq_ref[...], preferred_element_type=F32)
    for c in range(2 * PEER_HEADS):
        q_scr[c] = q[:, c * PEER_DK_HALF:(c + 1) * PEER_DK_HALF].astype(BF16)
    sub = lax.broadcasted_iota(jnp.int32, (SUBLANES, LANES), 0)

    def head_step(hd, carry):
        for p in range(2):
            s_scr[p] = lax.dot_general(keys_ref[hd, p], q_scr[2 * hd + p], nt, preferred_element_type=F32)
        row0 = pl.multiple_of(hd * PEER_TOPK, PEER_TOPK)
        groups = tm // LANES
        cols = [slice(lg * LANES, (lg + 1) * LANES) for lg in range(groups)]
        ranked = _top_keys([[s_scr[p, v * SUBLANES:(v + 1) * SUBLANES, cols[lg]] for v in range(KEY_TILES)]
                            for lg in range(groups) for p in range(2)], sub)
        tops = _product_key_topk([(ranked[2 * lg], ranked[2 * lg + 1]) for lg in range(groups)], sub)
        for lg, top in enumerate(tops):
            ex = [(jnp.exp(m - top[0][0]), e) for m, e in top]
            denom = _tree(jnp.add, [v for v, _ in ex])
            for half in range(2):
                pick = lambda r, half=half: half * SUBLANES + r
                rows = pl.ds(row0 + half * SUBLANES, SUBLANES)
                e_scr[rows, cols[lg]] = _rows_of(ex, sub, 1, pick)
                p_scr[rows, cols[lg]] = _rows_of(ex, sub, 0, pick) / denom
        return carry

    lax.fori_loop(0, PEER_HEADS, head_step, 0)
    eidx_ref[0] = e_scr[...].T
    gate_ref[0] = p_scr[...].T


def _peer_score(x, g, sh, sc, wq, keys):
    b, s, d = x.shape
    tm = PEER_TM
    n = wq.shape[1]
    tile = lambda k: pl.BlockSpec((1, tm, k), lambda bi, i: (bi, i, 0))
    return pl.pallas_call(
        _peer_score_kernel,
        grid=(b, s // tm),
        in_specs=[tile(d),
                  pl.BlockSpec((1, d), lambda bi, i: (0, 0)),
                  _mod_spec(sh), _mod_spec(sc),
                  pl.BlockSpec((d, n), lambda bi, i: (0, 0)),
                  pl.BlockSpec(keys.shape, lambda bi, i: (0, 0, 0, 0))],
        out_specs=[tile(d), tile(PEER_SEL), tile(PEER_SEL)],
        out_shape=[jax.ShapeDtypeStruct((b, s, d), F32),
                   jax.ShapeDtypeStruct((b, s, PEER_SEL), jnp.int32),
                   jax.ShapeDtypeStruct((b, s, PEER_SEL), F32)],
        scratch_shapes=[pltpu.VMEM((2 * PEER_HEADS, tm, PEER_DK_HALF), BF16),
                        pltpu.VMEM((2, PEER_NKEYS, tm), F32),
                        pltpu.VMEM((PEER_SEL, tm), jnp.int32),
                        pltpu.VMEM((PEER_SEL, tm), F32)],
        compiler_params=_cparams(2),
        name="peer_score",
    )(x, g.reshape(1, d), sh, sc, wq, keys)


HALF_ROWS = SUBLANES // 2


PACK_EB = 256
DOWN_WHOLE_WORD = True
UP_WHOLE_WORD = False


def _pack_kernel(x_ref, o_ref, *, whole_word):
    x_ref = x_ref.at[0]
    half = x_ref.shape[1] // 2
    is_tail = pl.program_id(0) == pl.num_programs(0) - 1
    row = lax.broadcasted_iota(jnp.int32, (PACK_EB, LANES), 0)
    for c in range(HALF_ROWS):
        lo = x_ref[:, half + c * LANES:half + (c + 1) * LANES].astype(BF16).astype(F32)
        lo_bits = pltpu.bitcast(lo, jnp.int32) >> 16 & 0xFFFF
        hi = x_ref[:, c * LANES:(c + 1) * LANES]
        if whole_word:
            bits = pltpu.bitcast(hi, jnp.int32)
            excess = lo_bits - (bits & 0xFFFF)
            step = jnp.where(excess > 0x8000, -1, jnp.where(excess < -0x8000, 1, 0))
            step = jnp.where((bits & 0x7FFF0000) == 0, jnp.maximum(step, 0), step)
            hi_bits = (bits & jnp.int32(-0x10000)) + (step << 16)
        else:
            hi_bits = pltpu.bitcast(hi.astype(BF16).astype(F32), jnp.int32)
        words = pltpu.bitcast(hi_bits | lo_bits, jnp.uint32)
        tail = jnp.where(row == 1, pltpu.roll(words, 1, axis=0), jnp.uint32(0))
        o_ref[pl.ds(c, PACK_EB, stride=HALF_ROWS), :] = jnp.where(is_tail, tail, words)


def _pack_table(tables, layer, whole_word):
    _, e, d = tables.shape
    assert d // 2 == HALF_ROWS * LANES and e % PACK_EB == 0
    nb = e // PACK_EB
    return pl.pallas_call(
        functools.partial(_pack_kernel, whole_word=whole_word),
        grid=(nb + 1,),
        in_specs=[pl.BlockSpec((1, PACK_EB, d), lambda i: (layer, i % nb, 0))],
        out_specs=pl.BlockSpec((PACK_EB * HALF_ROWS, LANES), lambda i: (i, 0)),
        out_shape=jax.ShapeDtypeStruct(((e + PACK_EB) * HALF_ROWS, LANES), jnp.uint32),
        compiler_params=_cparams(1),
        name="pack_table",
    )(tables)


def _load_rows(eidx, on_low_sublanes, n_experts):
    high_start = jnp.where(eidx == 0, HALF_ROWS * n_experts, HALF_ROWS * (eidx - 1))
    return jnp.where(on_low_sublanes, HALF_ROWS * eidx, high_start)


def _unpack(words, whole_word):
    first = words if whole_word else words & jnp.uint32(0xFFFF0000)
    return (pltpu.bitcast(first, F32), pltpu.bitcast(words << 16, F32))


_FOLD_POS = (6, 2, 4, 0, 7, 3, 5, 1)


MERGED = SUBLANES // 2


def _merge_halves(p, sub):
    return [jnp.where(sub < HALF_ROWS, p[2 * k], p[2 * k + 1]) for k in range(MERGED)]


def _fold_rows(m, sub):
    n = [jnp.where((sub & 2) != 0, m[2 * k] + pltpu.roll(m[2 * k], 2, axis=0),
                   m[2 * k + 1] + pltpu.roll(m[2 * k + 1], 6, axis=0)) for k in range(2)]
    return jnp.where((sub & 1) != 0, n[0] + pltpu.roll(n[0], 1, axis=0), n[1] + pltpu.roll(n[1], 7, axis=0))


DOWN_TM = 256
DOWN_BLOCK = 32


def _token_tiles(h_ref, t0, sub):
    chunks = [h_ref[pl.ds(t0, SUBLANES), r * LANES:(r + 1) * LANES] for r in range(SUBLANES)]
    tiles = []
    for s in range(SUBLANES):
        tile = None
        for r in range(SUBLANES):
            piece = jnp.broadcast_to(chunks[r][s:s + 1, :], (SUBLANES, LANES))
            tile = piece if tile is None else jnp.where(sub == r, piece, tile)
        tiles.append(tile)
    return tiles


def _peer_down_kernel(*refs):
    row_refs = refs[:SUBLANES]
    h_ref, gate_ref, tbl_ref, o_ref, pend_even, pend_odd = refs[SUBLANES:]
    pend_scr = (pend_even, pend_odd)
    tm = h_ref.shape[0]
    n_blocks = PEER_SEL // DOWN_BLOCK
    assert n_blocks % 2 == 0
    lane = lax.broadcasted_iota(jnp.int32, (SUBLANES, LANES), 1)
    sub = lax.broadcasted_iota(jnp.int32, (SUBLANES, LANES), 0)

    def group(gi, carry):
        t0 = pl.multiple_of(gi * SUBLANES, SUBLANES)
        hs = []
        for s, tile in enumerate(_token_tiles(h_ref, t0, sub)):
            swapped = pltpu.roll(tile, HALF_ROWS, axis=0)
            hs.append((tile, swapped) if _FOLD_POS[s] % 2 == 0 else (swapped, tile))
        gbase = gi * PEER_SEL

        def stash(slot, j0):
            for k in range(DOWN_BLOCK):
                prods = [None] * SUBLANES
                for s in range(SUBLANES):
                    hi, lo = _unpack(tbl_ref[pl.ds(row_refs[s][gbase + j0 + k], SUBLANES), :], DOWN_WHOLE_WORD)
                    prods[_FOLD_POS[s]] = hi * hs[s][0] + lo * hs[s][1]
                for q, tile in enumerate(_merge_halves(prods, sub)):
                    pend_scr[slot][MERGED * k + q] = tile

        def place(accs, slot, j0):
            accs = list(accs)
            for k in range(DOWN_BLOCK):
                for q in range(MERGED):
                    tot = jnp.sum(pend_scr[slot][MERGED * k + q], axis=-1, keepdims=True)
                    accs[q] = jnp.where(lane == j0 + k, tot, accs[q])
            return tuple(accs)

        def block_pair(i, accs):
            j0 = 2 * i * DOWN_BLOCK
            accs = place(accs, 0, j0)
            stash(1, j0 + DOWN_BLOCK)
            accs = place(accs, 1, j0 + DOWN_BLOCK)
            stash(0, j0 + 2 * DOWN_BLOCK)
            return accs

        stash(0, 0)
        accs = lax.fori_loop(0, n_blocks // 2 - 1, block_pair, (jnp.zeros((SUBLANES, LANES), F32),) * MERGED)
        accs = place(accs, 0, PEER_SEL - 2 * DOWN_BLOCK)
        stash(1, PEER_SEL - DOWN_BLOCK)
        accs = place(accs, 1, PEER_SEL - DOWN_BLOCK)
        act = _fold_rows(accs, sub)
        gelu = act * (lax.erf(act / np.sqrt(2).astype(np.float32)) + 1.0) / 2.0
        o_ref[pl.ds(t0, SUBLANES), :] = gate_ref[pl.ds(t0, SUBLANES), :] * gelu
        return carry

    lax.fori_loop(0, tm // SUBLANES, group, 0)


def _peer_down(eidx, h, gate, tbl):
    t, d = h.shape
    tm = DOWN_TM
    low = (jnp.asarray(_FOLD_POS, jnp.int32) % 2 == 0)[None, :, None]
    rows = _load_rows(eidx.reshape(t // SUBLANES, SUBLANES, PEER_SEL), low, tbl.shape[0] // HALF_ROWS - PACK_EB)
    rows = rows.transpose(1, 0, 2).reshape(SUBLANES, -1)
    smem = pl.BlockSpec((tm // SUBLANES * PEER_SEL,), lambda i: (i,), memory_space=pltpu.SMEM)
    return pl.pallas_call(
        _peer_down_kernel,
        grid=(t // tm,),
        in_specs=[smem] * SUBLANES + [
            pl.BlockSpec((tm, d), lambda i: (i, 0)),
            pl.BlockSpec((tm, PEER_SEL), lambda i: (i, 0)),
            pl.BlockSpec(tbl.shape, lambda i: (0, 0), pipeline_mode=pl.Buffered(1))],
        out_specs=pl.BlockSpec((tm, PEER_SEL), lambda i: (i, 0)),
        out_shape=jax.ShapeDtypeStruct((t, PEER_SEL), F32),
        scratch_shapes=[pltpu.VMEM((DOWN_BLOCK * MERGED, SUBLANES, LANES), F32)] * 2,
        compiler_params=_cparams(1),
        name="peer_down",
    )(*[rows[s] for s in range(SUBLANES)], h, gate, tbl)


UP_TM = 128
UP_BLOCK = 16
UP_SPREAD = SUBLANES * UP_BLOCK // PEER_SEL


def _peer_up_kernel(*refs):
    row_refs = refs[:SUBLANES]
    w_ref, x_ref, gate_ref, tbl_ref, o_ref, stage_scr, wt_scr, wv_even, wv_odd = refs[SUBLANES:]
    wv_scr = (wv_even, wv_odd)
    tm, d = o_ref.shape
    n_groups = tm // SUBLANES
    sub = lax.broadcasted_iota(jnp.int32, (SUBLANES, LANES), 0)
    low = sub < HALF_ROWS
    wt_scr[...] = w_ref[...].T

    def spread(t, slot, s):
        lanes = jnp.full((PEER_SEL, LANES), t, jnp.int32)
        wv_scr[slot][s] = jnp.take_along_axis(wt_scr[...], lanes, axis=1, mode="promise_in_bounds")

    def group(g, cur):
        t0 = pl.multiple_of(g * SUBLANES, SUBLANES)
        next_t0 = jnp.minimum(g + 1, n_groups - 1) * SUBLANES
        gbase = g * PEER_SEL

        def block(jb, accs):
            accs = list(accs)
            j0 = jb * UP_BLOCK
            for u in range(UP_SPREAD):
                s_next = jb * UP_SPREAD + u
                spread(next_t0 + s_next, 1 - cur, s_next)
            wv_rows = [wv_scr[cur].at[s, pl.ds(pl.multiple_of(j0, UP_BLOCK), UP_BLOCK)] for s in range(SUBLANES)]
            for k in range(0, UP_BLOCK, 2):
                for s in range(SUBLANES):
                    words = jnp.where(low, tbl_ref[pl.ds(row_refs[s][gbase + j0 + k], SUBLANES), :],
                                      tbl_ref[pl.ds(row_refs[s][gbase + j0 + k + 1], SUBLANES), :])
                    wv = jnp.where(low, jnp.broadcast_to(wv_rows[s][k:k + 1, :], (SUBLANES, LANES)),
                                   jnp.broadcast_to(wv_rows[s][k + 1:k + 2, :], (SUBLANES, LANES)))
                    hi, lo = _unpack(words, UP_WHOLE_WORD)
                    accs[2 * s] = accs[2 * s] + wv * hi
                    accs[2 * s + 1] = accs[2 * s + 1] + wv * lo
            return tuple(accs)

        zero = jnp.zeros((SUBLANES, LANES), F32)
        accs = lax.fori_loop(0, PEER_SEL // UP_BLOCK, block, (zero,) * (2 * SUBLANES))
        for s in range(SUBLANES):
            hi, lo = accs[2 * s], accs[2 * s + 1]
            stage_scr[s * SUBLANES:(s + 1) * SUBLANES, :] = jnp.where(
                low, hi + pltpu.roll(hi, HALF_ROWS, axis=0), lo + pltpu.roll(lo, HALF_ROWS, axis=0))
        for r in range(SUBLANES):
            cols = slice(r * LANES, (r + 1) * LANES)
            o_ref[pl.ds(t0, SUBLANES), cols] = (x_ref[pl.ds(t0, SUBLANES), cols]
                                                + gate_ref[0, :, cols] * stage_scr[pl.ds(r, SUBLANES, stride=SUBLANES), :])

    def group_pair(gp, carry):
        group(2 * gp, 0)
        group(2 * gp + 1, 1)
        return carry

    for s in range(SUBLANES):
        spread(s, 0, s)
    lax.fori_loop(0, n_groups // 2, group_pair, 0)


def _peer_up(eidx, w, tbl, x, gate):
    b, s, d = x.shape
    t = b * s
    tm = UP_TM
    assert s % tm == 0 and tm == LANES
    rows = _load_rows(eidx, (jnp.arange(PEER_SEL) % 2 == 0)[None, :], tbl.shape[0] // HALF_ROWS - PACK_EB)
    rows = rows.reshape(t // SUBLANES, SUBLANES, PEER_SEL).transpose(1, 0, 2).reshape(SUBLANES, -1)
    smem = pl.BlockSpec((tm // SUBLANES * PEER_SEL,), lambda i: (i,), memory_space=pltpu.SMEM)
    tiles_per_batch = s // tm
    out = pl.pallas_call(
        _peer_up_kernel,
        grid=(t // tm,),
        in_specs=[smem] * SUBLANES + [
            pl.BlockSpec((tm, PEER_SEL), lambda i: (i, 0)),
            pl.BlockSpec((tm, d), lambda i: (i, 0)),
            pl.BlockSpec((1, 1, d), lambda i: (i // tiles_per_batch, 0, 0)),
            pl.BlockSpec(tbl.shape, lambda i: (0, 0), pipeline_mode=pl.Buffered(1))],
        out_specs=pl.BlockSpec((tm, d), lambda i: (i, 0)),
        out_shape=jax.ShapeDtypeStruct((t, d), F32),
        scratch_shapes=[pltpu.VMEM((SUBLANES * SUBLANES, LANES), F32),
                        pltpu.VMEM((PEER_SEL, tm), F32),
                        pltpu.VMEM((SUBLANES, PEER_SEL, LANES), F32), pltpu.VMEM((SUBLANES, PEER_SEL, LANES), F32)],
        compiler_params=_cparams(1),
        name="peer_up",
    )(*[rows[u] for u in range(SUBLANES)], w, x.reshape(t, d), gate, tbl)
    return out.reshape(b, s, d)


def _peer_block(x, g, sh, sc, gate2, wq, keys, down, up, layer):
    b, s, d = x.shape
    t = b * s
    h, eidx, gsm = _peer_score(x, g, sh, sc, wq.astype(BF16), keys.astype(BF16))
    eidx = eidx.reshape(t, PEER_SEL)
    w = _peer_down(eidx, h.reshape(t, d), gsm.reshape(t, PEER_SEL), _pack_table(down, layer, DOWN_WHOLE_WORD))
    return _peer_up(eidx, w, _pack_table(up, layer, UP_WHOLE_WORD), x, gate2)


def _rmsnorm_kernel(x_ref, g_ref, o_ref):
    x = x_ref[0]
    o_ref[0] = (x * lax.rsqrt(jnp.mean(x * x, axis=-1, keepdims=True) + EPS)) * g_ref[...]


def _rmsnorm(x, g, tm):
    b, s, d = x.shape
    tile = pl.BlockSpec((1, tm, d), lambda bi, i: (bi, i, 0))
    return pl.pallas_call(
        _rmsnorm_kernel,
        grid=(b, s // tm),
        in_specs=[tile, pl.BlockSpec((1, d), lambda bi, i: (0, 0))],
        out_specs=tile,
        out_shape=jax.ShapeDtypeStruct((b, s, d), F32),
        compiler_params=_cparams(2),
        name="final_rmsnorm",
    )(x, g.reshape(1, d))


def kernel(x, c, ctx, c_ctx, ada_w, ada_b, norm1_g, norm2_g, final_g, ab_w_in, ab_w_out, na_rpb, fn_w,
           cv_w_in, cv_w, cv_w_out, peer_w_q, peer_keys, peer_down, peer_up):
    b, s, d = x.shape
    depth = ada_w.shape[0]
    rows = -(-(b + 1) // SUBLANES) * SUBLANES
    cc = jnp.concatenate([c, c_ctx[None], jnp.zeros((rows - b - 1, d), F32)], axis=0)
    mod = _ada_vectors(cc, ada_w, ada_b)

    def chunks(i, lo, hi):
        m = mod[i, lo:hi].reshape(hi - lo, 1, 6, d)
        return [m[:, :, k] for k in range(6)]

    for i in range(depth):
        sh1, sc1, g1, sh2, sc2, g2 = chunks(i, 0, b)
        if i % 2 == 0:
            e = i // 2
            csh1, csc1 = chunks(i, b, b + 1)[:2]
            w_in = ab_w_in[e].astype(BF16)
            qkv, f = _norm_mod_matmul(x, norm1_g[i], sh1, sc1, w_in, (3 * NA_WIDTH, FN_WIDTH), (BF16, BF16),
                                      512, "ab_in")
            (ctx_kv,) = _norm_mod_matmul(ctx, norm1_g[i], csh1, csc1, w_in[:, NA_WIDTH:3 * NA_WIDTH],
                                         (2 * NA_WIDTH,), (BF16,), ctx.shape[1], "ab_in_ctx")
            a = _attention(qkv, ctx_kv, _attention_bias(na_rpb[e]))
            fm = _fnet(f, fn_w[e])
            x = _proj_residual(a, fm, x, g1, ab_w_out[e].astype(BF16), 512)
        else:
            o = i // 2
            bg, u = _conv_in(x, norm1_g[i], sh1, sc1, cv_w_in[o].astype(BF16), 256)
            x = _conv_out(bg, u, cv_w[o], x, g1, cv_w_out[o].astype(BF16), 512)
        x = _peer_block(x, norm2_g[i], sh2, sc2, g2, peer_w_q[i], peer_keys[i], peer_down, peer_up, i)
    return _rmsnorm(x, final_g, 512)
```

```python
import functools

import jax
import jax.numpy as jnp
import numpy as np
from jax import lax
from jax.experimental import pallas as pl
from jax.experimental.pallas import tpu as pltpu

F32 = jnp.float32
BF16 = jnp.bfloat16
EPS = 1e-6

GRID_W = 64
HEAD_DIM = 64
NA_HEADS = 8
NA_WIDTH = NA_HEADS * HEAD_DIM
KH = 8
KH_MAX = 8
KW = 16
FN_GROUPS = 8
FN_GROUP_DIM = 64
FN_WIDTH = FN_GROUPS * FN_GROUP_DIM
PEER_HEADS = 8
PEER_NKEYS = 128
PEER_DK_HALF = 128
PEER_TOPK = 16
PEER_SEL = PEER_HEADS * PEER_TOPK
LANES = 128
SUBLANES = 8
VMEM_LIMIT = 56 * 1024 * 1024

NEG_INF = float("-inf")


def _cparams(n_axes):
    return pltpu.CompilerParams(dimension_semantics=("arbitrary",) * n_axes, vmem_limit_bytes=VMEM_LIMIT)


def _norm_mod(x, g, sh, sc):
    y = x * lax.rsqrt(jnp.mean(x * x, axis=-1, keepdims=True) + EPS)
    return (y * g) * (1.0 + sc) + sh


def _ada_kernel(c_ref, w_ref, b_ref, o_ref):
    c = c_ref[...]
    s = c / (1.0 + jnp.exp(-c))
    o_ref[0] = jnp.dot(s, w_ref[0], preferred_element_type=F32, precision=lax.Precision.HIGHEST) + b_ref[0]


def _ada_vectors(cc, ada_w, ada_b):
    depth, d, n = ada_w.shape
    rows = cc.shape[0]
    tn = 1536
    return pl.pallas_call(
        _ada_kernel,
        grid=(depth, n // tn),
        in_specs=[pl.BlockSpec((rows, d), lambda i, j: (0, 0)),
                  pl.BlockSpec((1, d, tn), lambda i, j: (i, 0, j)),
                  pl.BlockSpec((1, 1, tn), lambda i, j: (i, 0, j))],
        out_specs=pl.BlockSpec((1, rows, tn), lambda i, j: (i, 0, j)),
        out_shape=jax.ShapeDtypeStruct((depth, rows, n), F32),
        compiler_params=_cparams(2),
        name="ada_vectors",
    )(cc, ada_w, ada_b.reshape(depth, 1, n))


def _nmm_kernel(x_ref, g_ref, sh_ref, sc_ref, w_ref, *o_refs):
    h = _norm_mod(x_ref[0], g_ref[...], sh_ref[0], sc_ref[0])
    acc = jnp.dot(h.astype(BF16), w_ref[...], preferred_element_type=F32)
    off = 0
    for o_ref in o_refs:
        n = o_ref.shape[-1]
        o_ref[0] = acc[:, off:off + n].astype(o_ref.dtype)
        off += n


def _mod_spec(arr):
    d = arr.shape[-1]
    if arr.shape[0] == 1:
        return pl.BlockSpec((1, 1, d), lambda b, i: (0, 0, 0))
    return pl.BlockSpec((1, 1, d), lambda b, i: (b, 0, 0))


def _norm_mod_matmul(x, g, sh, sc, w, splits, dtypes, tm, name):
    b, s, d = x.shape
    n = w.shape[1]
    assert sum(splits) == n
    return pl.pallas_call(
        _nmm_kernel,
        grid=(b, s // tm),
        in_specs=[pl.BlockSpec((1, tm, d), lambda bi, i: (bi, i, 0)),
                  pl.BlockSpec((1, d), lambda bi, i: (0, 0)),
                  _mod_spec(sh), _mod_spec(sc),
                  pl.BlockSpec((d, n), lambda bi, i: (0, 0))],
        out_specs=[pl.BlockSpec((1, tm, k), lambda bi, i: (bi, i, 0)) for k in splits],
        out_shape=[jax.ShapeDtypeStruct((b, s, k), dt) for k, dt in zip(splits, dtypes)],
        compiler_params=_cparams(2),
        name=name,
    )(x, g.reshape(1, d), sh, sc, w)


def _attn_kernel(qkv_ref, ctx_ref, bias_ref, o_ref):
    rows = qkv_ref.shape[1] // GRID_W
    nt = (((1,), (1,)), ((), ()))
    scale = HEAD_DIM ** -0.5

    def row_step(r, carry):
        start = jnp.clip(r - KH // 2, 0, rows - KH)
        off = start - r + (KH_MAX - 1)
        q0 = pl.multiple_of(r * GRID_W, GRID_W)
        k0 = pl.multiple_of(start * GRID_W, GRID_W)
        heads = range(NA_HEADS)
        cols = [slice(h * HEAD_DIM, (h + 1) * HEAD_DIM) for h in heads]
        kcols = [slice(NA_WIDTH + h * HEAD_DIM, NA_WIDTH + (h + 1) * HEAD_DIM) for h in heads]
        vcols = [slice(2 * NA_WIDTH + h * HEAD_DIM, 2 * NA_WIDTH + (h + 1) * HEAD_DIM) for h in heads]
        q = [qkv_ref[0, pl.ds(q0, GRID_W), cols[h]] for h in heads]
        s_win = [lax.dot_general(q[h], qkv_ref[0, pl.ds(k0, KH * GRID_W), kcols[h]], nt,
                                 preferred_element_type=F32) * scale + bias_ref[h, off] for h in heads]
        s_ctx = [lax.dot_general(q[h], ctx_ref[0, :, cols[h]], nt, preferred_element_type=F32) * scale
                 for h in heads]
        m = [jnp.maximum(jnp.max(s_win[h], axis=-1, keepdims=True), jnp.max(s_ctx[h], axis=-1, keepdims=True))
             for h in heads]
        p_win = [jnp.exp(s_win[h] - m[h]) for h in heads]
        p_ctx = [jnp.exp(s_ctx[h] - m[h]) for h in heads]
        l = [jnp.sum(p_win[h], axis=-1, keepdims=True) + jnp.sum(p_ctx[h], axis=-1, keepdims=True) for h in heads]
        o = [jnp.dot(p_win[h].astype(BF16), qkv_ref[0, pl.ds(k0, KH * GRID_W), vcols[h]], preferred_element_type=F32)
             + jnp.dot(p_ctx[h].astype(BF16), ctx_ref[0, :, kcols[h]], preferred_element_type=F32) for h in heads]
        outs = [o[h] / l[h] for h in heads]
        o_ref[0, pl.ds(q0, GRID_W), :] = jnp.concatenate(outs, axis=-1).astype(o_ref.dtype)
        return carry

    lax.fori_loop(0, rows, row_step, 0)


def _attention_bias(rpb):
    cols = np.arange(GRID_W)
    col_start = np.clip(cols - KW // 2, 0, GRID_W - KW)
    in_win = (cols[None, :] >= col_start[:, None]) & (cols[None, :] < col_start[:, None] + KW)
    col_off = np.clip(cols[None, :] - cols[:, None] + KW - 1, 0, 2 * KW - 2)
    onehot = jnp.asarray(col_off[None] == np.arange(2 * KW - 1)[:, None, None], F32)
    by_row = jnp.sum(rpb.astype(F32)[:, :, :, None, None] * onehot[None, None], axis=2)
    by_row = jnp.where(in_win[None, None], by_row, NEG_INF)
    b = jnp.stack([by_row[:, o:o + KH] for o in range(KH)], axis=1)
    return b.transpose(0, 1, 3, 2, 4).reshape(NA_HEADS, KH, GRID_W, KH * GRID_W)


def _attention(qkv, ctx_kv, bias):
    b, s, _ = qkv.shape
    l = ctx_kv.shape[1]
    return pl.pallas_call(
        _attn_kernel,
        grid=(b,),
        in_specs=[pl.BlockSpec((1, s, 3 * NA_WIDTH), lambda i: (i, 0, 0)),
                  pl.BlockSpec((1, l, 2 * NA_WIDTH), lambda i: (i, 0, 0)),
                  pl.BlockSpec(bias.shape, lambda i: (0, 0, 0, 0), pipeline_mode=pl.Buffered(1))],
        out_specs=pl.BlockSpec((1, s, NA_WIDTH), lambda i: (i, 0, 0)),
        out_shape=jax.ShapeDtypeStruct((b, s, NA_WIDTH), BF16),
        compiler_params=_cparams(1),
        name="nbr_attention",
    )(qkv, ctx_kv, bias)


FN_ROW_CHUNK = 256


def _fnet_kernel(f_ref, cn_ref, sn_ref, c64_ref, s64_ref, wbd_ref, o_ref, a_scr, b_scr, xa_scr, xb_scr):
    n = f_ref.shape[1]
    scale = (n * FN_GROUP_DIM) ** -0.5

    @pl.when(pl.program_id(0) == 0)
    def _():
        hi = lax.Precision.HIGHEST
        a_scr[...] = (jnp.dot(c64_ref[...], wbd_ref[...], preferred_element_type=F32, precision=hi) * scale).astype(BF16)
        b_scr[...] = (jnp.dot(s64_ref[...], wbd_ref[...], preferred_element_type=F32, precision=hi) * -scale).astype(BF16)

    x = f_ref[0]
    xa_scr[...] = jnp.dot(x, a_scr[...], preferred_element_type=F32).astype(BF16)
    xb_scr[...] = jnp.dot(x, b_scr[...], preferred_element_type=F32).astype(BF16)

    def chunk(i, carry):
        r0 = pl.multiple_of(i * FN_ROW_CHUNK, FN_ROW_CHUNK)
        y = (jnp.dot(cn_ref[pl.ds(r0, FN_ROW_CHUNK), :], xa_scr[...], preferred_element_type=F32)
             + jnp.dot(sn_ref[pl.ds(r0, FN_ROW_CHUNK), :], xb_scr[...], preferred_element_type=F32))
        o_ref[0, pl.ds(r0, FN_ROW_CHUNK), :] = y.astype(o_ref.dtype)
        return carry

    lax.fori_loop(0, n // FN_ROW_CHUNK, chunk, 0)


def _dft_tables(n):
    k = (np.arange(n)[:, None] * np.arange(n)[None, :]) % n
    ang = 2.0 * np.pi * k.astype(np.float64) / n
    return np.cos(ang), np.sin(ang)


def _fnet(f, fn_w):
    b, s, _ = f.shape
    cn, sn = _dft_tables(s)
    c64, s64 = _dft_tables(FN_GROUP_DIM)
    eye = np.eye(FN_GROUPS)
    c64bd = jnp.asarray(np.kron(eye, c64), F32)
    s64bd = jnp.asarray(np.kron(eye, s64), F32)
    wbd = (jnp.asarray(eye, F32)[:, None, :, None] * fn_w[:, :, None, :]).reshape(FN_WIDTH, FN_WIDTH)
    const = lambda shape: pl.BlockSpec(shape, lambda i: (0, 0), pipeline_mode=pl.Buffered(1))
    return pl.pallas_call(
        _fnet_kernel,
        grid=(b,),
        in_specs=[pl.BlockSpec((1, s, FN_WIDTH), lambda i: (i, 0, 0)),
                  const((s, s)), const((s, s)),
                  const((FN_WIDTH, FN_WIDTH)), const((FN_WIDTH, FN_WIDTH)), const((FN_WIDTH, FN_WIDTH))],
        out_specs=pl.BlockSpec((1, s, FN_WIDTH), lambda i: (i, 0, 0)),
        out_shape=jax.ShapeDtypeStruct((b, s, FN_WIDTH), BF16),
        scratch_shapes=[pltpu.VMEM((FN_WIDTH, FN_WIDTH), BF16), pltpu.VMEM((FN_WIDTH, FN_WIDTH), BF16),
                        pltpu.VMEM((s, FN_WIDTH), BF16), pltpu.VMEM((s, FN_WIDTH), BF16)],
        compiler_params=_cparams(1),
        name="fnet_mix",
    )(f, jnp.asarray(cn, BF16), jnp.asarray(sn, BF16), c64bd, s64bd, wbd)


def _proj_res_kernel(a_ref, f_ref, x_ref, g_ref, w_ref, o_ref):
    ka = a_ref.shape[-1]
    out = (jnp.dot(a_ref[0], w_ref[:ka, :], preferred_element_type=F32)
           + jnp.dot(f_ref[0], w_ref[ka:, :], preferred_element_type=F32))
    o_ref[0] = x_ref[0] + g_ref[0] * out


def _proj_residual(a, f, x, gate, w, tm):
    b, s, d = x.shape
    ka, kf = a.shape[-1], f.shape[-1]
    return pl.pallas_call(
        _proj_res_kernel,
        grid=(b, s // tm),
        in_specs=[pl.BlockSpec((1, tm, ka), lambda bi, i: (bi, i, 0)),
                  pl.BlockSpec((1, tm, kf), lambda bi, i: (bi, i, 0)),
                  pl.BlockSpec((1, tm, d), lambda bi, i: (bi, i, 0)),
                  _mod_spec(gate),
                  pl.BlockSpec((ka + kf, d), lambda bi, i: (0, 0))],
        out_specs=pl.BlockSpec((1, tm, d), lambda bi, i: (bi, i, 0)),
        out_shape=jax.ShapeDtypeStruct((b, s, d), F32),
        compiler_params=_cparams(2),
        name="proj_residual",
    )(a, f, x, gate, w)


def _conv_in_kernel(x_ref, g_ref, sh_ref, sc_ref, w_ref, bg_ref, u_ref):
    d = bg_ref.shape[-1]
    h = _norm_mod(x_ref[0], g_ref[...], sh_ref[0], sc_ref[0])
    acc = jnp.dot(h.astype(BF16), w_ref[...], preferred_element_type=F32)
    bg_ref[0] = acc[:, :d]
    u_ref[0] = acc[:, d:2 * d] * acc[:, 2 * d:]


def _conv_in(x, g, sh, sc, w, tm):
    b, s, d = x.shape
    n = w.shape[1]
    return pl.pallas_call(
        _conv_in_kernel,
        grid=(b, s // tm),
        in_specs=[pl.BlockSpec((1, tm, d), lambda bi, i: (bi, i, 0)),
                  pl.BlockSpec((1, d), lambda bi, i: (0, 0)),
                  _mod_spec(sh), _mod_spec(sc),
                  pl.BlockSpec((d, n), lambda bi, i: (0, 0))],
        out_specs=[pl.BlockSpec((1, tm, d), lambda bi, i: (bi, i, 0))] * 2,
        out_shape=[jax.ShapeDtypeStruct((b, s, d), F32)] * 2,
        compiler_params=_cparams(2),
        name="conv_in",
    )(x, g.reshape(1, d), sh, sc, w)


def _conv_out_kernel(bg_ref, u_ref, up_ref, un_ref, cw_ref, x_ref, g_ref, w_ref, o_ref):
    i = pl.program_id(1)
    last = pl.num_programs(1) - 1
    u = u_ref[0]
    tm = u.shape[0]
    row = lax.broadcasted_iota(jnp.int32, u.shape, 0)
    prev_row = jnp.where(i == 0, 0.0, up_ref[0, SUBLANES - 1:SUBLANES, :])
    next_row = jnp.where(i == last, 0.0, un_ref[0, 0:1, :])
    u_prev = jnp.where(row == 0, prev_row, pltpu.roll(u, 1, axis=0))
    u_next = jnp.where(row == tm - 1, next_row, pltpu.roll(u, tm - 1, axis=0))
    y = cw_ref[0:1, :] * u_prev + cw_ref[1:2, :] * u + cw_ref[2:3, :] * u_next
    z = (bg_ref[0] * y).astype(BF16)
    o_ref[0] = x_ref[0] + g_ref[0] * jnp.dot(z, w_ref[...], preferred_element_type=F32)


def _conv_out(bg, u, cw, x, gate, w, tm):
    b, s, d = x.shape
    hb = tm // SUBLANES
    nhb = s // SUBLANES
    tile = pl.BlockSpec((1, tm, d), lambda bi, i: (bi, i, 0))
    return pl.pallas_call(
        _conv_out_kernel,
        grid=(b, s // tm),
        in_specs=[tile, tile,
                  pl.BlockSpec((1, SUBLANES, d), lambda bi, i: (bi, jnp.maximum(i * hb - 1, 0), 0)),
                  pl.BlockSpec((1, SUBLANES, d), lambda bi, i: (bi, jnp.minimum((i + 1) * hb, nhb - 1), 0)),
                  pl.BlockSpec(cw.shape, lambda bi, i: (0, 0)),
                  tile, _mod_spec(gate),
                  pl.BlockSpec((d, d), lambda bi, i: (0, 0))],
        out_specs=tile,
        out_shape=jax.ShapeDtypeStruct((b, s, d), F32),
        compiler_params=_cparams(2),
        name="conv_out",
    )(bg, u, u, u, cw, x, gate, w)


PEER_TM = 512
KEY_TILES = PEER_NKEYS // SUBLANES
assert PEER_TOPK == 2 * SUBLANES


def _tree(op, xs):
    xs = list(xs)
    while len(xs) > 1:
        xs = [op(xs[i], xs[i + 1]) for i in range(0, len(xs) - 1, 2)] + ([xs[-1]] if len(xs) % 2 else [])
    return xs[0]


def _all_sublanes(op, x):
    for shift in (4, 2, 1):
        x = op(x, pltpu.roll(x, shift, axis=0))
    return x


def _top_keys(problems, sub):
    key_id = [sub + SUBLANES * v for v in range(KEY_TILES)]
    problems = [list(tiles) for tiles in problems]
    out = [[] for _ in problems]
    for _ in range(PEER_TOPK):
        for i, tiles in enumerate(problems):
            m = _all_sublanes(jnp.maximum, _tree(jnp.maximum, tiles))
            idx = _all_sublanes(jnp.minimum, _tree(jnp.minimum, [jnp.where(t == m, k, PEER_NKEYS)
                                                                 for t, k in zip(tiles, key_id)]))
            problems[i] = [jnp.where(k == idx, NEG_INF, t) for t, k in zip(tiles, key_id)]
            out[i].append((m, idx))
    return out


def _rows_of(ranked, sub, which, pick):
    t = ranked[pick(0)][which]
    for r in range(1, SUBLANES):
        t = jnp.where(sub == r, ranked[pick(r)][which], t)
    return t


def _product_key_topk(pairs, sub):
    def tiles(first, second, which):
        lo = _rows_of(second, sub, which, lambda r: r)
        hi = _rows_of(second, sub, which, lambda r: SUBLANES + r)
        quad = _rows_of(second, sub, which, lambda r: r % 4)
        a45 = _rows_of(first, sub, which, lambda r: 4 + r // 4)
        a67 = _rows_of(first, sub, which, lambda r: 6 + r // 4)
        ahi = _rows_of(first, sub, which, lambda r: SUBLANES + r)
        a = [first[k][which] for k in range(4)]
        return (a[0], lo), (a[0], hi), (a[1], lo), (a[2], lo), (a[3], lo), (a45, quad), (a67, quad), (ahi, second[0][which])

    cands = [[x + y for x, y in tiles(f, s, 0)] for f, s in pairs]
    cidxs = [[x * PEER_NKEYS + y for x, y in tiles(f, s, 1)] for f, s in pairs]
    quad_pos = jnp.where(sub < 4, sub, sub + (PEER_TOPK - 4))
    pos = [sub, sub + 8, sub + 16, sub + 32, sub + 48, quad_pos + 64, quad_pos + 96, (sub + 8) * PEER_TOPK]
    out = [[] for _ in pairs]
    for _ in range(PEER_TOPK):
        for i, (cand, cidx) in enumerate(zip(cands, cidxs)):
            m = _all_sublanes(jnp.maximum, _tree(jnp.maximum, cand))
            psel = _all_sublanes(jnp.minimum, _tree(jnp.minimum, [jnp.where(c == m, p, PEER_TOPK * PEER_TOPK)
                                                                  for c, p in zip(cand, pos)]))
            hits = [p == psel for p in pos]
            e = _all_sublanes(jnp.maximum, _tree(jnp.maximum, [jnp.where(hh, x, -1) for hh, x in zip(hits, cidx)]))
            cands[i] = [jnp.where(hh, NEG_INF, c) for hh, c in zip(hits, cand)]
            out[i].append((m, e))
    return out


def _peer_score_kernel(x_ref, g_ref, sh_ref, sc_ref, wq_ref, keys_ref, h_ref, eidx_ref, gate_ref,
                       q_scr, s_scr, e_scr, p_scr):
    tm = x_ref.shape[1]
    nt = (((1,), (1,)), ((), ()))
    h = _norm_mod(x_ref[0], g_ref[...], sh_ref[0], sc_ref[0])
    h_ref[0] = h
    q = jnp.dot(h.astype(BF16), wq_ref[...], preferred_element_type=F32)
    for c in range(2 * PEER_HEADS):
        q_scr[c] = q[:, c * PEER_DK_HALF:(c + 1) * PEER_DK_HALF].astype(BF16)
    sub = lax.broadcasted_iota(jnp.int32, (SUBLANES, LANES), 0)

    def head_step(hd, carry):
        for p in range(2):
            s_scr[p] = lax.dot_general(keys_ref[hd, p], q_scr[2 * hd + p], nt, preferred_element_type=F32)
        row0 = pl.multiple_of(hd * PEER_TOPK, PEER_TOPK)
        groups = tm // LANES
        cols = [slice(lg * LANES, (lg + 1) * LANES) for lg in range(groups)]
        ranked = _top_keys([[s_scr[p, v * SUBLANES:(v + 1) * SUBLANES, cols[lg]] for v in range(KEY_TILES)]
                            for lg in range(groups) for p in range(2)], sub)
        tops = _product_key_topk([(ranked[2 * lg], ranked[2 * lg + 1]) for lg in range(groups)], sub)
        for lg, top in enumerate(tops):
            ex = [(jnp.exp(m - top[0][0]), e) for m, e in top]
            denom = _tree(jnp.add, [v for v, _ in ex])
            for half in range(2):
                pick = lambda r, half=half: half * SUBLANES + r
                rows = pl.ds(row0 + half * SUBLANES, SUBLANES)
                e_scr[rows, cols[lg]] = _rows_of(ex, sub, 1, pick)
                p_scr[rows, cols[lg]] = _rows_of(ex, sub, 0, pick) / denom
        return carry

    lax.fori_loop(0, PEER_HEADS, head_step, 0)
    eidx_ref[0] = e_scr[...].T
    gate_ref[0] = p_scr[...].T


def _peer_score(x, g, sh, sc, wq, keys):
    b, s, d = x.shape
    tm = PEER_TM
    n = wq.shape[1]
    tile = lambda k: pl.BlockSpec((1, tm, k), lambda bi, i: (bi, i, 0))
    return pl.pallas_call(
        _peer_score_kernel,
        grid=(b, s // tm),
        in_specs=[tile(d),
                  pl.BlockSpec((1, d), lambda bi, i: (0, 0)),
                  _mod_spec(sh), _mod_spec(sc),
                  pl.BlockSpec((d, n), lambda bi, i: (0, 0)),
                  pl.BlockSpec(keys.shape, lambda bi, i: (0, 0, 0, 0))],
        out_specs=[tile(d), tile(PEER_SEL), tile(PEER_SEL)],
        out_shape=[jax.ShapeDtypeStruct((b, s, d), F32),
                   jax.ShapeDtypeStruct((b, s, PEER_SEL), jnp.int32),
                   jax.ShapeDtypeStruct((b, s, PEER_SEL), F32)],
        scratch_shapes=[pltpu.VMEM((2 * PEER_HEADS, tm, PEER_DK_HALF), BF16),
                        pltpu.VMEM((2, PEER_NKEYS, tm), F32),
                        pltpu.VMEM((PEER_SEL, tm), jnp.int32),
                        pltpu.VMEM((PEER_SEL, tm), F32)],
        compiler_params=_cparams(2),
        name="peer_score",
    )(x, g.reshape(1, d), sh, sc, wq, keys)


HALF_ROWS = SUBLANES // 2


PACK_EB = 256
DOWN_WHOLE_WORD = True
UP_WHOLE_WORD = False


def _pack_kernel(x_ref, o_ref, *, whole_word):
    x_ref = x_ref.at[0]
    half = x_ref.shape[1] // 2
    is_tail = pl.program_id(0) == pl.num_programs(0) - 1
    row = lax.broadcasted_iota(jnp.int32, (PACK_EB, LANES), 0)
    for c in range(HALF_ROWS):
        lo = x_ref[:, half + c * LANES:half + (c + 1) * LANES].astype(BF16).astype(F32)
        lo_bits = pltpu.bitcast(lo, jnp.int32) >> 16 & 0xFFFF
        hi = x_ref[:, c * LANES:(c + 1) * LANES]
        if whole_word:
            bits = pltpu.bitcast(hi, jnp.int32)
            excess = lo_bits - (bits & 0xFFFF)
            step = jnp.where(excess > 0x8000, -1, jnp.where(excess < -0x8000, 1, 0))
            step = jnp.where((bits & 0x7FFF0000) == 0, jnp.maximum(step, 0), step)
            hi_bits = (bits & jnp.int32(-0x10000)) + (step << 16)
        else:
            hi_bits = pltpu.bitcast(hi.astype(BF16).astype(F32), jnp.int32)
        words = pltpu.bitcast(hi_bits | lo_bits, jnp.uint32)
        tail = jnp.where(row == 1, pltpu.roll(words, 1, axis=0), jnp.uint32(0))
        o_ref[pl.ds(c, PACK_EB, stride=HALF_ROWS), :] = jnp.where(is_tail, tail, words)


def _pack_table(tables, layer, whole_word):
    _, e, d = tables.shape
    assert d // 2 == HALF_ROWS * LANES and e % PACK_EB == 0
    nb = e // PACK_EB
    return pl.pallas_call(
        functools.partial(_pack_kernel, whole_word=whole_word),
        grid=(nb + 1,),
        in_specs=[pl.BlockSpec((1, PACK_EB, d), lambda i: (layer, i % nb, 0))],
        out_specs=pl.BlockSpec((PACK_EB * HALF_ROWS, LANES), lambda i: (i, 0)),
        out_shape=jax.ShapeDtypeStruct(((e + PACK_EB) * HALF_ROWS, LANES), jnp.uint32),
        compiler_params=_cparams(1),
        name="pack_table",
    )(tables)


def _load_rows(eidx, on_low_sublanes, n_experts):
    high_start = jnp.where(eidx == 0, HALF_ROWS * n_experts, HALF_ROWS * (eidx - 1))
    return jnp.where(on_low_sublanes, HALF_ROWS * eidx, high_start)


def _unpack(words, whole_word):
    first = words if whole_word else words & jnp.uint32(0xFFFF0000)
    return (pltpu.bitcast(first, F32), pltpu.bitcast(words << 16, F32))


_FOLD_POS = (6, 2, 4, 0, 7, 3, 5, 1)


MERGED = SUBLANES // 2


def _merge_halves(p, sub):
    return [jnp.where(sub < HALF_ROWS, p[2 * k], p[2 * k + 1]) for k in range(MERGED)]


def _fold_rows(m, sub):
    n = [jnp.where((sub & 2) != 0, m[2 * k] + pltpu.roll(m[2 * k], 2, axis=0),
                   m[2 * k + 1] + pltpu.roll(m[2 * k + 1], 6, axis=0)) for k in range(2)]
    return jnp.where((sub & 1) != 0, n[0] + pltpu.roll(n[0], 1, axis=0), n[1] + pltpu.roll(n[1], 7, axis=0))


DOWN_TM = 256
DOWN_BLOCK = 32


def _token_tiles(h_ref, t0, sub):
    chunks = [h_ref[pl.ds(t0, SUBLANES), r * LANES:(r + 1) * LANES] for r in range(SUBLANES)]
    tiles = []
    for s in range(SUBLANES):
        tile = None
        for r in range(SUBLANES):
            piece = jnp.broadcast_to(chunks[r][s:s + 1, :], (SUBLANES, LANES))
            tile = piece if tile is None else jnp.where(sub == r, piece, tile)
        tiles.append(tile)
    return tiles


def _peer_down_kernel(*refs):
    row_refs = refs[:SUBLANES]
    h_ref, gate_ref, tbl_ref, o_ref, pend_even, pend_odd = refs[SUBLANES:]
    pend_scr = (pend_even, pend_odd)
    tm = h_ref.shape[0]
    n_blocks = PEER_SEL // DOWN_BLOCK
    assert n_blocks % 2 == 0
    lane = lax.broadcasted_iota(jnp.int32, (SUBLANES, LANES), 1)
    sub = lax.broadcasted_iota(jnp.int32, (SUBLANES, LANES), 0)

    def group(gi, carry):
        t0 = pl.multiple_of(gi * SUBLANES, SUBLANES)
        hs = []
        for s, tile in enumerate(_token_tiles(h_ref, t0, sub)):
            swapped = pltpu.roll(tile, HALF_ROWS, axis=0)
            hs.append((tile, swapped) if _FOLD_POS[s] % 2 == 0 else (swapped, tile))
        gbase = gi * PEER_SEL

        def stash(slot, j0):
            for k in range(DOWN_BLOCK):
                prods = [None] * SUBLANES
                for s in range(SUBLANES):
                    hi, lo = _unpack(tbl_ref[pl.ds(row_refs[s][gbase + j0 + k], SUBLANES), :], DOWN_WHOLE_WORD)
                    prods[_FOLD_POS[s]] = hi * hs[s][0] + lo * hs[s][1]
                for q, tile in enumerate(_merge_halves(prods, sub)):
                    pend_scr[slot][MERGED * k + q] = tile

        def place(accs, slot, j0):
            accs = list(accs)
            for k in range(DOWN_BLOCK):
                for q in range(MERGED):
                    tot = jnp.sum(pend_scr[slot][MERGED * k + q], axis=-1, keepdims=True)
                    accs[q] = jnp.where(lane == j0 + k, tot, accs[q])
            return tuple(accs)

        def block_pair(i, accs):
            j0 = 2 * i * DOWN_BLOCK
            accs = place(accs, 0, j0)
            stash(1, j0 + DOWN_BLOCK)
            accs = place(accs, 1, j0 + DOWN_BLOCK)
            stash(0, j0 + 2 * DOWN_BLOCK)
            return accs

        stash(0, 0)
        accs = lax.fori_loop(0, n_blocks // 2 - 1, block_pair, (jnp.zeros((SUBLANES, LANES), F32),) * MERGED)
        accs = place(accs, 0, PEER_SEL - 2 * DOWN_BLOCK)
        stash(1, PEER_SEL - DOWN_BLOCK)
        accs = place(accs, 1, PEER_SEL - DOWN_BLOCK)
        act = _fold_rows(accs, sub)
        gelu = act * (lax.erf(act / np.sqrt(2).astype(np.float32)) + 1.0) / 2.0
        o_ref[pl.ds(t0, SUBLANES), :] = gate_ref[pl.ds(t0, SUBLANES), :] * gelu
        return carry

    lax.fori_loop(0, tm // SUBLANES, group, 0)


def _peer_down(eidx, h, gate, tbl):
    t, d = h.shape
    tm = DOWN_TM
    low = (jnp.asarray(_FOLD_POS, jnp.int32) % 2 == 0)[None, :, None]
    rows = _load_rows(eidx.reshape(t // SUBLANES, SUBLANES, PEER_SEL), low, tbl.shape[0] // HALF_ROWS - PACK_EB)
    rows = rows.transpose(1, 0, 2).reshape(SUBLANES, -1)
    smem = pl.BlockSpec((tm // SUBLANES * PEER_SEL,), lambda i: (i,), memory_space=pltpu.SMEM)
    return pl.pallas_call(
        _peer_down_kernel,
        grid=(t // tm,),
        in_specs=[smem] * SUBLANES + [
            pl.BlockSpec((tm, d), lambda i: (i, 0)),
            pl.BlockSpec((tm, PEER_SEL), lambda i: (i, 0)),
            pl.BlockSpec(tbl.shape, lambda i: (0, 0), pipeline_mode=pl.Buffered(1))],
        out_specs=pl.BlockSpec((tm, PEER_SEL), lambda i: (i, 0)),
        out_shape=jax.ShapeDtypeStruct((t, PEER_SEL), F32),
        scratch_shapes=[pltpu.VMEM((DOWN_BLOCK * MERGED, SUBLANES, LANES), F32)] * 2,
        compiler_params=_cparams(1),
        name="peer_down",
    )(*[rows[s] for s in range(SUBLANES)], h, gate, tbl)


UP_TM = 128
UP_BLOCK = 16
UP_SPREAD = SUBLANES * UP_BLOCK // PEER_SEL


def _peer_up_kernel(*refs):
    row_refs = refs[:SUBLANES]
    w_ref, x_ref, gate_ref, tbl_ref, o_ref, stage_scr, wt_scr, wv_even, wv_odd = refs[SUBLANES:]
    wv_scr = (wv_even, wv_odd)
    tm, d = o_ref.shape
    n_groups = tm // SUBLANES
    sub = lax.broadcasted_iota(jnp.int32, (SUBLANES, LANES), 0)
    low = sub < HALF_ROWS
    wt_scr[...] = w_ref[...].T

    def spread(t, slot, s):
        lanes = jnp.full((PEER_SEL, LANES), t, jnp.int32)
        wv_scr[slot][s] = jnp.take_along_axis(wt_scr[...], lanes, axis=1, mode="promise_in_bounds")

    def group(g, cur):
        t0 = pl.multiple_of(g * SUBLANES, SUBLANES)
        next_t0 = jnp.minimum(g + 1, n_groups - 1) * SUBLANES
        gbase = g * PEER_SEL

        def block(jb, accs):
            accs = list(accs)
            j0 = jb * UP_BLOCK
            for u in range(UP_SPREAD):
                s_next = jb * UP_SPREAD + u
                spread(next_t0 + s_next, 1 - cur, s_next)
            wv_rows = [wv_scr[cur].at[s, pl.ds(pl.multiple_of(j0, UP_BLOCK), UP_BLOCK)] for s in range(SUBLANES)]
            for k in range(0, UP_BLOCK, 2):
                for s in range(SUBLANES):
                    words = jnp.where(low, tbl_ref[pl.ds(row_refs[s][gbase + j0 + k], SUBLANES), :],
                                      tbl_ref[pl.ds(row_refs[s][gbase + j0 + k + 1], SUBLANES), :])
                    wv = jnp.where(low, jnp.broadcast_to(wv_rows[s][k:k + 1, :], (SUBLANES, LANES)),
                                   jnp.broadcast_to(wv_rows[s][k + 1:k + 2, :], (SUBLANES, LANES)))
                    hi, lo = _unpack(words, UP_WHOLE_WORD)
                    accs[2 * s] = accs[2 * s] + wv * hi
                    accs[2 * s + 1] = accs[2 * s + 1] + wv * lo
            return tuple(accs)

        zero = jnp.zeros((SUBLANES, LANES), F32)
        accs = lax.fori_loop(0, PEER_SEL // UP_BLOCK, block, (zero,) * (2 * SUBLANES))
        for s in range(SUBLANES):
            hi, lo = accs[2 * s], accs[2 * s + 1]
            stage_scr[s * SUBLANES:(s + 1) * SUBLANES, :] = jnp.where(
                low, hi + pltpu.roll(hi, HALF_ROWS, axis=0), lo + pltpu.roll(lo, HALF_ROWS, axis=0))
        for r in range(SUBLANES):
            cols = slice(r * LANES, (r + 1) * LANES)
            o_ref[pl.ds(t0, SUBLANES), cols] = (x_ref[pl.ds(t0, SUBLANES), cols]
                                                + gate_ref[0, :, cols] * stage_scr[pl.ds(r, SUBLANES, stride=SUBLANES), :])

    def group_pair(gp, carry):
        group(2 * gp, 0)
        group(2 * gp + 1, 1)
        return carry

    for s in range(SUBLANES):
        spread(s, 0, s)
    lax.fori_loop(0, n_groups // 2, group_pair, 0)


def _peer_up(eidx, w, tbl, x, gate):
    b, s, d = x.shape
    t = b * s
    tm = UP_TM
    assert s % tm == 0 and tm == LANES
    rows = _load_rows(eidx, (jnp.arange(PEER_SEL) % 2 == 0)[None, :], tbl.shape[0] // HALF_ROWS - PACK_EB)
    rows = rows.reshape(t // SUBLANES, SUBLANES, PEER_SEL).transpose(1, 0, 2).reshape(SUBLANES, -1)
    smem = pl.BlockSpec((tm // SUBLANES * PEER_SEL,), lambda i: (i,), memory_space=pltpu.SMEM)
    tiles_per_batch = s // tm
    out = pl.pallas_call(
        _peer_up_kernel,
        grid=(t // tm,),
        in_specs=[smem] * SUBLANES + [
            pl.BlockSpec((tm, PEER_SEL), lambda i: (i, 0)),
            pl.BlockSpec((tm, d), lambda i: (i, 0)),
            pl.BlockSpec((1, 1, d), lambda i: (i // tiles_per_batch, 0, 0)),
            pl.BlockSpec(tbl.shape, lambda i: (0, 0), pipeline_mode=pl.Buffered(1))],
        out_specs=pl.BlockSpec((tm, d), lambda i: (i, 0)),
        out_shape=jax.ShapeDtypeStruct((t, d), F32),
        scratch_shapes=[pltpu.VMEM((SUBLANES * SUBLANES, LANES), F32),
                        pltpu.VMEM((PEER_SEL, tm), F32),
                        pltpu.VMEM((SUBLANES, PEER_SEL, LANES), F32), pltpu.VMEM((SUBLANES, PEER_SEL, LANES), F32)],
        compiler_params=_cparams(1),
        name="peer_up",
    )(*[rows[u] for u in range(SUBLANES)], w, x.reshape(t, d), gate, tbl)
    return out.reshape(b, s, d)


def _peer_block(x, g, sh, sc, gate2, wq, keys, down, up, layer):
    b, s, d = x.shape
    t = b * s
    h, eidx, gsm = _peer_score(x, g, sh, sc, wq.astype(BF16), keys.astype(BF16))
    eidx = eidx.reshape(t, PEER_SEL)
    w = _peer_down(eidx, h.reshape(t, d), gsm.reshape(t, PEER_SEL), _pack_table(down, layer, DOWN_WHOLE_WORD))
    return _peer_up(eidx, w, _pack_table(up, layer, UP_WHOLE_WORD), x, gate2)


def _rmsnorm_kernel(x_ref, g_ref, o_ref):
    x = x_ref[0]
    o_ref[0] = (x * lax.rsqrt(jnp.mean(x * x, axis=-1, keepdims=True) + EPS)) * g_ref[...]


def _rmsnorm(x, g, tm):
    b, s, d = x.shape
    tile = pl.BlockSpec((1, tm, d), lambda bi, i: (bi, i, 0))
    return pl.pallas_call(
        _rmsnorm_kernel,
        grid=(b, s // tm),
        in_specs=[tile, pl.BlockSpec((1, d), lambda bi, i: (0, 0))],
        out_specs=tile,
        out_shape=jax.ShapeDtypeStruct((b, s, d), F32),
        compiler_params=_cparams(2),
        name="final_rmsnorm",
    )(x, g.reshape(1, d))


def kernel(x, c, ctx, c_ctx, ada_w, ada_b, norm1_g, norm2_g, final_g, ab_w_in, ab_w_out, na_rpb, fn_w,
           cv_w_in, cv_w, cv_w_out, peer_w_q, peer_keys, peer_down, peer_up):
    b, s, d = x.shape
    depth = ada_w.shape[0]
    rows = -(-(b + 1) // SUBLANES) * SUBLANES
    cc = jnp.concatenate([c, c_ctx[None], jnp.zeros((rows - b - 1, d), F32)], axis=0)
    mod = _ada_vectors(cc, ada_w, ada_b)

    def chunks(i, lo, hi):
        m = mod[i, lo:hi].reshape(hi - lo, 1, 6, d)
        return [m[:, :, k] for k in range(6)]

    for i in range(depth):
        sh1, sc1, g1, sh2, sc2, g2 = chunks(i, 0, b)
        if i % 2 == 0:
            e = i // 2
            csh1, csc1 = chunks(i, b, b + 1)[:2]
            w_in = ab_w_in[e].astype(BF16)
            qkv, f = _norm_mod_matmul(x, norm1_g[i], sh1, sc1, w_in, (3 * NA_WIDTH, FN_WIDTH), (BF16, BF16),
                                      512, "ab_in")
            (ctx_kv,) = _norm_mod_matmul(ctx, norm1_g[i], csh1, csc1, w_in[:, NA_WIDTH:3 * NA_WIDTH],
                                         (2 * NA_WIDTH,), (BF16,), ctx.shape[1], "ab_in_ctx")
            a = _attention(qkv, ctx_kv, _attention_bias(na_rpb[e]))
            fm = _fnet(f, fn_w[e])
            x = _proj_residual(a, fm, x, g1, ab_w_out[e].astype(BF16), 512)
        else:
            o = i // 2
            bg, u = _conv_in(x, norm1_g[i], sh1, sc1, cv_w_in[o].astype(BF16), 256)
            x = _conv_out(bg, u, cv_w[o], x, g1, cv_w_out[o].astype(BF16), 512)
        x = _peer_block(x, norm2_g[i], sh2, sc2, g2, peer_w_q[i], peer_keys[i], peer_down, peer_up, i)
    return _rmsnorm(x, final_g, 512)
```

```python
import functools

import jax
import jax.numpy as jnp
import numpy as np
from jax import lax
from jax.experimental import pallas as pl
from jax.experimental.pallas import tpu as pltpu

F32 = jnp.float32
BF16 = jnp.bfloat16
EPS = 1e-6

GRID_W = 64
HEAD_DIM = 64
NA_HEADS = 8
NA_WIDTH = NA_HEADS * HEAD_DIM
KH = 8
KH_MAX = 8
KW = 16
FN_GROUPS = 8
FN_GROUP_DIM = 64
FN_WIDTH = FN_GROUPS * FN_GROUP_DIM
PEER_HEADS = 8
PEER_NKEYS = 128
PEER_DK_HALF = 128
PEER_TOPK = 16
PEER_SEL = PEER_HEADS * PEER_TOPK
LANES = 128
SUBLANES = 8
VMEM_LIMIT = 56 * 1024 * 1024

NEG_INF = float("-inf")


def _cparams(n_axes):
    return pltpu.CompilerParams(dimension_semantics=("arbitrary",) * n_axes, vmem_limit_bytes=VMEM_LIMIT)


def _norm_mod(x, g, sh, sc):
    y = x * lax.rsqrt(jnp.mean(x * x, axis=-1, keepdims=True) + EPS)
    return (y * g) * (1.0 + sc) + sh


def _ada_kernel(c_ref, w_ref, b_ref, o_ref):
    c = c_ref[...]
    s = c / (1.0 + jnp.exp(-c))
    o_ref[0] = jnp.dot(s, w_ref[0], preferred_element_type=F32, precision=lax.Precision.HIGHEST) + b_ref[0]


def _ada_vectors(cc, ada_w, ada_b):
    depth, d, n = ada_w.shape
    rows = cc.shape[0]
    tn = 1536
    return pl.pallas_call(
        _ada_kernel,
        grid=(depth, n // tn),
        in_specs=[pl.BlockSpec((rows, d), lambda i, j: (0, 0)),
                  pl.BlockSpec((1, d, tn), lambda i, j: (i, 0, j)),
                  pl.BlockSpec((1, 1, tn), lambda i, j: (i, 0, j))],
        out_specs=pl.BlockSpec((1, rows, tn), lambda i, j: (i, 0, j)),
        out_shape=jax.ShapeDtypeStruct((depth, rows, n), F32),
        compiler_params=_cparams(2),
        name="ada_vectors",
    )(cc, ada_w, ada_b.reshape(depth, 1, n))


def _nmm_kernel(x_ref, g_ref, sh_ref, sc_ref, w_ref, *o_refs):
    h = _norm_mod(x_ref[0], g_ref[...], sh_ref[0], sc_ref[0])
    acc = jnp.dot(h.astype(BF16), w_ref[...], preferred_element_type=F32)
    off = 0
    for o_ref in o_refs:
        n = o_ref.shape[-1]
        o_ref[0] = acc[:, off:off + n].astype(o_ref.dtype)
        off += n


def _mod_spec(arr):
    d = arr.shape[-1]
    if arr.shape[0] == 1:
        return pl.BlockSpec((1, 1, d), lambda b, i: (0, 0, 0))
    return pl.BlockSpec((1, 1, d), lambda b, i: (b, 0, 0))


def _norm_mod_matmul(x, g, sh, sc, w, splits, dtypes, tm, name):
    b, s, d = x.shape
    n = w.shape[1]
    assert sum(splits) == n
    return pl.pallas_call(
        _nmm_kernel,
        grid=(b, s // tm),
        in_specs=[pl.BlockSpec((1, tm, d), lambda bi, i: (bi, i, 0)),
                  pl.BlockSpec((1, d), lambda bi, i: (0, 0)),
                  _mod_spec(sh), _mod_spec(sc),
                  pl.BlockSpec((d, n), lambda bi, i: (0, 0))],
        out_specs=[pl.BlockSpec((1, tm, k), lambda bi, i: (bi, i, 0)) for k in splits],
        out_shape=[jax.ShapeDtypeStruct((b, s, k), dt) for k, dt in zip(splits, dtypes)],
        compiler_params=_cparams(2),
        name=name,
    )(x, g.reshape(1, d), sh, sc, w)


ATTN_ROWS = 2


def _attn_kernel(qkv_ref, ctx_ref, bias_ref, o_ref):
    rows = qkv_ref.shape[1] // GRID_W
    nt = (((1,), (1,)), ((), ()))
    scale = HEAD_DIM ** -0.5

    def row_step(rp, carry):
        heads = range(NA_HEADS)
        cols = [slice(h * HEAD_DIM, (h + 1) * HEAD_DIM) for h in heads]
        kcols = [slice(NA_WIDTH + h * HEAD_DIM, NA_WIDTH + (h + 1) * HEAD_DIM) for h in heads]
        vcols = [slice(2 * NA_WIDTH + h * HEAD_DIM, 2 * NA_WIDTH + (h + 1) * HEAD_DIM) for h in heads]
        q0, k0, off = [], [], []
        for dr in range(ATTN_ROWS):
            r = rp * ATTN_ROWS + dr
            start = jnp.clip(r - KH // 2, 0, rows - KH)
            off.append(start - r + (KH_MAX - 1))
            q0.append(pl.multiple_of(r * GRID_W, GRID_W))
            k0.append(pl.multiple_of(start * GRID_W, GRID_W))
        items = [(i, h) for i in range(ATTN_ROWS) for h in heads]
        q = [qkv_ref[0, pl.ds(q0[i], GRID_W), cols[h]] for i, h in items]
        s_win = [lax.dot_general(q[n], qkv_ref[0, pl.ds(k0[i], KH * GRID_W), kcols[h]], nt,
                                 preferred_element_type=F32) * scale + bias_ref[h, off[i]]
                 for n, (i, h) in enumerate(items)]
        s_ctx = [lax.dot_general(q[n], ctx_ref[0, :, cols[h]], nt, preferred_element_type=F32) * scale
                 for n, (i, h) in enumerate(items)]
        m = [jnp.maximum(jnp.max(a, axis=-1, keepdims=True), jnp.max(c, axis=-1, keepdims=True))
             for a, c in zip(s_win, s_ctx)]
        p_win = [jnp.exp(a - mm) for a, mm in zip(s_win, m)]
        p_ctx = [jnp.exp(c - mm) for c, mm in zip(s_ctx, m)]
        l = [jnp.sum(a, axis=-1, keepdims=True) + jnp.sum(c, axis=-1, keepdims=True) for a, c in zip(p_win, p_ctx)]
        o = [jnp.dot(p_win[n].astype(BF16), qkv_ref[0, pl.ds(k0[i], KH * GRID_W), vcols[h]],
                     preferred_element_type=F32)
             + jnp.dot(p_ctx[n].astype(BF16), ctx_ref[0, :, kcols[h]], preferred_element_type=F32)
             for n, (i, h) in enumerate(items)]
        outs = [oo / ll for oo, ll in zip(o, l)]
        for i in range(ATTN_ROWS):
            o_ref[0, pl.ds(q0[i], GRID_W), :] = jnp.concatenate(
                outs[i * NA_HEADS:(i + 1) * NA_HEADS], axis=-1).astype(o_ref.dtype)
        return carry

    lax.fori_loop(0, rows // ATTN_ROWS, row_step, 0)


def _attention_bias(rpb):
    cols = np.arange(GRID_W)
    col_start = np.clip(cols - KW // 2, 0, GRID_W - KW)
    in_win = (cols[None, :] >= col_start[:, None]) & (cols[None, :] < col_start[:, None] + KW)
    col_off = np.clip(cols[None, :] - cols[:, None] + KW - 1, 0, 2 * KW - 2)
    onehot = jnp.asarray(col_off[None] == np.arange(2 * KW - 1)[:, None, None], F32)
    by_row = jnp.sum(rpb.astype(F32)[:, :, :, None, None] * onehot[None, None], axis=2)
    by_row = jnp.where(in_win[None, None], by_row, NEG_INF)
    b = jnp.stack([by_row[:, o:o + KH] for o in range(KH)], axis=1)
    return b.transpose(0, 1, 3, 2, 4).reshape(NA_HEADS, KH, GRID_W, KH * GRID_W)


def _attention(qkv, ctx_kv, bias):
    b, s, _ = qkv.shape
    l = ctx_kv.shape[1]
    return pl.pallas_call(
        _attn_kernel,
        grid=(b,),
        in_specs=[pl.BlockSpec((1, s, 3 * NA_WIDTH), lambda i: (i, 0, 0)),
                  pl.BlockSpec((1, l, 2 * NA_WIDTH), lambda i: (i, 0, 0)),
                  pl.BlockSpec(bias.shape, lambda i: (0, 0, 0, 0), pipeline_mode=pl.Buffered(1))],
        out_specs=pl.BlockSpec((1, s, NA_WIDTH), lambda i: (i, 0, 0)),
        out_shape=jax.ShapeDtypeStruct((b, s, NA_WIDTH), BF16),
        compiler_params=_cparams(1),
        name="nbr_attention",
    )(qkv, ctx_kv, bias)


FN_ROW_CHUNK = 256


def _fnet_kernel(f_ref, cn_ref, sn_ref, c64_ref, s64_ref, wbd_ref, o_ref, a_scr, b_scr, xa_scr, xb_scr):
    n = f_ref.shape[1]
    scale = (n * FN_GROUP_DIM) ** -0.5

    @pl.when(pl.program_id(0) == 0)
    def _():
        hi = lax.Precision.HIGHEST
        a_scr[...] = (jnp.dot(c64_ref[...], wbd_ref[...], preferred_element_type=F32, precision=hi) * scale).astype(BF16)
        b_scr[...] = (jnp.dot(s64_ref[...], wbd_ref[...], preferred_element_type=F32, precision=hi) * -scale).astype(BF16)

    x = f_ref[0]
    xa_scr[...] = jnp.dot(x, a_scr[...], preferred_element_type=F32).astype(BF16)
    xb_scr[...] = jnp.dot(x, b_scr[...], preferred_element_type=F32).astype(BF16)

    def chunk(i, carry):
        r0 = pl.multiple_of(i * FN_ROW_CHUNK, FN_ROW_CHUNK)
        y = (jnp.dot(cn_ref[pl.ds(r0, FN_ROW_CHUNK), :], xa_scr[...], preferred_element_type=F32)
             + jnp.dot(sn_ref[pl.ds(r0, FN_ROW_CHUNK), :], xb_scr[...], preferred_element_type=F32))
        o_ref[0, pl.ds(r0, FN_ROW_CHUNK), :] = y.astype(o_ref.dtype)
        return carry

    lax.fori_loop(0, n // FN_ROW_CHUNK, chunk, 0)


def _dft_tables(n):
    k = (np.arange(n)[:, None] * np.arange(n)[None, :]) % n
    ang = 2.0 * np.pi * k.astype(np.float64) / n
    return np.cos(ang), np.sin(ang)


def _fnet(f, fn_w):
    b, s, _ = f.shape
    cn, sn = _dft_tables(s)
    c64, s64 = _dft_tables(FN_GROUP_DIM)
    eye = np.eye(FN_GROUPS)
    c64bd = jnp.asarray(np.kron(eye, c64), F32)
    s64bd = jnp.asarray(np.kron(eye, s64), F32)
    wbd = (jnp.asarray(eye, F32)[:, None, :, None] * fn_w[:, :, None, :]).reshape(FN_WIDTH, FN_WIDTH)
    const = lambda shape: pl.BlockSpec(shape, lambda i: (0, 0), pipeline_mode=pl.Buffered(1))
    return pl.pallas_call(
        _fnet_kernel,
        grid=(b,),
        in_specs=[pl.BlockSpec((1, s, FN_WIDTH), lambda i: (i, 0, 0)),
                  const((s, s)), const((s, s)),
                  const((FN_WIDTH, FN_WIDTH)), const((FN_WIDTH, FN_WIDTH)), const((FN_WIDTH, FN_WIDTH))],
        out_specs=pl.BlockSpec((1, s, FN_WIDTH), lambda i: (i, 0, 0)),
        out_shape=jax.ShapeDtypeStruct((b, s, FN_WIDTH), BF16),
        scratch_shapes=[pltpu.VMEM((FN_WIDTH, FN_WIDTH), BF16), pltpu.VMEM((FN_WIDTH, FN_WIDTH), BF16),
                        pltpu.VMEM((s, FN_WIDTH), BF16), pltpu.VMEM((s, FN_WIDTH), BF16)],
        compiler_params=_cparams(1),
        name="fnet_mix",
    )(f, jnp.asarray(cn, BF16), jnp.asarray(sn, BF16), c64bd, s64bd, wbd)


def _proj_res_kernel(a_ref, f_ref, x_ref, g_ref, w_ref, o_ref):
    ka = a_ref.shape[-1]
    out = (jnp.dot(a_ref[0], w_ref[:ka, :], preferred_element_type=F32)
           + jnp.dot(f_ref[0], w_ref[ka:, :], preferred_element_type=F32))
    o_ref[0] = x_ref[0] + g_ref[0] * out


def _proj_residual(a, f, x, gate, w, tm):
    b, s, d = x.shape
    ka, kf = a.shape[-1], f.shape[-1]
    return pl.pallas_call(
        _proj_res_kernel,
        grid=(b, s // tm),
        in_specs=[pl.BlockSpec((1, tm, ka), lambda bi, i: (bi, i, 0)),
                  pl.BlockSpec((1, tm, kf), lambda bi, i: (bi, i, 0)),
                  pl.BlockSpec((1, tm, d), lambda bi, i: (bi, i, 0)),
                  _mod_spec(gate),
                  pl.BlockSpec((ka + kf, d), lambda bi, i: (0, 0))],
        out_specs=pl.BlockSpec((1, tm, d), lambda bi, i: (bi, i, 0)),
        out_shape=jax.ShapeDtypeStruct((b, s, d), F32),
        compiler_params=_cparams(2),
        name="proj_residual",
    )(a, f, x, gate, w)


def _conv_in_kernel(x_ref, g_ref, sh_ref, sc_ref, w_ref, bg_ref, u_ref):
    d = bg_ref.shape[-1]
    h = _norm_mod(x_ref[0], g_ref[...], sh_ref[0], sc_ref[0])
    acc = jnp.dot(h.astype(BF16), w_ref[...], preferred_element_type=F32)
    bg_ref[0] = acc[:, :d]
    u_ref[0] = acc[:, d:2 * d] * acc[:, 2 * d:]


def _conv_in(x, g, sh, sc, w, tm):
    b, s, d = x.shape
    n = w.shape[1]
    return pl.pallas_call(
        _conv_in_kernel,
        grid=(b, s // tm),
        in_specs=[pl.BlockSpec((1, tm, d), lambda bi, i: (bi, i, 0)),
                  pl.BlockSpec((1, d), lambda bi, i: (0, 0)),
                  _mod_spec(sh), _mod_spec(sc),
                  pl.BlockSpec((d, n), lambda bi, i: (0, 0))],
        out_specs=[pl.BlockSpec((1, tm, d), lambda bi, i: (bi, i, 0))] * 2,
        out_shape=[jax.ShapeDtypeStruct((b, s, d), F32)] * 2,
        compiler_params=_cparams(2),
        name="conv_in",
    )(x, g.reshape(1, d), sh, sc, w)


def _conv_out_kernel(bg_ref, u_ref, up_ref, un_ref, cw_ref, x_ref, g_ref, w_ref, o_ref):
    i = pl.program_id(1)
    last = pl.num_programs(1) - 1
    u = u_ref[0]
    tm = u.shape[0]
    row = lax.broadcasted_iota(jnp.int32, u.shape, 0)
    prev_row = jnp.where(i == 0, 0.0, up_ref[0, SUBLANES - 1:SUBLANES, :])
    next_row = jnp.where(i == last, 0.0, un_ref[0, 0:1, :])
    u_prev = jnp.where(row == 0, prev_row, pltpu.roll(u, 1, axis=0))
    u_next = jnp.where(row == tm - 1, next_row, pltpu.roll(u, tm - 1, axis=0))
    y = cw_ref[0:1, :] * u_prev + cw_ref[1:2, :] * u + cw_ref[2:3, :] * u_next
    z = (bg_ref[0] * y).astype(BF16)
    o_ref[0] = x_ref[0] + g_ref[0] * jnp.dot(z, w_ref[...], preferred_element_type=F32)


def _conv_out(bg, u, cw, x, gate, w, tm):
    b, s, d = x.shape
    hb = tm // SUBLANES
    nhb = s // SUBLANES
    tile = pl.BlockSpec((1, tm, d), lambda bi, i: (bi, i, 0))
    return pl.pallas_call(
        _conv_out_kernel,
        grid=(b, s // tm),
        in_specs=[tile, tile,
                  pl.BlockSpec((1, SUBLANES, d), lambda bi, i: (bi, jnp.maximum(i * hb - 1, 0), 0)),
                  pl.BlockSpec((1, SUBLANES, d), lambda bi, i: (bi, jnp.minimum((i + 1) * hb, nhb - 1), 0)),
                  pl.BlockSpec(cw.shape, lambda bi, i: (0, 0)),
                  tile, _mod_spec(gate),
                  pl.BlockSpec((d, d), lambda bi, i: (0, 0))],
        out_specs=tile,
        out_shape=jax.ShapeDtypeStruct((b, s, d), F32),
        compiler_params=_cparams(2),
        name="conv_out",
    )(bg, u, u, u, cw, x, gate, w)


PEER_TM = 512
KEY_TILES = PEER_NKEYS // SUBLANES
assert PEER_TOPK == 2 * SUBLANES


def _tree(op, xs):
    xs = list(xs)
    while len(xs) > 1:
        xs = [op(xs[i], xs[i + 1]) for i in range(0, len(xs) - 1, 2)] + ([xs[-1]] if len(xs) % 2 else [])
    return xs[0]


def _all_sublanes(op, x):
    for shift in (4, 2, 1):
        x = op(x, pltpu.roll(x, shift, axis=0))
    return x


def _top_keys(problems, sub):
    key_id = [sub + SUBLANES * v for v in range(KEY_TILES)]
    problems = [list(tiles) for tiles in problems]
    out = [[] for _ in problems]
    for _ in range(PEER_TOPK):
        for i, tiles in enumerate(problems):
            m = _all_sublanes(jnp.maximum, _tree(jnp.maximum, tiles))
            idx = _all_sublanes(jnp.minimum, _tree(jnp.minimum, [jnp.where(t == m, k, PEER_NKEYS)
                                                                 for t, k in zip(tiles, key_id)]))
            problems[i] = [jnp.where(k == idx, NEG_INF, t) for t, k in zip(tiles, key_id)]
            out[i].append((m, idx))
    return out


def _rows_of(ranked, sub, which, pick):
    t = ranked[pick(0)][which]
    for r in range(1, SUBLANES):
        t = jnp.where(sub == r, ranked[pick(r)][which], t)
    return t


def _product_key_topk(pairs, sub):
    def tiles(first, second, which):
        lo = _rows_of(second, sub, which, lambda r: r)
        hi = _rows_of(second, sub, which, lambda r: SUBLANES + r)
        quad = _rows_of(second, sub, which, lambda r: r % 4)
        a45 = _rows_of(first, sub, which, lambda r: 4 + r // 4)
        a67 = _rows_of(first, sub, which, lambda r: 6 + r // 4)
        ahi = _rows_of(first, sub, which, lambda r: SUBLANES + r)
        a = [first[k][which] for k in range(4)]
        return (a[0], lo), (a[0], hi), (a[1], lo), (a[2], lo), (a[3], lo), (a45, quad), (a67, quad), (ahi, second[0][which])

    cands = [[x + y for x, y in tiles(f, s, 0)] for f, s in pairs]
    cidxs = [[x * PEER_NKEYS + y for x, y in tiles(f, s, 1)] for f, s in pairs]
    quad_pos = jnp.where(sub < 4, sub, sub + (PEER_TOPK - 4))
    pos = [sub, sub + 8, sub + 16, sub + 32, sub + 48, quad_pos + 64, quad_pos + 96, (sub + 8) * PEER_TOPK]
    out = [[] for _ in pairs]
    for _ in range(PEER_TOPK):
        for i, (cand, cidx) in enumerate(zip(cands, cidxs)):
            m = _all_sublanes(jnp.maximum, _tree(jnp.maximum, cand))
            psel = _all_sublanes(jnp.minimum, _tree(jnp.minimum, [jnp.where(c == m, p, PEER_TOPK * PEER_TOPK)
                                                                  for c, p in zip(cand, pos)]))
            hits = [p == psel for p in pos]
            e = _all_sublanes(jnp.maximum, _tree(jnp.maximum, [jnp.where(hh, x, -1) for hh, x in zip(hits, cidx)]))
            cands[i] = [jnp.where(hh, NEG_INF, c) for hh, c in zip(hits, cand)]
            out[i].append((m, e))
    return out


def _peer_score_kernel(x_ref, g_ref, sh_ref, sc_ref, wq_ref, keys_ref, h_ref, eidx_ref, gate_ref,
                       q_scr, s_scr, e_scr, p_scr):
    tm = x_ref.shape[1]
    nt = (((1,), (1,)), ((), ()))
    h = _norm_mod(x_ref[0], g_ref[...], sh_ref[0], sc_ref[0])
    h_ref[0] = h
    q = jnp.dot(h.astype(BF16), wq_ref[...], preferred_element_type=F32)
    for c in range(2 * PEER_HEADS):
        q_scr[c] = q[:, c * PEER_DK_HALF:(c + 1) * PEER_DK_HALF].astype(BF16)
    sub = lax.broadcasted_iota(jnp.int32, (SUBLANES, LANES), 0)

    def head_step(hd, carry):
        for p in range(2):
            s_scr[p] = lax.dot_general(keys_ref[hd, p], q_scr[2 * hd + p], nt, preferred_element_type=F32)
        row0 = pl.multiple_of(hd * PEER_TOPK, PEER_TOPK)
        groups = tm // LANES
        cols = [slice(lg * LANES, (lg + 1) * LANES) for lg in range(groups)]
        ranked = _top_keys([[s_scr[p, v * SUBLANES:(v + 1) * SUBLANES, cols[lg]] for v in range(KEY_TILES)]
                            for lg in range(groups) for p in range(2)], sub)
        tops = _product_key_topk([(ranked[2 * lg], ranked[2 * lg + 1]) for lg in range(groups)], sub)
        for lg, top in enumerate(tops):
            ex = [(jnp.exp(m - top[0][0]), e) for m, e in top]
            denom = _tree(jnp.add, [v for v, _ in ex])
            for half in range(2):
                pick = lambda r, half=half: half * SUBLANES + r
                rows = pl.ds(row0 + half * SUBLANES, SUBLANES)
                e_scr[rows, cols[lg]] = _rows_of(ex, sub, 1, pick)
                p_scr[rows, cols[lg]] = _rows_of(ex, sub, 0, pick) / denom
        return carry

    lax.fori_loop(0, PEER_HEADS, head_step, 0)
    eidx_ref[0] = e_scr[...].T
    gate_ref[0] = p_scr[...].T


def _peer_score(x, g, sh, sc, wq, keys):
    b, s, d = x.shape
    tm = PEER_TM
    n = wq.shape[1]
    tile = lambda k: pl.BlockSpec((1, tm, k), lambda bi, i: (bi, i, 0))
    return pl.pallas_call(
        _peer_score_kernel,
        grid=(b, s // tm),
        in_specs=[tile(d),
                  pl.BlockSpec((1, d), lambda bi, i: (0, 0)),
                  _mod_spec(sh), _mod_spec(sc),
                  pl.BlockSpec((d, n), lambda bi, i: (0, 0)),
                  pl.BlockSpec(keys.shape, lambda bi, i: (0, 0, 0, 0))],
        out_specs=[tile(d), tile(PEER_SEL), tile(PEER_SEL)],
        out_shape=[jax.ShapeDtypeStruct((b, s, d), F32),
                   jax.ShapeDtypeStruct((b, s, PEER_SEL), jnp.int32),
                   jax.ShapeDtypeStruct((b, s, PEER_SEL), F32)],
        scratch_shapes=[pltpu.VMEM((2 * PEER_HEADS, tm, PEER_DK_HALF), BF16),
                        pltpu.VMEM((2, PEER_NKEYS, tm), F32),
                        pltpu.VMEM((PEER_SEL, tm), jnp.int32),
                        pltpu.VMEM((PEER_SEL, tm), F32)],
        compiler_params=_cparams(2),
        name="peer_score",
    )(x, g.reshape(1, d), sh, sc, wq, keys)


HALF_ROWS = SUBLANES // 2


PACK_EB = 256
DOWN_WHOLE_WORD = True
UP_WHOLE_WORD = False


def _pack_kernel(x_ref, o_ref, *, whole_word):
    x_ref = x_ref.at[0]
    half = x_ref.shape[1] // 2
    is_tail = pl.program_id(0) == pl.num_programs(0) - 1
    row = lax.broadcasted_iota(jnp.int32, (PACK_EB, LANES), 0)
    for c in range(HALF_ROWS):
        lo = x_ref[:, half + c * LANES:half + (c + 1) * LANES].astype(BF16).astype(F32)
        lo_bits = pltpu.bitcast(lo, jnp.int32) >> 16 & 0xFFFF
        hi = x_ref[:, c * LANES:(c + 1) * LANES]
        if whole_word:
            bits = pltpu.bitcast(hi, jnp.int32)
            excess = lo_bits - (bits & 0xFFFF)
            step = jnp.where(excess > 0x8000, -1, jnp.where(excess < -0x8000, 1, 0))
            step = jnp.where((bits & 0x7FFF0000) == 0, jnp.maximum(step, 0), step)
            hi_bits = (bits & jnp.int32(-0x10000)) + (step << 16)
        else:
            hi_bits = pltpu.bitcast(hi.astype(BF16).astype(F32), jnp.int32)
        words = pltpu.bitcast(hi_bits | lo_bits, jnp.uint32)
        tail = jnp.where(row == 1, pltpu.roll(words, 1, axis=0), jnp.uint32(0))
        o_ref[pl.ds(c, PACK_EB, stride=HALF_ROWS), :] = jnp.where(is_tail, tail, words)


def _pack_table(tables, layer, whole_word):
    _, e, d = tables.shape
    assert d // 2 == HALF_ROWS * LANES and e % PACK_EB == 0
    nb = e // PACK_EB
    return pl.pallas_call(
        functools.partial(_pack_kernel, whole_word=whole_word),
        grid=(nb + 1,),
        in_specs=[pl.BlockSpec((1, PACK_EB, d), lambda i: (layer, i % nb, 0))],
        out_specs=pl.BlockSpec((PACK_EB * HALF_ROWS, LANES), lambda i: (i, 0)),
        out_shape=jax.ShapeDtypeStruct(((e + PACK_EB) * HALF_ROWS, LANES), jnp.uint32),
        compiler_params=_cparams(1),
        name="pack_table",
    )(tables)


def _load_rows(eidx, on_low_sublanes, n_experts):
    high_start = jnp.where(eidx == 0, HALF_ROWS * n_experts, HALF_ROWS * (eidx - 1))
    return jnp.where(on_low_sublanes, HALF_ROWS * eidx, high_start)


def _unpack(words, whole_word):
    first = words if whole_word else words & jnp.uint32(0xFFFF0000)
    return (pltpu.bitcast(first, F32), pltpu.bitcast(words << 16, F32))


_FOLD_POS = (6, 2, 4, 0, 7, 3, 5, 1)


MERGED = SUBLANES // 2


def _merge_halves(p, sub):
    return [jnp.where(sub < HALF_ROWS, p[2 * k], p[2 * k + 1]) for k in range(MERGED)]


def _fold_rows(m, sub):
    n = [jnp.where((sub & 2) != 0, m[2 * k] + pltpu.roll(m[2 * k], 2, axis=0),
                   m[2 * k + 1] + pltpu.roll(m[2 * k + 1], 6, axis=0)) for k in range(2)]
    return jnp.where((sub & 1) != 0, n[0] + pltpu.roll(n[0], 1, axis=0), n[1] + pltpu.roll(n[1], 7, axis=0))


DOWN_TM = 256
DOWN_BLOCK = 32


def _token_tiles(h_ref, t0, sub):
    chunks = [h_ref[pl.ds(t0, SUBLANES), r * LANES:(r + 1) * LANES] for r in range(SUBLANES)]
    tiles = []
    for s in range(SUBLANES):
        tile = None
        for r in range(SUBLANES):
            piece = jnp.broadcast_to(chunks[r][s:s + 1, :], (SUBLANES, LANES))
            tile = piece if tile is None else jnp.where(sub == r, piece, tile)
        tiles.append(tile)
    return tiles


def _peer_down_kernel(*refs):
    row_refs = refs[:SUBLANES]
    h_ref, gate_ref, tbl_ref, o_ref, pend_even, pend_odd = refs[SUBLANES:]
    pend_scr = (pend_even, pend_odd)
    tm = h_ref.shape[0]
    n_blocks = PEER_SEL // DOWN_BLOCK
    assert n_blocks % 2 == 0
    lane = lax.broadcasted_iota(jnp.int32, (SUBLANES, LANES), 1)
    sub = lax.broadcasted_iota(jnp.int32, (SUBLANES, LANES), 0)

    def group(gi, carry):
        t0 = pl.multiple_of(gi * SUBLANES, SUBLANES)
        hs = []
        for s, tile in enumerate(_token_tiles(h_ref, t0, sub)):
            swapped = pltpu.roll(tile, HALF_ROWS, axis=0)
            hs.append((tile, swapped) if _FOLD_POS[s] % 2 == 0 else (swapped, tile))
        gbase = gi * PEER_SEL

        def stash(slot, j0):
            for k in range(DOWN_BLOCK):
                prods = [None] * SUBLANES
                for s in range(SUBLANES):
                    hi, lo = _unpack(tbl_ref[pl.ds(row_refs[s][gbase + j0 + k], SUBLANES), :], DOWN_WHOLE_WORD)
                    prods[_FOLD_POS[s]] = hi * hs[s][0] + lo * hs[s][1]
                for q, tile in enumerate(_merge_halves(prods, sub)):
                    pend_scr[slot][MERGED * k + q] = tile

        def place(accs, slot, j0):
            accs = list(accs)
            for k in range(DOWN_BLOCK):
                for q in range(MERGED):
                    tot = jnp.sum(pend_scr[slot][MERGED * k + q], axis=-1, keepdims=True)
                    accs[q] = jnp.where(lane == j0 + k, tot, accs[q])
            return tuple(accs)

        def block_pair(i, accs):
            j0 = 2 * i * DOWN_BLOCK
            accs = place(accs, 0, j0)
            stash(1, j0 + DOWN_BLOCK)
            accs = place(accs, 1, j0 + DOWN_BLOCK)
            stash(0, j0 + 2 * DOWN_BLOCK)
            return accs

        stash(0, 0)
        accs = lax.fori_loop(0, n_blocks // 2 - 1, block_pair, (jnp.zeros((SUBLANES, LANES), F32),) * MERGED)
        accs = place(accs, 0, PEER_SEL - 2 * DOWN_BLOCK)
        stash(1, PEER_SEL - DOWN_BLOCK)
        accs = place(accs, 1, PEER_SEL - DOWN_BLOCK)
        act = _fold_rows(accs, sub)
        gelu = act * (lax.erf(act / np.sqrt(2).astype(np.float32)) + 1.0) / 2.0
        o_ref[pl.ds(t0, SUBLANES), :] = gate_ref[pl.ds(t0, SUBLANES), :] * gelu
        return carry

    lax.fori_loop(0, tm // SUBLANES, group, 0)


def _peer_down(eidx, h, gate, tbl):
    t, d = h.shape
    tm = DOWN_TM
    low = (jnp.asarray(_FOLD_POS, jnp.int32) % 2 == 0)[None, :, None]
    rows = _load_rows(eidx.reshape(t // SUBLANES, SUBLANES, PEER_SEL), low, tbl.shape[0] // HALF_ROWS - PACK_EB)
    rows = rows.transpose(1, 0, 2).reshape(SUBLANES, -1)
    smem = pl.BlockSpec((tm // SUBLANES * PEER_SEL,), lambda i: (i,), memory_space=pltpu.SMEM)
    return pl.pallas_call(
        _peer_down_kernel,
        grid=(t // tm,),
        in_specs=[smem] * SUBLANES + [
            pl.BlockSpec((tm, d), lambda i: (i, 0)),
            pl.BlockSpec((tm, PEER_SEL), lambda i: (i, 0)),
            pl.BlockSpec(tbl.shape, lambda i: (0, 0), pipeline_mode=pl.Buffered(1))],
        out_specs=pl.BlockSpec((tm, PEER_SEL), lambda i: (i, 0)),
        out_shape=jax.ShapeDtypeStruct((t, PEER_SEL), F32),
        scratch_shapes=[pltpu.VMEM((DOWN_BLOCK * MERGED, SUBLANES, LANES), F32)] * 2,
        compiler_params=_cparams(1),
        name="peer_down",
    )(*[rows[s] for s in range(SUBLANES)], h, gate, tbl)


UP_TM = 128
UP_BLOCK = 16
UP_SPREAD = SUBLANES * UP_BLOCK // PEER_SEL


def _peer_up_kernel(*refs):
    row_refs = refs[:SUBLANES]
    w_ref, x_ref, gate_ref, tbl_ref, o_ref, stage_scr, wt_scr, wv_even, wv_odd = refs[SUBLANES:]
    wv_scr = (wv_even, wv_odd)
    tm, d = o_ref.shape
    n_groups = tm // SUBLANES
    sub = lax.broadcasted_iota(jnp.int32, (SUBLANES, LANES), 0)
    low = sub < HALF_ROWS
    wt_scr[...] = w_ref[...].T

    def spread(t, slot, s):
        lanes = jnp.full((PEER_SEL, LANES), t, jnp.int32)
        wv_scr[slot][s] = jnp.take_along_axis(wt_scr[...], lanes, axis=1, mode="promise_in_bounds")

    def group(g, cur):
        t0 = pl.multiple_of(g * SUBLANES, SUBLANES)
        next_t0 = jnp.minimum(g + 1, n_groups - 1) * SUBLANES
        gbase = g * PEER_SEL

        def block(jb, accs):
            accs = list(accs)
            j0 = jb * UP_BLOCK
            for u in range(UP_SPREAD):
                s_next = jb * UP_SPREAD + u
                spread(next_t0 + s_next, 1 - cur, s_next)
            wv_rows = [wv_scr[cur].at[s, pl.ds(pl.multiple_of(j0, UP_BLOCK), UP_BLOCK)] for s in range(SUBLANES)]
            for k in range(0, UP_BLOCK, 2):
                for s in range(SUBLANES):
                    words = jnp.where(low, tbl_ref[pl.ds(row_refs[s][gbase + j0 + k], SUBLANES), :],
                                      tbl_ref[pl.ds(row_refs[s][gbase + j0 + k + 1], SUBLANES), :])
                    wv = jnp.where(low, jnp.broadcast_to(wv_rows[s][k:k + 1, :], (SUBLANES, LANES)),
                                   jnp.broadcast_to(wv_rows[s][k + 1:k + 2, :], (SUBLANES, LANES)))
                    hi, lo = _unpack(words, UP_WHOLE_WORD)
                    accs[2 * s] = accs[2 * s] + wv * hi
                    accs[2 * s + 1] = accs[2 * s + 1] + wv * lo
            return tuple(accs)

        zero = jnp.zeros((SUBLANES, LANES), F32)
        accs = lax.fori_loop(0, PEER_SEL // UP_BLOCK, block, (zero,) * (2 * SUBLANES))
        for s in range(SUBLANES):
            hi, lo = accs[2 * s], accs[2 * s + 1]
            stage_scr[s * SUBLANES:(s + 1) * SUBLANES, :] = jnp.where(
                low, hi + pltpu.roll(hi, HALF_ROWS, axis=0), lo + pltpu.roll(lo, HALF_ROWS, axis=0))
        for r in range(SUBLANES):
            cols = slice(r * LANES, (r + 1) * LANES)
            o_ref[pl.ds(t0, SUBLANES), cols] = (x_ref[pl.ds(t0, SUBLANES), cols]
                                                + gate_ref[0, :, cols] * stage_scr[pl.ds(r, SUBLANES, stride=SUBLANES), :])

    def group_pair(gp, carry):
        group(2 * gp, 0)
        group(2 * gp + 1, 1)
        return carry

    for s in range(SUBLANES):
        spread(s, 0, s)
    lax.fori_loop(0, n_groups // 2, group_pair, 0)


def _peer_up(eidx, w, tbl, x, gate):
    b, s, d = x.shape
    t = b * s
    tm = UP_TM
    assert s % tm == 0 and tm == LANES
    rows = _load_rows(eidx, (jnp.arange(PEER_SEL) % 2 == 0)[None, :], tbl.shape[0] // HALF_ROWS - PACK_EB)
    rows = rows.reshape(t // SUBLANES, SUBLANES, PEER_SEL).transpose(1, 0, 2).reshape(SUBLANES, -1)
    smem = pl.BlockSpec((tm // SUBLANES * PEER_SEL,), lambda i: (i,), memory_space=pltpu.SMEM)
    tiles_per_batch = s // tm
    out = pl.pallas_call(
        _peer_up_kernel,
        grid=(t // tm,),
        in_specs=[smem] * SUBLANES + [
            pl.BlockSpec((tm, PEER_SEL), lambda i: (i, 0)),
            pl.BlockSpec((tm, d), lambda i: (i, 0)),
            pl.BlockSpec((1, 1, d), lambda i: (i // tiles_per_batch, 0, 0)),
            pl.BlockSpec(tbl.shape, lambda i: (0, 0), pipeline_mode=pl.Buffered(1))],
        out_specs=pl.BlockSpec((tm, d), lambda i: (i, 0)),
        out_shape=jax.ShapeDtypeStruct((t, d), F32),
        scratch_shapes=[pltpu.VMEM((SUBLANES * SUBLANES, LANES), F32),
                        pltpu.VMEM((PEER_SEL, tm), F32),
                        pltpu.VMEM((SUBLANES, PEER_SEL, LANES), F32), pltpu.VMEM((SUBLANES, PEER_SEL, LANES), F32)],
        compiler_params=_cparams(1),
        name="peer_up",
    )(*[rows[u] for u in range(SUBLANES)], w, x.reshape(t, d), gate, tbl)
    return out.reshape(b, s, d)


def _peer_block(x, g, sh, sc, gate2, wq, keys, down, up, layer):
    b, s, d = x.shape
    t = b * s
    h, eidx, gsm = _peer_score(x, g, sh, sc, wq.astype(BF16), keys.astype(BF16))
    eidx = eidx.reshape(t, PEER_SEL)
    w = _peer_down(eidx, h.reshape(t, d), gsm.reshape(t, PEER_SEL), _pack_table(down, layer, DOWN_WHOLE_WORD))
    return _peer_up(eidx, w, _pack_table(up, layer, UP_WHOLE_WORD), x, gate2)


def _rmsnorm_kernel(x_ref, g_ref, o_ref):
    x = x_ref[0]
    o_ref[0] = (x * lax.rsqrt(jnp.mean(x * x, axis=-1, keepdims=True) + EPS)) * g_ref[...]


def _rmsnorm(x, g, tm):
    b, s, d = x.shape
    tile = pl.BlockSpec((1, tm, d), lambda bi, i: (bi, i, 0))
    return pl.pallas_call(
        _rmsnorm_kernel,
        grid=(b, s // tm),
        in_specs=[tile, pl.BlockSpec((1, d), lambda bi, i: (0, 0))],
        out_specs=tile,
        out_shape=jax.ShapeDtypeStruct((b, s, d), F32),
        compiler_params=_cparams(2),
        name="final_rmsnorm",
    )(x, g.reshape(1, d))


def kernel(x, c, ctx, c_ctx, ada_w, ada_b, norm1_g, norm2_g, final_g, ab_w_in, ab_w_out, na_rpb, fn_w,
           cv_w_in, cv_w, cv_w_out, peer_w_q, peer_keys, peer_down, peer_up):
    b, s, d = x.shape
    depth = ada_w.shape[0]
    rows = -(-(b + 1) // SUBLANES) * SUBLANES
    cc = jnp.concatenate([c, c_ctx[None], jnp.zeros((rows - b - 1, d), F32)], axis=0)
    mod = _ada_vectors(cc, ada_w, ada_b)

    def chunks(i, lo, hi):
        m = mod[i, lo:hi].reshape(hi - lo, 1, 6, d)
        return [m[:, :, k] for k in range(6)]

    for i in range(depth):
        sh1, sc1, g1, sh2, sc2, g2 = chunks(i, 0, b)
        if i % 2 == 0:
            e = i // 2
            csh1, csc1 = chunks(i, b, b + 1)[:2]
            w_in = ab_w_in[e].astype(BF16)
            qkv, f = _norm_mod_matmul(x, norm1_g[i], sh1, sc1, w_in, (3 * NA_WIDTH, FN_WIDTH), (BF16, BF16),
                                      512, "ab_in")
            (ctx_kv,) = _norm_mod_matmul(ctx, norm1_g[i], csh1, csc1, w_in[:, NA_WIDTH:3 * NA_WIDTH],
                                         (2 * NA_WIDTH,), (BF16,), ctx.shape[1], "ab_in_ctx")
            a = _attention(qkv, ctx_kv, _attention_bias(na_rpb[e]))
            fm = _fnet(f, fn_w[e])
            x = _proj_residual(a, fm, x, g1, ab_w_out[e].astype(BF16), 512)
        else:
            o = i // 2
            bg, u = _conv_in(x, norm1_g[i], sh1, sc1, cv_w_in[o].astype(BF16), 256)
            x = _conv_out(bg, u, cv_w[o], x, g1, cv_w_out[o].astype(BF16), 512)
        x = _peer_block(x, norm2_g[i], sh2, sc2, g2, peer_w_q[i], peer_keys[i], peer_down, peer_up, i)
    return _rmsnorm(x, final_g, 512)
```
